```python
import math
import jax, jax.numpy as jnp
from jax import lax
import numpy as np

D_MODEL = 2048
BATCH = 2
SEQ = 4096
DEPTH = 4

N_MEM = 256
D_FF = 5632
RMS_EPS = 1e-5

WINDOW = 128
HEAD_DIM = 64
N_Q_HEADS = 16
N_KV_HEADS = 4
GQA_REP = N_Q_HEADS // N_KV_HEADS
Q_WIDTH = N_Q_HEADS * HEAD_DIM
KV_WIDTH = N_KV_HEADS * HEAD_DIM

SSM_WIDTH = 1024
SSM_GROUP = 16
SSM_GROUPS = SSM_WIDTH // SSM_GROUP
SSM_STATE = 64
DT_MIN = 1e-3
DT_MAX = 1e-1

MEM_HEADS = 4
MEM_HEAD_DIM = 256
MEM_WIDTH = MEM_HEADS * MEM_HEAD_DIM

N_BRANCHES = 3
IN_SPLITS = (Q_WIDTH, KV_WIDTH, KV_WIDTH, SSM_WIDTH, MEM_WIDTH, N_BRANCHES * D_MODEL)
IN_WIDTH = sum(IN_SPLITS)
NEG_INF = -1e30

kernel_name = "macaron_hybrid_swa_s5_memory_gated"


def rms_norm(x, w):
    xf = x.astype(jnp.float32)
    y = xf * lax.rsqrt(jnp.mean(xf * xf, axis=-1, keepdims=True) + RMS_EPS)
    return (y * w.astype(jnp.float32)).astype(x.dtype)


def swiglu(x, w_in, w_out):
    g, u = jnp.split(x @ w_in, 2, axis=-1)
    return (jax.nn.silu(g) * u) @ w_out


def sliding_window_attention(q, k, v, sinks):
    B, L = q.shape[0], q.shape[1]
    W = WINDOW
    nb = L // W
    qb = q.reshape(B, nb, W, N_KV_HEADS, GQA_REP, HEAD_DIM)
    kb = k.reshape(B, nb, W, N_KV_HEADS, HEAD_DIM)
    vb = v.reshape(B, nb, W, N_KV_HEADS, HEAD_DIM)
    pad = jnp.zeros_like(kb[:, :1])
    kk = jnp.concatenate([jnp.concatenate([pad, kb[:, :-1]], axis=1), kb], axis=2)
    vv = jnp.concatenate([jnp.concatenate([pad, vb[:, :-1]], axis=1), vb], axis=2)
    scale = HEAD_DIM ** -0.5
    s = jnp.einsum('bnqgrd,bnkgd->bngrqk', qb, kk).astype(jnp.float32) * scale
    qi = jnp.arange(W)[:, None]
    kj = jnp.arange(2 * W)[None, :] - W
    rel = qi - kj
    local = (rel >= 0) & (rel < W)
    blk = jnp.arange(nb)[:, None, None]
    valid = local[None] & ((blk * W + kj[None]) >= 0)
    s = jnp.where(valid[None, :, None, None], s, NEG_INF)
    sink = jnp.broadcast_to(
        sinks.astype(jnp.float32).reshape(N_KV_HEADS, GQA_REP)[None, None, :, :, None, None],
        s.shape[:-1] + (1,))
    p = jax.nn.softmax(jnp.concatenate([s, sink], axis=-1), axis=-1)[..., :-1]
    o = jnp.einsum('bngrqk,bnkgd->bnqgrd', p.astype(v.dtype), vv)
    return o.reshape(B, L, Q_WIDTH)


def memory_cross_attention(q, mem_k, mem_v):
    s = jnp.einsum('blhd,bmhd->bhlm', q, mem_k).astype(jnp.float32) * (MEM_HEAD_DIM ** -0.5)
    p = jax.nn.softmax(s, axis=-1)
    o = jnp.einsum('bhlm,bmhd->blhd', p.astype(mem_v.dtype), mem_v)
    return o.reshape(q.shape[0], q.shape[1], MEM_WIDTH)


def s5_ssm(u, lam_re, lam_im, log_dt, b_re, b_im, c_re, c_im, d_skip):
    Bsz, L = u.shape[0], u.shape[1]
    uf = u.astype(jnp.float32).reshape(Bsz, L, SSM_GROUPS, SSM_GROUP)
    lr = jnp.minimum(lam_re.astype(jnp.float32), -1e-4)
    li = lam_im.astype(jnp.float32)
    dt = jnp.exp(log_dt.astype(jnp.float32))[:, None]
    mag = jnp.exp(lr * dt)
    ar = mag * jnp.cos(li * dt)
    ai = mag * jnp.sin(li * dt)
    nr, ni = ar - 1.0, ai
    den = lr * lr + li * li
    kr = (nr * lr + ni * li) / den
    ki = (ni * lr - nr * li) / den
    br, bi = b_re.astype(jnp.float32), b_im.astype(jnp.float32)
    bbr = kr[..., None] * br - ki[..., None] * bi
    bbi = kr[..., None] * bi + ki[..., None] * br
    xr = jnp.einsum('blgc,gpc->blgp', uf, bbr)
    xi = jnp.einsum('blgc,gpc->blgp', uf, bbi)
    a_r = jnp.broadcast_to(ar, xr.shape)
    a_i = jnp.broadcast_to(ai, xi.shape)

    def combine(e1, e2):
        a1r, a1i, b1r, b1i = e1
        a2r, a2i, b2r, b2i = e2
        return (a1r * a2r - a1i * a2i,
                a1r * a2i + a1i * a2r,
                a2r * b1r - a2i * b1i + b2r,
                a2r * b1i + a2i * b1r + b2i)

    _, _, sr, si = lax.associative_scan(combine, (a_r, a_i, xr, xi), axis=1)
    y = (jnp.einsum('blgp,gcp->blgc', sr, c_re.astype(jnp.float32))
         - jnp.einsum('blgp,gcp->blgc', si, c_im.astype(jnp.float32)))
    y = y + d_skip.astype(jnp.float32).reshape(SSM_GROUPS, SSM_GROUP) * uf
    return y.reshape(Bsz, L, SSM_WIDTH).astype(u.dtype)


def setup_inputs(seed: int = 0) -> dict:
    key = jax.random.key(seed)
    ks = iter(jax.random.split(key, 40))

    def nrm(shape, scale):
        return jax.random.normal(next(ks), shape, jnp.float32) * scale

    def gain(shape):
        return 1.0 + nrm(shape, 0.02)

    Lr, G, P, CH = DEPTH, SSM_GROUPS, SSM_STATE, SSM_GROUP
    lam_re = -0.5 + nrm((Lr, G, P), 0.01)
    lam_im = math.pi * jnp.arange(P, dtype=jnp.float32)[None, None, :] + nrm((Lr, G, P), 0.01)
    log_dt = jax.random.uniform(next(ks), (Lr, G), jnp.float32,
                                math.log(DT_MIN), math.log(DT_MAX))
    return {
        "x": nrm((BATCH, SEQ, D_MODEL), 1.0),
        "mem": nrm((BATCH, N_MEM, D_MODEL), 1.0),
        "ffn1_norm": gain((Lr, D_MODEL)),
        "ffn1_w_in": nrm((Lr, D_MODEL, 2 * D_FF), D_MODEL ** -0.5),
        "ffn1_w_out": nrm((Lr, D_FF, D_MODEL), D_FF ** -0.5),
        "mix_norm": gain((Lr, D_MODEL)),
        "mem_norm": gain((Lr, D_MODEL)),
        "w_in": nrm((Lr, D_MODEL, IN_WIDTH), D_MODEL ** -0.5),
        "sinks": nrm((Lr, N_Q_HEADS), 0.5),
        "w_mem_kv": nrm((Lr, D_MODEL, 2 * MEM_WIDTH), D_MODEL ** -0.5),
        "lam_re": lam_re,
        "lam_im": lam_im,
        "log_dt": log_dt,
        "b_re": nrm((Lr, G, P, CH), (2 * CH) ** -0.5),
        "b_im": nrm((Lr, G, P, CH), (2 * CH) ** -0.5),
        "c_re": nrm((Lr, G, CH, P), (2 * P) ** -0.5),
        "c_im": nrm((Lr, G, CH, P), (2 * P) ** -0.5),
        "d_skip": nrm((Lr, SSM_WIDTH), 1.0),
        "w_ssm_glu": nrm((Lr, SSM_WIDTH, 2 * D_MODEL), SSM_WIDTH ** -0.5),
        "w_swa_up": nrm((Lr, Q_WIDTH, D_MODEL), Q_WIDTH ** -0.5),
        "w_mem_up": nrm((Lr, MEM_WIDTH, D_MODEL), MEM_WIDTH ** -0.5),
        "w_out": nrm((Lr, D_MODEL, D_MODEL), D_MODEL ** -0.5),
        "ffn2_norm": gain((Lr, D_MODEL)),
        "ffn2_w_in": nrm((Lr, D_MODEL, 2 * D_FF), D_MODEL ** -0.5),
        "ffn2_w_out": nrm((Lr, D_FF, D_MODEL), D_FF ** -0.5),
        "final_norm": gain((D_MODEL,)),
    }


def reference(x, mem, ffn1_norm, ffn1_w_in, ffn1_w_out, mix_norm, mem_norm, w_in, sinks,
              w_mem_kv, lam_re, lam_im, log_dt, b_re, b_im, c_re, c_im, d_skip, w_ssm_glu,
              w_swa_up, w_mem_up, w_out, ffn2_norm, ffn2_w_in, ffn2_w_out, final_norm):
    B, L = x.shape[0], x.shape[1]
    M = mem.shape[1]
    split_points = list(np.cumsum(IN_SPLITS)[:-1])
    h = x
    for l in range(DEPTH):
        h = h + 0.5 * swiglu(rms_norm(h, ffn1_norm[l]), ffn1_w_in[l], ffn1_w_out[l])

        u = rms_norm(h, mix_norm[l])
        q, k, v, s_in, mq, gates = jnp.split(u @ w_in[l], split_points, axis=-1)

        y_swa = sliding_window_attention(
            q.reshape(B, L, N_Q_HEADS, HEAD_DIM),
            k.reshape(B, L, N_KV_HEADS, HEAD_DIM),
            v.reshape(B, L, N_KV_HEADS, HEAD_DIM),
            sinks[l]) @ w_swa_up[l]

        y_s = s5_ssm(s_in, lam_re[l], lam_im[l], log_dt[l], b_re[l], b_im[l],
                     c_re[l], c_im[l], d_skip[l])
        ga, gb = jnp.split(y_s @ w_ssm_glu[l], 2, axis=-1)
        y_ssm = ga * jax.nn.sigmoid(gb)

        mk, mv = jnp.split(rms_norm(mem, mem_norm[l]) @ w_mem_kv[l], 2, axis=-1)
        y_mem = memory_cross_attention(
            mq.reshape(B, L, MEM_HEADS, MEM_HEAD_DIM),
            mk.reshape(B, M, MEM_HEADS, MEM_HEAD_DIM),
            mv.reshape(B, M, MEM_HEADS, MEM_HEAD_DIM)) @ w_mem_up[l]

        g = jax.nn.sigmoid(gates.reshape(B, L, N_BRANCHES, D_MODEL))
        merged = g[:, :, 0] * y_swa + g[:, :, 1] * y_ssm + g[:, :, 2] * y_mem
        h = h + merged @ w_out[l]

        h = h + 0.5 * swiglu(rms_norm(h, ffn2_norm[l]), ffn2_w_in[l], ffn2_w_out[l])
    return rms_norm(h, final_norm)
```

```python
import functools
import math

import jax
import jax.numpy as jnp
from jax import lax
from jax.experimental import pallas as pl
from jax.experimental.pallas import tpu as pltpu

D_MODEL = 2048
DEPTH = 4
N_MEM = 256
D_FF = 5632
RMS_EPS = 1e-5

WINDOW = 128
HEAD_DIM = 64
N_Q_HEADS = 16
N_KV_HEADS = 4
GQA_REP = N_Q_HEADS // N_KV_HEADS
Q_WIDTH = N_Q_HEADS * HEAD_DIM
KV_WIDTH = N_KV_HEADS * HEAD_DIM

SSM_WIDTH = 1024
SSM_GROUP = 16
SSM_GROUPS = SSM_WIDTH // SSM_GROUP
SSM_STATE = 64
SSM_CHUNK = 16
SSM_FLAT = SSM_CHUNK * SSM_GROUP

MEM_HEADS = 4
MEM_HEAD_DIM = 256
MEM_WIDTH = MEM_HEADS * MEM_HEAD_DIM

N_BRANCHES = 3
NEG_INF = -1e30

PROJ_WIDTH = Q_WIDTH + SSM_WIDTH + MEM_WIDTH + 2 * KV_WIDTH
GATE_OFFSET = Q_WIDTH + 2 * KV_WIDTH + SSM_WIDTH + MEM_WIDTH

VMEM_LIMIT_BYTES = 56 * 1024 * 1024

BF16 = jnp.bfloat16
F32 = jnp.float32


def _params(semantics):
    return pltpu.CompilerParams(dimension_semantics=semantics, vmem_limit_bytes=VMEM_LIMIT_BYTES)


def _rms_normalize(x, w):
    ms = jnp.mean(x * x, axis=-1, keepdims=True)
    return (x * lax.rsqrt(ms + RMS_EPS)) * w


def _sigmoid(x):
    return 1.0 / (1.0 + jnp.exp(-x))


def _ffn_body(h_ref, nw_ref, wg_ref, wu_ref, wo_ref, fw_ref, o_ref, xn_ref, acc_ref, *, apply_final_norm):
    j = pl.program_id(1)

    @pl.when(j == 0)
    def _():
        xn_ref[...] = _rms_normalize(h_ref[...], nw_ref[...]).astype(BF16)
        acc_ref[...] = jnp.zeros_like(acc_ref)

    xn = xn_ref[...]
    g = jnp.dot(xn, wg_ref[...], preferred_element_type=F32)
    u = jnp.dot(xn, wu_ref[...], preferred_element_type=F32)
    a = (g * _sigmoid(g)) * u
    acc_ref[...] += jnp.dot(a.astype(BF16), wo_ref[...], preferred_element_type=F32)

    @pl.when(j == pl.num_programs(1) - 1)
    def _():
        y = h_ref[...] + 0.5 * acc_ref[...]
        if apply_final_norm:
            y = _rms_normalize(y, fw_ref[...])
        o_ref[...] = y


def _ffn(h, norm_w, w_in, w_out, final_w, *, apply_final_norm, tm=512, tf=512):
    n = h.shape[0]
    nf = D_FF // tf
    return pl.pallas_call(
        functools.partial(_ffn_body, apply_final_norm=apply_final_norm),
        grid=(n // tm, nf),
        in_specs=[
            pl.BlockSpec((tm, D_MODEL), lambda i, j: (i, 0)),
            pl.BlockSpec((1, D_MODEL), lambda i, j: (0, 0)),
            pl.BlockSpec((D_MODEL, tf), lambda i, j: (0, j)),
            pl.BlockSpec((D_MODEL, tf), lambda i, j: (0, j + nf)),
            pl.BlockSpec((tf, D_MODEL), lambda i, j: (j, 0)),
            pl.BlockSpec((1, D_MODEL), lambda i, j: (0, 0)),
        ],
        out_specs=pl.BlockSpec((tm, D_MODEL), lambda i, j: (i, 0)),
        out_shape=jax.ShapeDtypeStruct((n, D_MODEL), F32),
        scratch_shapes=[pltpu.VMEM((tm, D_MODEL), BF16), pltpu.VMEM((tm, D_MODEL), F32)],
        compiler_params=_params(("parallel", "arbitrary")),
        name="ffn",
    )(h, norm_w, w_in, w_in, w_out, final_w)


def _proj_body(h_ref, nw_ref, w_ref, o_ref, xn_ref):
    @pl.when(pl.program_id(1) == 0)
    def _():
        xn_ref[...] = _rms_normalize(h_ref[...], nw_ref[...]).astype(BF16)

    o_ref[...] = jnp.dot(xn_ref[...], w_ref[...], preferred_element_type=F32).astype(BF16)


def _norm_proj(h, norm_w, w, *, tm, tn=512):
    n, width = h.shape[0], w.shape[1]
    return pl.pallas_call(
        _proj_body,
        grid=(n // tm, width // tn),
        in_specs=[
            pl.BlockSpec((tm, D_MODEL), lambda i, j: (i, 0)),
            pl.BlockSpec((1, D_MODEL), lambda i, j: (0, 0)),
            pl.BlockSpec((D_MODEL, tn), lambda i, j: (0, j)),
        ],
        out_specs=pl.BlockSpec((tm, tn), lambda i, j: (i, j)),
        out_shape=jax.ShapeDtypeStruct((n, width), BF16),
        scratch_shapes=[pltpu.VMEM((tm, D_MODEL), BF16)],
        compiler_params=_params(("parallel", "arbitrary")),
        name="norm_proj",
    )(h, norm_w, w)


def _swa_body(sinks_ref, q_ref, kvc_ref, kvp_ref, o_ref):
    blk = pl.program_id(1)
    q = q_ref[...]
    kvc = kvc_ref[...]
    kvp = kvp_ref[...]
    qi = lax.broadcasted_iota(jnp.int32, (WINDOW, 2 * WINDOW), 0)
    kj = lax.broadcasted_iota(jnp.int32, (WINDOW, 2 * WINDOW), 1)
    first_key = jnp.where(blk > 0, 0, WINDOW)
    valid = (kj > qi) & (kj <= qi + WINDOW) & (kj >= first_key)
    scale = HEAD_DIM ** -0.5
    for g in range(N_KV_HEADS):
        ks = slice(g * HEAD_DIM, (g + 1) * HEAD_DIM)
        vs = slice(KV_WIDTH + g * HEAD_DIM, KV_WIDTH + (g + 1) * HEAD_DIM)
        k = jnp.concatenate([kvp[:, ks], kvc[:, ks]], axis=0)
        v = jnp.concatenate([kvp[:, vs], kvc[:, vs]], axis=0)
        for r in range(GQA_REP):
            h = g * GQA_REP + r
            qh = q[:, h * HEAD_DIM:(h + 1) * HEAD_DIM]
            s = lax.dot_general(qh, k, (((1,), (1,)), ((), ())), preferred_element_type=F32) * scale
            s = jnp.where(valid, s, NEG_INF)
            sink = sinks_ref[h]
            m = jnp.maximum(jnp.max(s, axis=-1, keepdims=True), sink)
            p = jnp.exp(s - m)
            denom = jnp.sum(p, axis=-1, keepdims=True) + jnp.exp(sink - m)
            o = jnp.dot(p.astype(BF16), v, preferred_element_type=F32) / denom
            o_ref[:, h * HEAD_DIM:(h + 1) * HEAD_DIM] = o.astype(BF16)


def _swa(proj, sinks, batch, seq):
    nb = seq // WINDOW
    kv_col = (PROJ_WIDTH - 2 * KV_WIDTH) // (2 * KV_WIDTH)
    return pl.pallas_call(
        _swa_body,
        grid=(batch, nb),
        in_specs=[
            pl.BlockSpec(memory_space=pltpu.SMEM),
            pl.BlockSpec((WINDOW, Q_WIDTH), lambda b, n: (b * nb + n, 0)),
            pl.BlockSpec((WINDOW, 2 * KV_WIDTH), lambda b, n: (b * nb + n, kv_col)),
            pl.BlockSpec((WINDOW, 2 * KV_WIDTH), lambda b, n: (b * nb + jnp.maximum(n - 1, 0), kv_col)),
        ],
        out_specs=pl.BlockSpec((WINDOW, Q_WIDTH), lambda b, n: (b * nb + n, 0)),
        out_shape=jax.ShapeDtypeStruct((batch * seq, Q_WIDTH), BF16),
        compiler_params=_params(("parallel", "arbitrary")),
        name="swa",
    )(sinks, proj, proj, proj)


def _mem_attn_body(q_ref, kv_ref, o_ref):
    scale = MEM_HEAD_DIM ** -0.5
    for h in range(MEM_HEADS):
        cs = slice(h * MEM_HEAD_DIM, (h + 1) * MEM_HEAD_DIM)
        vs = slice(MEM_WIDTH + h * MEM_HEAD_DIM, MEM_WIDTH + (h + 1) * MEM_HEAD_DIM)
        s = lax.dot_general(q_ref[:, cs], kv_ref[:, cs], (((1,), (1,)), ((), ())),
                            preferred_element_type=F32) * scale
        m = jnp.max(s, axis=-1, keepdims=True)
        p = jnp.exp(s - m)
        denom = jnp.sum(p, axis=-1, keepdims=True)
        o = jnp.dot(p.astype(BF16), kv_ref[:, vs], preferred_element_type=F32) / denom
        o_ref[:, cs] = o.astype(BF16)


def _mem_attn(proj, mem_kv, batch, seq, *, tq=512):
    nq = seq // tq
    mq_col = (Q_WIDTH + SSM_WIDTH) // MEM_WIDTH
    return pl.pallas_call(
        _mem_attn_body,
        grid=(batch, nq),
        in_specs=[
            pl.BlockSpec((tq, MEM_WIDTH), lambda b, i: (b * nq + i, mq_col)),
            pl.BlockSpec((N_MEM, 2 * MEM_WIDTH), lambda b, i: (b, 0)),
        ],
        out_specs=pl.BlockSpec((tq, MEM_WIDTH), lambda b, i: (b * nq + i, 0)),
        out_shape=jax.ShapeDtypeStruct((batch * seq, MEM_WIDTH), BF16),
        compiler_params=_params(("parallel", "arbitrary")),
        name="mem_attn",
    )(proj, mem_kv)


def _ssm_operators(lam_re, lam_im, log_dt, b_re, b_im, c_re, c_im, d_skip):
    hp = lax.Precision.HIGHEST
    t_len, g_n, p_n, ch = SSM_CHUNK, SSM_GROUPS, SSM_STATE, SSM_GROUP
    lr = jnp.minimum(lam_re, -1e-4)
    li = lam_im
    dt = jnp.exp(log_dt)[:, None]
    mag = jnp.exp(lr * dt)
    ar = mag * jnp.cos(li * dt)
    ai = mag * jnp.sin(li * dt)
    nr, ni = ar - 1.0, ai
    den = lr * lr + li * li
    kr = (nr * lr + ni * li) / den
    ki = (ni * lr - nr * li) / den
    bbr = kr[..., None] * b_re - ki[..., None] * b_im
    bbi = kr[..., None] * b_im + ki[..., None] * b_re
    steps = jnp.arange(t_len + 1, dtype=F32)[None, :, None]
    pmag = jnp.exp(steps * (lr * dt)[:, None, :])
    ang = steps * (li * dt)[:, None, :]
    pr = pmag * jnp.cos(ang)
    pi = pmag * jnp.sin(ang)
    wr = pr[:, :t_len, :, None] * bbr[:, None] - pi[:, :t_len, :, None] * bbi[:, None]
    wi = pr[:, :t_len, :, None] * bbi[:, None] + pi[:, :t_len, :, None] * bbr[:, None]
    lagk = (jnp.einsum('gcp,gkpd->gkcd', c_re, wr, precision=hp)
            - jnp.einsum('gcp,gkpd->gkcd', c_im, wi, precision=hp))
    lagk = jnp.concatenate([lagk, jnp.zeros((g_n, 1, ch, ch), F32)], axis=1)
    tt = jnp.arange(t_len)
    lag = tt[None, :] - tt[:, None]
    idx = jnp.where(lag >= 0, lag, t_len)
    toep = lagk[:, idx]
    toep = toep.transpose(0, 1, 4, 2, 3).reshape(g_n, SSM_FLAT, SSM_FLAT)
    inp_r = wr[:, ::-1].transpose(0, 1, 3, 2).reshape(g_n, SSM_FLAT, p_n)
    inp_i = wi[:, ::-1].transpose(0, 1, 3, 2).reshape(g_n, SSM_FLAT, p_n)
    inp = jnp.concatenate([inp_r, inp_i], axis=-1)
    pr1 = pr[:, 1:, None, :]
    pi1 = pi[:, 1:, None, :]
    out_r = c_re[:, None] * pr1 - c_im[:, None] * pi1
    out_i = -(c_re[:, None] * pi1 + c_im[:, None] * pr1)
    out_r = out_r.transpose(0, 3, 1, 2).reshape(g_n, p_n, SSM_FLAT)
    out_i = out_i.transpose(0, 3, 1, 2).reshape(g_n, p_n, SSM_FLAT)
    outp = jnp.concatenate([out_r, out_i], axis=1)
    a1 = jnp.concatenate([pr[:, t_len], pr[:, t_len]], axis=-1)
    a2 = jnp.concatenate([-pi[:, t_len], pi[:, t_len]], axis=-1)
    dflat = jnp.tile(d_skip.reshape(g_n, 1, ch), (1, 1, t_len))
    return toep.astype(BF16), inp.astype(BF16), outp.astype(BF16), a1, a2, dflat


def _ssm_chunk_input_body(u_ref, inp_ref, z_ref):
    z_ref[...] = jnp.dot(u_ref[0], inp_ref[0], preferred_element_type=F32)


def _ssm_chunk_input(u_g, inp):
    n_chunks = u_g.shape[1]
    return pl.pallas_call(
        _ssm_chunk_input_body,
        grid=(SSM_GROUPS,),
        in_specs=[
            pl.BlockSpec((1, n_chunks, SSM_FLAT), lambda g: (g, 0, 0)),
            pl.BlockSpec((1, SSM_FLAT, 2 * SSM_STATE), lambda g: (g, 0, 0)),
        ],
        out_specs=pl.BlockSpec((n_chunks, 2 * SSM_STATE), lambda g: (0, g)),
        out_shape=jax.ShapeDtypeStruct((n_chunks, SSM_GROUPS * 2 * SSM_STATE), F32),
        compiler_params=_params(("parallel",)),
        name="ssm_chunk_input",
    )(u_g, inp)


def _ssm_scan_body(z_ref, a1_ref, a2_ref, s_ref):
    a1 = a1_ref[...]
    a2 = a2_ref[...]
    n_chunks = z_ref.shape[1]

    def step(c, carry):
        v0, v1 = carry
        s_ref[0, c] = v0.astype(BF16)
        z0 = z_ref[0, c]
        z1 = pltpu.roll(z0, SSM_STATE, axis=1)
        return a1 * v0 + a2 * v1 + z0, a1 * v1 - a2 * v0 + z1

    zero = jnp.zeros((SSM_GROUPS, 2 * SSM_STATE), F32)
    lax.fori_loop(0, n_chunks, step, (zero, zero))


def _ssm_scan(z, a1, a2):
    batch, n_chunks = z.shape[0], z.shape[1]
    blk = (1, n_chunks, SSM_GROUPS, 2 * SSM_STATE)
    return pl.pallas_call(
        _ssm_scan_body,
        grid=(batch,),
        in_specs=[
            pl.BlockSpec(blk, lambda b: (b, 0, 0, 0)),
            pl.BlockSpec((SSM_GROUPS, 2 * SSM_STATE), lambda b: (0, 0)),
            pl.BlockSpec((SSM_GROUPS, 2 * SSM_STATE), lambda b: (0, 0)),
        ],
        out_specs=pl.BlockSpec(blk, lambda b: (b, 0, 0, 0)),
        out_shape=jax.ShapeDtypeStruct(z.shape, BF16),
        compiler_params=_params(("parallel",)),
        name="ssm_scan",
    )(z, a1, a2)


def _ssm_output_body(u_ref, toep_ref, s_ref, outp_ref, d_ref, y_ref):
    u = u_ref[0]
    y = jnp.dot(u, toep_ref[0], preferred_element_type=F32)
    y = y + jnp.dot(s_ref[0], outp_ref[0], preferred_element_type=F32)
    y = y + d_ref[0] * u.astype(F32)
    y_ref[0] = y.astype(BF16)


def _ssm_output(u_g, toep, s_g, outp, dflat):
    n_chunks = u_g.shape[1]
    return pl.pallas_call(
        _ssm_output_body,
        grid=(SSM_GROUPS,),
        in_specs=[
            pl.BlockSpec((1, n_chunks, SSM_FLAT), lambda g: (g, 0, 0)),
            pl.BlockSpec((1, SSM_FLAT, SSM_FLAT), lambda g: (g, 0, 0)),
            pl.BlockSpec((1, n_chunks, 2 * SSM_STATE), lambda g: (g, 0, 0)),
            pl.BlockSpec((1, 2 * SSM_STATE, SSM_FLAT), lambda g: (g, 0, 0)),
            pl.BlockSpec((1, 1, SSM_FLAT), lambda g: (g, 0, 0)),
        ],
        out_specs=pl.BlockSpec((1, n_chunks, SSM_FLAT), lambda g: (g, 0, 0)),
        out_shape=jax.ShapeDtypeStruct((SSM_GROUPS, n_chunks, SSM_FLAT), BF16),
        compiler_params=_params(("parallel",)),
        name="ssm_output",
    )(u_g, toep, s_g, outp, dflat)


def _ssm(proj, ops, batch, seq):
    toep, inp, outp, a1, a2, dflat = ops
    n = batch * seq
    n_chunks = n // SSM_CHUNK
    s_in = proj[:, Q_WIDTH:Q_WIDTH + SSM_WIDTH]
    u_g = (s_in.reshape(n_chunks, SSM_CHUNK, SSM_GROUPS, SSM_GROUP)
           .transpose(2, 0, 1, 3).reshape(SSM_GROUPS, n_chunks, SSM_FLAT))
    z = _ssm_chunk_input(u_g, inp)
    z = z.reshape(batch, n_chunks // batch, SSM_GROUPS, 2 * SSM_STATE)
    s_prev = _ssm_scan(z, a1, a2)
    s_g = s_prev.reshape(n_chunks, SSM_GROUPS, 2 * SSM_STATE).transpose(1, 0, 2)
    y_g = _ssm_output(u_g, toep, s_g, outp, dflat)
    return (y_g.reshape(SSM_GROUPS, n_chunks, SSM_CHUNK, SSM_GROUP)
            .transpose(1, 2, 0, 3).reshape(n, SSM_WIDTH))


def _merge_body(h_ref, nw_ref, swa_ref, ssm_ref, mem_ref, wg0_ref, wg1_ref, wg2_ref, wswa_ref,
                wga_ref, wgb_ref, wmem_ref, wo_ref, o_ref, xn_ref, acc_ref):
    j = pl.program_id(1)

    @pl.when(j == 0)
    def _():
        xn_ref[...] = _rms_normalize(h_ref[...], nw_ref[...]).astype(BF16)
        acc_ref[...] = jnp.zeros_like(acc_ref)

    xn = xn_ref[...]
    ys = ssm_ref[...]
    y_swa = jnp.dot(swa_ref[...], wswa_ref[...], preferred_element_type=F32)
    ga = jnp.dot(ys, wga_ref[...], preferred_element_type=F32)
    gb = jnp.dot(ys, wgb_ref[...], preferred_element_type=F32)
    y_ssm = ga * _sigmoid(gb)
    y_mem = jnp.dot(mem_ref[...], wmem_ref[...], preferred_element_type=F32)
    merged = _sigmoid(jnp.dot(xn, wg0_ref[...], preferred_element_type=F32)) * y_swa
    merged += _sigmoid(jnp.dot(xn, wg1_ref[...], preferred_element_type=F32)) * y_ssm
    merged += _sigmoid(jnp.dot(xn, wg2_ref[...], preferred_element_type=F32)) * y_mem
    acc_ref[...] += jnp.dot(merged.astype(BF16), wo_ref[...], preferred_element_type=F32)

    @pl.when(j == pl.num_programs(1) - 1)
    def _():
        o_ref[...] = h_ref[...] + acc_ref[...]


def _merge(h, norm_w, o_swa, y_s, o_mem, w_gates, w_swa_up, w_ssm_glu, w_mem_up, w_out, *, tm=512, tn=256):
    n = h.shape[0]
    nj = D_MODEL // tn
    row = lambda i, j: (i, 0)
    return pl.pallas_call(
        _merge_body,
        grid=(n // tm, nj),
        in_specs=[
            pl.BlockSpec((tm, D_MODEL), row),
            pl.BlockSpec((1, D_MODEL), lambda i, j: (0, 0)),
            pl.BlockSpec((tm, Q_WIDTH), row),
            pl.BlockSpec((tm, SSM_WIDTH), row),
            pl.BlockSpec((tm, MEM_WIDTH), row),
            pl.BlockSpec((D_MODEL, tn), lambda i, j: (0, j)),
            pl.BlockSpec((D_MODEL, tn), lambda i, j: (0, j + nj)),
            pl.BlockSpec((D_MODEL, tn), lambda i, j: (0, j + 2 * nj)),
            pl.BlockSpec((Q_WIDTH, tn), lambda i, j: (0, j)),
            pl.BlockSpec((SSM_WIDTH, tn), lambda i, j: (0, j)),
            pl.BlockSpec((SSM_WIDTH, tn), lambda i, j: (0, j + nj)),
            pl.BlockSpec((MEM_WIDTH, tn), lambda i, j: (0, j)),
            pl.BlockSpec((tn, D_MODEL), lambda i, j: (j, 0)),
        ],
        out_specs=pl.BlockSpec((tm, D_MODEL), row),
        out_shape=jax.ShapeDtypeStruct((n, D_MODEL), F32),
        scratch_shapes=[pltpu.VMEM((tm, D_MODEL), BF16), pltpu.VMEM((tm, D_MODEL), F32)],
        compiler_params=_params(("parallel", "arbitrary")),
        name="merge",
    )(h, norm_w, o_swa, y_s, o_mem, w_gates, w_gates, w_gates, w_swa_up, w_ssm_glu, w_ssm_glu, w_mem_up, w_out)


def kernel(x, mem, ffn1_norm, ffn1_w_in, ffn1_w_out, mix_norm, mem_norm, w_in, sinks, w_mem_kv, lam_re, lam_im, log_dt, b_re, b_im, c_re, c_im, d_skip, w_ssm_glu, w_swa_up, w_mem_up, w_out, ffn2_norm, ffn2_w_in, ffn2_w_out, final_norm):
    batch, seq = x.shape[0], x.shape[1]
    n = batch * seq
    h = x.reshape(n, D_MODEL)
    mem2 = mem.reshape(batch * N_MEM, D_MODEL)
    final_w = final_norm.reshape(1, D_MODEL)
    q_end = Q_WIDTH
    k_end = q_end + KV_WIDTH
    v_end = k_end + KV_WIDTH
    s_end = v_end + SSM_WIDTH
    m_end = s_end + MEM_WIDTH
    for l in range(DEPTH):
        wl = w_in[l]
        w_proj = jnp.concatenate(
            [wl[:, :q_end], wl[:, v_end:s_end], wl[:, s_end:m_end], wl[:, q_end:v_end]], axis=1).astype(BF16)
        w_gates = wl[:, m_end:].astype(BF16)
        ops = _ssm_operators(lam_re[l], lam_im[l], log_dt[l], b_re[l], b_im[l], c_re[l], c_im[l], d_skip[l])

        h = _ffn(h, ffn1_norm[l].reshape(1, D_MODEL), ffn1_w_in[l].astype(BF16), ffn1_w_out[l].astype(BF16),
                 final_w, apply_final_norm=False)
        proj = _norm_proj(h, mix_norm[l].reshape(1, D_MODEL), w_proj, tm=1024)
        mem_kv = _norm_proj(mem2, mem_norm[l].reshape(1, D_MODEL), w_mem_kv[l].astype(BF16), tm=batch * N_MEM)
        o_swa = _swa(proj, sinks[l], batch, seq)
        o_mem = _mem_attn(proj, mem_kv, batch, seq)
        y_s = _ssm(proj, ops, batch, seq)
        h = _merge(h, mix_norm[l].reshape(1, D_MODEL), o_swa, y_s, o_mem, w_gates, w_swa_up[l].astype(BF16),
                   w_ssm_glu[l].astype(BF16), w_mem_up[l].astype(BF16), w_out[l].astype(BF16))
        h = _ffn(h, ffn2_norm[l].reshape(1, D_MODEL), ffn2_w_in[l].astype(BF16), ffn2_w_out[l].astype(BF16),
                 final_w, apply_final_norm=(l == DEPTH - 1))
    return h.reshape(batch, seq, D_MODEL)
```

```python
import functools
import math

import jax
import jax.numpy as jnp
from jax import lax
from jax.experimental import pallas as pl
from jax.experimental.pallas import tpu as pltpu

D_MODEL = 2048
DEPTH = 4
N_MEM = 256
D_FF = 5632
RMS_EPS = 1e-5

WINDOW = 128
HEAD_DIM = 64
N_Q_HEADS = 16
N_KV_HEADS = 4
GQA_REP = N_Q_HEADS // N_KV_HEADS
Q_WIDTH = N_Q_HEADS * HEAD_DIM
KV_WIDTH = N_KV_HEADS * HEAD_DIM

SSM_WIDTH = 1024
SSM_GROUP = 16
SSM_GROUPS = SSM_WIDTH // SSM_GROUP
SSM_STATE = 64
SSM_CHUNK = 16
SSM_FLAT = SSM_CHUNK * SSM_GROUP

MEM_HEADS = 4
MEM_HEAD_DIM = 256
MEM_WIDTH = MEM_HEADS * MEM_HEAD_DIM

N_BRANCHES = 3
NEG_INF = -1e30

QKV_WIDTH = Q_WIDTH + 2 * KV_WIDTH
SSM_OFFSET = QKV_WIDTH
MEMQ_OFFSET = SSM_OFFSET + SSM_WIDTH
GATE_OFFSET = MEMQ_OFFSET + MEM_WIDTH

VMEM_LIMIT_BYTES = 56 * 1024 * 1024

BF16 = jnp.bfloat16
F32 = jnp.float32


def _params(semantics):
    return pltpu.CompilerParams(dimension_semantics=semantics, vmem_limit_bytes=VMEM_LIMIT_BYTES)


def _rms_normalize(x, w):
    ms = jnp.mean(x * x, axis=-1, keepdims=True)
    return (x * lax.rsqrt(ms + RMS_EPS)) * w


def _sigmoid(x):
    return 1.0 / (1.0 + jnp.exp(-x))


FFN_ROW_CHUNK = 512


def _ffn_body(h_ref, nw_ref, wg_ref, wu_ref, wo_ref, fw_ref, o_ref, xn_ref, *, apply_final_norm):
    j = pl.program_id(1)

    @pl.when(j == 0)
    def _():
        h = h_ref[...]
        xn_ref[...] = _rms_normalize(h, nw_ref[...]).astype(BF16)
        o_ref[...] = h

    wg = wg_ref[...].astype(BF16)
    wu = wu_ref[...].astype(BF16)
    wo = wo_ref[...].astype(BF16)
    for r in range(o_ref.shape[0] // FFN_ROW_CHUNK):
        rows = pl.ds(r * FFN_ROW_CHUNK, FFN_ROW_CHUNK)
        xn = xn_ref[rows, :]
        g = jnp.dot(xn, wg, preferred_element_type=F32)
        u = jnp.dot(xn, wu, preferred_element_type=F32)
        a = ((0.5 * g) * _sigmoid(g)) * u
        o_ref[rows, :] += jnp.dot(a.astype(BF16), wo, preferred_element_type=F32)

    if apply_final_norm:
        @pl.when(j == pl.num_programs(1) - 1)
        def _():
            o_ref[...] = _rms_normalize(o_ref[...], fw_ref[...])


def _ffn(h, norm_w, w_in, w_out, final_w, layer, *, apply_final_norm, tm=1024, tf=256):
    n = h.shape[0]
    nf = D_FF // tf
    return pl.pallas_call(
        functools.partial(_ffn_body, apply_final_norm=apply_final_norm),
        grid=(n // tm, nf),
        in_specs=[
            pl.BlockSpec((tm, D_MODEL), lambda i, j: (i, 0), pipeline_mode=pl.Buffered(1)),
            pl.BlockSpec((1, D_MODEL), lambda i, j: (0, 0)),
            pl.BlockSpec((None, D_MODEL, tf), lambda i, j: (layer, 0, j)),
            pl.BlockSpec((None, D_MODEL, tf), lambda i, j: (layer, 0, j + nf)),
            pl.BlockSpec((None, tf, D_MODEL), lambda i, j: (layer, j, 0)),
            pl.BlockSpec((1, D_MODEL), lambda i, j: (0, 0)),
        ],
        out_specs=pl.BlockSpec((tm, D_MODEL), lambda i, j: (i, 0)),
        out_shape=jax.ShapeDtypeStruct((n, D_MODEL), F32),
        scratch_shapes=[pltpu.VMEM((tm, D_MODEL), BF16)],
        compiler_params=_params(("parallel", "arbitrary")),
        name="ffn",
    )(h, norm_w, w_in, w_in, w_out, final_w)


def _proj_body(h_ref, nw_ref, w_ref, o_ref, xn_ref):
    @pl.when(pl.program_id(1) == 0)
    def _():
        xn_ref[...] = _rms_normalize(h_ref[...], nw_ref[...]).astype(BF16)

    o_ref[...] = jnp.dot(xn_ref[...], w_ref[...].astype(BF16), preferred_element_type=F32).astype(o_ref.dtype)


def _norm_proj(h, norm_w, w, layer, col0, width, out_dtype, *, tm, tn=512):
    n = h.shape[0]
    c0 = col0 // tn
    return pl.pallas_call(
        _proj_body,
        grid=(n // tm, width // tn),
        in_specs=[
            pl.BlockSpec((tm, D_MODEL), lambda i, j: (i, 0)),
            pl.BlockSpec((1, D_MODEL), lambda i, j: (0, 0)),
            pl.BlockSpec((None, D_MODEL, tn), lambda i, j: (layer, 0, c0 + j)),
        ],
        out_specs=pl.BlockSpec((tm, tn), lambda i, j: (i, j)),
        out_shape=jax.ShapeDtypeStruct((n, width), out_dtype),
        scratch_shapes=[pltpu.VMEM((tm, D_MODEL), BF16)],
        compiler_params=_params(("parallel", "arbitrary")),
        name="norm_proj",
    )(h, norm_w, w)


def _swa_body(sinks_ref, q_ref, kvc_ref, kvp_ref, o_ref):
    blk = pl.program_id(1)
    q = q_ref[...]
    kvc = kvc_ref[...]
    kvp = kvp_ref[...]
    qi = lax.broadcasted_iota(jnp.int32, (WINDOW, 2 * WINDOW), 0)
    kj = lax.broadcasted_iota(jnp.int32, (WINDOW, 2 * WINDOW), 1)
    first_key = jnp.where(blk > 0, 0, WINDOW)
    valid = (kj > qi) & (kj <= qi + WINDOW) & (kj >= first_key)
    scale = HEAD_DIM ** -0.5
    for g in range(N_KV_HEADS):
        ks = slice(g * HEAD_DIM, (g + 1) * HEAD_DIM)
        vs = slice(KV_WIDTH + g * HEAD_DIM, KV_WIDTH + (g + 1) * HEAD_DIM)
        k = jnp.concatenate([kvp[:, ks], kvc[:, ks]], axis=0)
        v = jnp.concatenate([kvp[:, vs], kvc[:, vs]], axis=0)
        for r in range(GQA_REP):
            h = g * GQA_REP + r
            qh = q[:, h * HEAD_DIM:(h + 1) * HEAD_DIM]
            s = lax.dot_general(qh, k, (((1,), (1,)), ((), ())), preferred_element_type=F32) * scale
            s = jnp.where(valid, s, NEG_INF)
            sink = sinks_ref[h]
            m = jnp.maximum(jnp.max(s, axis=-1, keepdims=True), sink)
            p = jnp.exp(s - m)
            denom = jnp.sum(p, axis=-1, keepdims=True) + jnp.exp(sink - m)
            o = jnp.dot(p.astype(BF16), v, preferred_element_type=F32) / denom
            o_ref[:, h * HEAD_DIM:(h + 1) * HEAD_DIM] = o.astype(BF16)


def _swa(qkv, sinks, batch, seq):
    nb = seq // WINDOW
    kv_col = Q_WIDTH // (2 * KV_WIDTH)
    return pl.pallas_call(
        _swa_body,
        grid=(batch, nb),
        in_specs=[
            pl.BlockSpec(memory_space=pltpu.SMEM),
            pl.BlockSpec((WINDOW, Q_WIDTH), lambda b, n: (b * nb + n, 0)),
            pl.BlockSpec((WINDOW, 2 * KV_WIDTH), lambda b, n: (b * nb + n, kv_col)),
            pl.BlockSpec((WINDOW, 2 * KV_WIDTH), lambda b, n: (b * nb + jnp.maximum(n - 1, 0), kv_col)),
        ],
        out_specs=pl.BlockSpec((WINDOW, Q_WIDTH), lambda b, n: (b * nb + n, 0)),
        out_shape=jax.ShapeDtypeStruct((batch * seq, Q_WIDTH), BF16),
        compiler_params=_params(("parallel", "arbitrary")),
        name="swa",
    )(sinks, qkv, qkv, qkv)


def _mem_attn_body(q_ref, kv_ref, o_ref):
    scale = MEM_HEAD_DIM ** -0.5
    for h in range(MEM_HEADS):
        cs = slice(h * MEM_HEAD_DIM, (h + 1) * MEM_HEAD_DIM)
        vs = slice(MEM_WIDTH + h * MEM_HEAD_DIM, MEM_WIDTH + (h + 1) * MEM_HEAD_DIM)
        s = lax.dot_general(q_ref[:, cs], kv_ref[:, cs], (((1,), (1,)), ((), ())),
                            preferred_element_type=F32) * scale
        m = jnp.max(s, axis=-1, keepdims=True)
        p = jnp.exp(s - m)
        denom = jnp.sum(p, axis=-1, keepdims=True)
        o = jnp.dot(p.astype(BF16), kv_ref[:, vs], preferred_element_type=F32) / denom
        o_ref[:, cs] = o.astype(BF16)


def _mem_attn(mq, mem_kv, batch, seq, *, tq=512):
    nq = seq // tq
    return pl.pallas_call(
        _mem_attn_body,
        grid=(batch, nq),
        in_specs=[
            pl.BlockSpec((tq, MEM_WIDTH), lambda b, i: (b * nq + i, 0)),
            pl.BlockSpec((N_MEM, 2 * MEM_WIDTH), lambda b, i: (b, 0)),
        ],
        out_specs=pl.BlockSpec((tq, MEM_WIDTH), lambda b, i: (b * nq + i, 0)),
        out_shape=jax.ShapeDtypeStruct((batch * seq, MEM_WIDTH), BF16),
        compiler_params=_params(("parallel", "arbitrary")),
        name="mem_attn",
    )(mq, mem_kv)


def _ssm_operators(lam_re, lam_im, log_dt, b_re, b_im, c_re, c_im, d_skip):
    hp = lax.Precision.HIGHEST
    t_len, g_n, p_n, ch = SSM_CHUNK, SSM_GROUPS, SSM_STATE, SSM_GROUP
    lr = jnp.minimum(lam_re, -1e-4)
    li = lam_im
    dt = jnp.exp(log_dt)[:, None]
    mag = jnp.exp(lr * dt)
    ar = mag * jnp.cos(li * dt)
    ai = mag * jnp.sin(li * dt)
    nr, ni = ar - 1.0, ai
    den = lr * lr + li * li
    kr = (nr * lr + ni * li) / den
    ki = (ni * lr - nr * li) / den
    bbr = kr[..., None] * b_re - ki[..., None] * b_im
    bbi = kr[..., None] * b_im + ki[..., None] * b_re
    steps = jnp.arange(t_len + 1, dtype=F32)[None, :, None]
    pmag = jnp.exp(steps * (lr * dt)[:, None, :])
    ang = steps * (li * dt)[:, None, :]
    pr = pmag * jnp.cos(ang)
    pi = pmag * jnp.sin(ang)
    wr = pr[:, :t_len, :, None] * bbr[:, None] - pi[:, :t_len, :, None] * bbi[:, None]
    wi = pr[:, :t_len, :, None] * bbi[:, None] + pi[:, :t_len, :, None] * bbr[:, None]
    lagk = (jnp.einsum('gcp,gkpd->gkcd', c_re, wr, precision=hp)
            - jnp.einsum('gcp,gkpd->gkcd', c_im, wi, precision=hp))
    lagk = jnp.concatenate([lagk, jnp.zeros((g_n, 1, ch, ch), F32)], axis=1)
    tt = jnp.arange(t_len)
    lag = tt[None, :] - tt[:, None]
    idx = jnp.where(lag >= 0, lag, t_len)
    toep = lagk[:, idx]
    toep = toep.transpose(0, 1, 4, 2, 3).reshape(g_n, SSM_FLAT, SSM_FLAT)
    inp_r = wr[:, ::-1].transpose(0, 1, 3, 2).reshape(g_n, SSM_FLAT, p_n)
    inp_i = wi[:, ::-1].transpose(0, 1, 3, 2).reshape(g_n, SSM_FLAT, p_n)
    inp = jnp.concatenate([inp_r, inp_i], axis=-1)
    pr1 = pr[:, 1:, None, :]
    pi1 = pi[:, 1:, None, :]
    out_r = c_re[:, None] * pr1 - c_im[:, None] * pi1
    out_i = -(c_re[:, None] * pi1 + c_im[:, None] * pr1)
    out_r = out_r.transpose(0, 3, 1, 2).reshape(g_n, p_n, SSM_FLAT)
    out_i = out_i.transpose(0, 3, 1, 2).reshape(g_n, p_n, SSM_FLAT)
    outp = jnp.concatenate([out_r, out_i], axis=1)
    a1 = jnp.concatenate([pr[:, t_len], pr[:, t_len]], axis=-1)
    a2 = jnp.concatenate([-pi[:, t_len], pi[:, t_len]], axis=-1)
    dflat = jnp.tile(d_skip.reshape(g_n, 1, ch), (1, 1, t_len))
    return toep.astype(BF16), inp.astype(BF16), outp.astype(BF16), a1, a2, dflat


def _ssm_chunk_input_body(u_ref, inp_ref, z_ref):
    z_ref[...] = jnp.dot(u_ref[0], inp_ref[0], preferred_element_type=F32)


def _ssm_chunk_input(u_g, inp):
    n_chunks = u_g.shape[1]
    return pl.pallas_call(
        _ssm_chunk_input_body,
        grid=(SSM_GROUPS,),
        in_specs=[
            pl.BlockSpec((1, n_chunks, SSM_FLAT), lambda g: (g, 0, 0)),
            pl.BlockSpec((1, SSM_FLAT, 2 * SSM_STATE), lambda g: (g, 0, 0)),
        ],
        out_specs=pl.BlockSpec((n_chunks, 2 * SSM_STATE), lambda g: (0, g)),
        out_shape=jax.ShapeDtypeStruct((n_chunks, SSM_GROUPS * 2 * SSM_STATE), F32),
        compiler_params=_params(("parallel",)),
        name="ssm_chunk_input",
    )(u_g, inp)


def _ssm_scan_body(z_ref, a1_ref, a2_ref, s_ref):
    a1 = a1_ref[...]
    a2 = a2_ref[...]
    n_chunks = z_ref.shape[1]

    def step(c, carry):
        v0, v1 = carry
        s_ref[0, c] = v0.astype(BF16)
        z0 = z_ref[0, c]
        z1 = pltpu.roll(z0, SSM_STATE, axis=1)
        return a1 * v0 + a2 * v1 + z0, a1 * v1 - a2 * v0 + z1

    zero = jnp.zeros((SSM_GROUPS, 2 * SSM_STATE), F32)
    lax.fori_loop(0, n_chunks, step, (zero, zero))


def _ssm_scan(z, a1, a2):
    batch, n_chunks = z.shape[0], z.shape[1]
    blk = (1, n_chunks, SSM_GROUPS, 2 * SSM_STATE)
    return pl.pallas_call(
        _ssm_scan_body,
        grid=(batch,),
        in_specs=[
            pl.BlockSpec(blk, lambda b: (b, 0, 0, 0)),
            pl.BlockSpec((SSM_GROUPS, 2 * SSM_STATE), lambda b: (0, 0)),
            pl.BlockSpec((SSM_GROUPS, 2 * SSM_STATE), lambda b: (0, 0)),
        ],
        out_specs=pl.BlockSpec(blk, lambda b: (b, 0, 0, 0)),
        out_shape=jax.ShapeDtypeStruct(z.shape, BF16),
        compiler_params=_params(("parallel",)),
        name="ssm_scan",
    )(z, a1, a2)


def _ssm_output_body(u_ref, toep_ref, s_ref, outp_ref, d_ref, y_ref):
    u = u_ref[0]
    y = jnp.dot(u, toep_ref[0], preferred_element_type=F32)
    y = y + jnp.dot(s_ref[0], outp_ref[0], preferred_element_type=F32)
    y = y + d_ref[0] * u.astype(F32)
    y_ref[0] = y.astype(BF16)


def _ssm_output(u_g, toep, s_g, outp, dflat):
    n_chunks = u_g.shape[1]
    return pl.pallas_call(
        _ssm_output_body,
        grid=(SSM_GROUPS,),
        in_specs=[
            pl.BlockSpec((1, n_chunks, SSM_FLAT), lambda g: (g, 0, 0)),
            pl.BlockSpec((1, SSM_FLAT, SSM_FLAT), lambda g: (g, 0, 0)),
            pl.BlockSpec((1, n_chunks, 2 * SSM_STATE), lambda g: (g, 0, 0)),
            pl.BlockSpec((1, 2 * SSM_STATE, SSM_FLAT), lambda g: (g, 0, 0)),
            pl.BlockSpec((1, 1, SSM_FLAT), lambda g: (g, 0, 0)),
        ],
        out_specs=pl.BlockSpec((1, n_chunks, SSM_FLAT), lambda g: (g, 0, 0)),
        out_shape=jax.ShapeDtypeStruct((SSM_GROUPS, n_chunks, SSM_FLAT), BF16),
        compiler_params=_params(("parallel",)),
        name="ssm_output",
    )(u_g, toep, s_g, outp, dflat)


def _ssm(s_in, ops, batch, seq):
    toep, inp, outp, a1, a2, dflat = ops
    n = batch * seq
    n_chunks = n // SSM_CHUNK
    u_g = (s_in.reshape(n_chunks, SSM_CHUNK, SSM_GROUPS, SSM_GROUP)
           .transpose(2, 0, 1, 3).reshape(SSM_GROUPS, n_chunks, SSM_FLAT))
    z = _ssm_chunk_input(u_g, inp)
    z = z.reshape(batch, n_chunks // batch, SSM_GROUPS, 2 * SSM_STATE)
    s_prev = _ssm_scan(z, a1, a2)
    s_g = s_prev.reshape(n_chunks, SSM_GROUPS, 2 * SSM_STATE).transpose(1, 0, 2)
    y_g = _ssm_output(u_g, toep, s_g, outp, dflat)
    return (y_g.reshape(SSM_GROUPS, n_chunks, SSM_CHUNK, SSM_GROUP)
            .transpose(1, 2, 0, 3).reshape(n, SSM_WIDTH))


def _merge_body(h_ref, nw_ref, swa_ref, ssm_ref, mem_ref, wg0_ref, wg1_ref, wg2_ref, wswa_ref,
                wga_ref, wgb_ref, wmem_ref, wo_ref, o_ref, xn_ref, acc_ref):
    j = pl.program_id(1)

    @pl.when(j == 0)
    def _():
        xn_ref[...] = _rms_normalize(h_ref[...], nw_ref[...]).astype(BF16)
        acc_ref[...] = jnp.zeros_like(acc_ref)

    xn = xn_ref[...]
    ys = ssm_ref[...]
    y_swa = jnp.dot(swa_ref[...], wswa_ref[...], preferred_element_type=F32)
    ga = jnp.dot(ys, wga_ref[...], preferred_element_type=F32)
    gb = jnp.dot(ys, wgb_ref[...], preferred_element_type=F32)
    y_ssm = ga * _sigmoid(gb)
    y_mem = jnp.dot(mem_ref[...], wmem_ref[...], preferred_element_type=F32)
    merged = _sigmoid(jnp.dot(xn, wg0_ref[...], preferred_element_type=F32)) * y_swa
    merged += _sigmoid(jnp.dot(xn, wg1_ref[...], preferred_element_type=F32)) * y_ssm
    merged += _sigmoid(jnp.dot(xn, wg2_ref[...], preferred_element_type=F32)) * y_mem
    acc_ref[...] += jnp.dot(merged.astype(BF16), wo_ref[...], preferred_element_type=F32)

    @pl.when(j == pl.num_programs(1) - 1)
    def _():
        o_ref[...] = h_ref[...] + acc_ref[...]


def _merge(h, norm_w, o_swa, y_s, o_mem, w_gates, w_swa_up, w_ssm_glu, w_mem_up, w_out, *, tm=512, tn=256):
    n = h.shape[0]
    nj = D_MODEL // tn
    row = lambda i, j: (i, 0)
    return pl.pallas_call(
        _merge_body,
        grid=(n // tm, nj),
        in_specs=[
            pl.BlockSpec((tm, D_MODEL), row),
            pl.BlockSpec((1, D_MODEL), lambda i, j: (0, 0)),
            pl.BlockSpec((tm, Q_WIDTH), row),
            pl.BlockSpec((tm, SSM_WIDTH), row),
            pl.BlockSpec((tm, MEM_WIDTH), row),
            pl.BlockSpec((D_MODEL, tn), lambda i, j: (0, j)),
            pl.BlockSpec((D_MODEL, tn), lambda i, j: (0, j + nj)),
            pl.BlockSpec((D_MODEL, tn), lambda i, j: (0, j + 2 * nj)),
            pl.BlockSpec((Q_WIDTH, tn), lambda i, j: (0, j)),
            pl.BlockSpec((SSM_WIDTH, tn), lambda i, j: (0, j)),
            pl.BlockSpec((SSM_WIDTH, tn), lambda i, j: (0, j + nj)),
            pl.BlockSpec((MEM_WIDTH, tn), lambda i, j: (0, j)),
            pl.BlockSpec((tn, D_MODEL), lambda i, j: (j, 0)),
        ],
        out_specs=pl.BlockSpec((tm, D_MODEL), row),
        out_shape=jax.ShapeDtypeStruct((n, D_MODEL), F32),
        scratch_shapes=[pltpu.VMEM((tm, D_MODEL), BF16), pltpu.VMEM((tm, D_MODEL), F32)],
        compiler_params=_params(("parallel", "arbitrary")),
        name="merge",
    )(h, norm_w, o_swa, y_s, o_mem, w_gates, w_gates, w_gates, w_swa_up, w_ssm_glu, w_ssm_glu, w_mem_up, w_out)


def kernel(x, mem, ffn1_norm, ffn1_w_in, ffn1_w_out, mix_norm, mem_norm, w_in, sinks, w_mem_kv, lam_re, lam_im, log_dt, b_re, b_im, c_re, c_im, d_skip, w_ssm_glu, w_swa_up, w_mem_up, w_out, ffn2_norm, ffn2_w_in, ffn2_w_out, final_norm):
    batch, seq = x.shape[0], x.shape[1]
    n = batch * seq
    h = x.reshape(n, D_MODEL)
    mem2 = mem.reshape(batch * N_MEM, D_MODEL)
    final_w = final_norm.reshape(1, D_MODEL)
    for l in range(DEPTH):
        mix_w = mix_norm[l].reshape(1, D_MODEL)
        ops = _ssm_operators(lam_re[l], lam_im[l], log_dt[l], b_re[l], b_im[l], c_re[l], c_im[l], d_skip[l])

        h = _ffn(h, ffn1_norm[l].reshape(1, D_MODEL), ffn1_w_in, ffn1_w_out, final_w, l, apply_final_norm=False)
        qkv = _norm_proj(h, mix_w, w_in, l, 0, QKV_WIDTH, BF16, tm=1024)
        s_in = _norm_proj(h, mix_w, w_in, l, SSM_OFFSET, SSM_WIDTH, BF16, tm=1024)
        mq = _norm_proj(h, mix_w, w_in, l, MEMQ_OFFSET, MEM_WIDTH, BF16, tm=1024)
        mem_kv = _norm_proj(mem2, mem_norm[l].reshape(1, D_MODEL), w_mem_kv, l, 0, 2 * MEM_WIDTH, BF16,
                            tm=batch * N_MEM)
        o_swa = _swa(qkv, sinks[l], batch, seq)
        o_mem = _mem_attn(mq, mem_kv, batch, seq)
        y_s = _ssm(s_in, ops, batch, seq)
        h = _merge(h, mix_w, o_swa, y_s, o_mem, w_in[l, :, GATE_OFFSET:].astype(BF16), w_swa_up[l].astype(BF16),
                   w_ssm_glu[l].astype(BF16), w_mem_up[l].astype(BF16), w_out[l].astype(BF16))
        h = _ffn(h, ffn2_norm[l].reshape(1, D_MODEL), ffn2_w_in, ffn2_w_out, final_w, l,
                 apply_final_norm=(l == DEPTH - 1))
    return h.reshape(batch, seq, D_MODEL)
```

```python
import functools
import math

import jax
import jax.numpy as jnp
from jax import lax
from jax.experimental import pallas as pl
from jax.experimental.pallas import tpu as pltpu

D_MODEL = 2048
DEPTH = 4
N_MEM = 256
D_FF = 5632
RMS_EPS = 1e-5

WINDOW = 128
HEAD_DIM = 64
N_Q_HEADS = 16
N_KV_HEADS = 4
GQA_REP = N_Q_HEADS // N_KV_HEADS
Q_WIDTH = N_Q_HEADS * HEAD_DIM
KV_WIDTH = N_KV_HEADS * HEAD_DIM

SSM_WIDTH = 1024
SSM_GROUP = 16
SSM_GROUPS = SSM_WIDTH // SSM_GROUP
SSM_STATE = 64
SSM_CHUNK = 16
LANES = 128
SSM_LANE_TILES = SSM_WIDTH // LANES
SSM_TILE_GROUPS = LANES // SSM_GROUP

MEM_HEADS = 4
MEM_HEAD_DIM = 256
MEM_WIDTH = MEM_HEADS * MEM_HEAD_DIM

N_BRANCHES = 3
NEG_INF = -1e30

QKV_WIDTH = Q_WIDTH + 2 * KV_WIDTH
SSM_OFFSET = QKV_WIDTH
MEMQ_OFFSET = SSM_OFFSET + SSM_WIDTH
GATE_OFFSET = MEMQ_OFFSET + MEM_WIDTH

VMEM_LIMIT_BYTES = 56 * 1024 * 1024

BF16 = jnp.bfloat16
F32 = jnp.float32


def _params(semantics):
    return pltpu.CompilerParams(dimension_semantics=semantics, vmem_limit_bytes=VMEM_LIMIT_BYTES)


def _rms_normalize(x, w):
    ms = jnp.mean(x * x, axis=-1, keepdims=True)
    return (x * lax.rsqrt(ms + RMS_EPS)) * w


def _sigmoid(x):
    return 1.0 / (1.0 + jnp.exp(-x))


FFN_ROW_CHUNK = 512


def _ffn_body(h_ref, nw_ref, wg_ref, wu_ref, wo_ref, fw_ref, o_ref, xn_ref, *, apply_final_norm):
    j = pl.program_id(1)

    @pl.when(j == 0)
    def _():
        h = h_ref[...]
        xn_ref[...] = _rms_normalize(h, nw_ref[...]).astype(BF16)
        o_ref[...] = h

    wg = wg_ref[...].astype(BF16)
    wu = wu_ref[...].astype(BF16)
    wo = wo_ref[...].astype(BF16)
    for r in range(o_ref.shape[0] // FFN_ROW_CHUNK):
        rows = pl.ds(r * FFN_ROW_CHUNK, FFN_ROW_CHUNK)
        xn = xn_ref[rows, :]
        g = jnp.dot(xn, wg, preferred_element_type=F32)
        u = jnp.dot(xn, wu, preferred_element_type=F32)
        a = ((0.5 * g) * _sigmoid(g)) * u
        o_ref[rows, :] += jnp.dot(a.astype(BF16), wo, preferred_element_type=F32)

    if apply_final_norm:
        @pl.when(j == pl.num_programs(1) - 1)
        def _():
            o_ref[...] = _rms_normalize(o_ref[...], fw_ref[...])


def _ffn(h, norm_w, w_in, w_out, final_w, layer, *, apply_final_norm, tm=1024, tf=256):
    n = h.shape[0]
    nf = D_FF // tf
    return pl.pallas_call(
        functools.partial(_ffn_body, apply_final_norm=apply_final_norm),
        grid=(n // tm, nf),
        in_specs=[
            pl.BlockSpec((tm, D_MODEL), lambda i, j: (i, 0), pipeline_mode=pl.Buffered(1)),
            pl.BlockSpec((1, D_MODEL), lambda i, j: (0, 0)),
            pl.BlockSpec((None, D_MODEL, tf), lambda i, j: (layer, 0, j)),
            pl.BlockSpec((None, D_MODEL, tf), lambda i, j: (layer, 0, j + nf)),
            pl.BlockSpec((None, tf, D_MODEL), lambda i, j: (layer, j, 0)),
            pl.BlockSpec((1, D_MODEL), lambda i, j: (0, 0)),
        ],
        out_specs=pl.BlockSpec((tm, D_MODEL), lambda i, j: (i, 0)),
        out_shape=jax.ShapeDtypeStruct((n, D_MODEL), F32),
        scratch_shapes=[pltpu.VMEM((tm, D_MODEL), BF16)],
        compiler_params=_params(("parallel", "arbitrary")),
        name="ffn",
    )(h, norm_w, w_in, w_in, w_out, final_w)


def _proj_body(h_ref, nw_ref, w_ref, o_ref, xn_ref):
    @pl.when(pl.program_id(1) == 0)
    def _():
        xn_ref[...] = _rms_normalize(h_ref[...], nw_ref[...]).astype(BF16)

    o_ref[...] = jnp.dot(xn_ref[...], w_ref[...].astype(BF16), preferred_element_type=F32).astype(o_ref.dtype)


def _norm_proj(h, norm_w, w, layer, col0, width, out_dtype, *, tm, tn=512):
    n = h.shape[0]
    c0 = col0 // tn
    return pl.pallas_call(
        _proj_body,
        grid=(n // tm, width // tn),
        in_specs=[
            pl.BlockSpec((tm, D_MODEL), lambda i, j: (i, 0)),
            pl.BlockSpec((1, D_MODEL), lambda i, j: (0, 0)),
            pl.BlockSpec((None, D_MODEL, tn), lambda i, j: (layer, 0, c0 + j)),
        ],
        out_specs=pl.BlockSpec((tm, tn), lambda i, j: (i, j)),
        out_shape=jax.ShapeDtypeStruct((n, width), out_dtype),
        scratch_shapes=[pltpu.VMEM((tm, D_MODEL), BF16)],
        compiler_params=_params(("parallel", "arbitrary")),
        name="norm_proj",
    )(h, norm_w, w)


def _swa_body(sinks_ref, q_ref, kvc_ref, kvp_ref, o_ref):
    blk = pl.program_id(1)
    q = q_ref[...]
    kvc = kvc_ref[...]
    kvp = kvp_ref[...]
    qi = lax.broadcasted_iota(jnp.int32, (WINDOW, 2 * WINDOW), 0)
    kj = lax.broadcasted_iota(jnp.int32, (WINDOW, 2 * WINDOW), 1)
    first_key = jnp.where(blk > 0, 0, WINDOW)
    valid = (kj > qi) & (kj <= qi + WINDOW) & (kj >= first_key)
    scale = HEAD_DIM ** -0.5
    for g in range(N_KV_HEADS):
        ks = slice(g * HEAD_DIM, (g + 1) * HEAD_DIM)
        vs = slice(KV_WIDTH + g * HEAD_DIM, KV_WIDTH + (g + 1) * HEAD_DIM)
        k = jnp.concatenate([kvp[:, ks], kvc[:, ks]], axis=0)
        v = jnp.concatenate([kvp[:, vs], kvc[:, vs]], axis=0)
        for r in range(GQA_REP):
            h = g * GQA_REP + r
            qh = q[:, h * HEAD_DIM:(h + 1) * HEAD_DIM]
            s = lax.dot_general(qh, k, (((1,), (1,)), ((), ())), preferred_element_type=F32) * scale
            s = jnp.where(valid, s, NEG_INF)
            sink = sinks_ref[h]
            m = jnp.maximum(jnp.max(s, axis=-1, keepdims=True), sink)
            p = jnp.exp(s - m)
            denom = jnp.sum(p, axis=-1, keepdims=True) + jnp.exp(sink - m)
            o = jnp.dot(p.astype(BF16), v, preferred_element_type=F32) / denom
            o_ref[:, h * HEAD_DIM:(h + 1) * HEAD_DIM] = o.astype(BF16)


def _swa(qkv, sinks, batch, seq):
    nb = seq // WINDOW
    kv_col = Q_WIDTH // (2 * KV_WIDTH)
    return pl.pallas_call(
        _swa_body,
        grid=(batch, nb),
        in_specs=[
            pl.BlockSpec(memory_space=pltpu.SMEM),
            pl.BlockSpec((WINDOW, Q_WIDTH), lambda b, n: (b * nb + n, 0)),
            pl.BlockSpec((WINDOW, 2 * KV_WIDTH), lambda b, n: (b * nb + n, kv_col)),
            pl.BlockSpec((WINDOW, 2 * KV_WIDTH), lambda b, n: (b * nb + jnp.maximum(n - 1, 0), kv_col)),
        ],
        out_specs=pl.BlockSpec((WINDOW, Q_WIDTH), lambda b, n: (b * nb + n, 0)),
        out_shape=jax.ShapeDtypeStruct((batch * seq, Q_WIDTH), BF16),
        compiler_params=_params(("parallel", "arbitrary")),
        name="swa",
    )(sinks, qkv, qkv, qkv)


def _mem_attn_body(q_ref, kv_ref, o_ref):
    scale = MEM_HEAD_DIM ** -0.5
    for h in range(MEM_HEADS):
        cs = slice(h * MEM_HEAD_DIM, (h + 1) * MEM_HEAD_DIM)
        vs = slice(MEM_WIDTH + h * MEM_HEAD_DIM, MEM_WIDTH + (h + 1) * MEM_HEAD_DIM)
        s = lax.dot_general(q_ref[:, cs], kv_ref[:, cs], (((1,), (1,)), ((), ())),
                            preferred_element_type=F32) * scale
        m = jnp.max(s, axis=-1, keepdims=True)
        p = jnp.exp(s - m)
        denom = jnp.sum(p, axis=-1, keepdims=True)
        o = jnp.dot(p.astype(BF16), kv_ref[:, vs], preferred_element_type=F32) / denom
        o_ref[:, cs] = o.astype(BF16)


def _mem_attn(mq, mem_kv, batch, seq, *, tq=512):
    nq = seq // tq
    return pl.pallas_call(
        _mem_attn_body,
        grid=(batch, nq),
        in_specs=[
            pl.BlockSpec((tq, MEM_WIDTH), lambda b, i: (b * nq + i, 0)),
            pl.BlockSpec((N_MEM, 2 * MEM_WIDTH), lambda b, i: (b, 0)),
        ],
        out_specs=pl.BlockSpec((tq, MEM_WIDTH), lambda b, i: (b * nq + i, 0)),
        out_shape=jax.ShapeDtypeStruct((batch * seq, MEM_WIDTH), BF16),
        compiler_params=_params(("parallel", "arbitrary")),
        name="mem_attn",
    )(mq, mem_kv)


def _ssm_operators(lam_re, lam_im, log_dt, b_re, b_im, c_re, c_im, d_skip):
    hp = lax.Precision.HIGHEST
    t_len, g_n, p_n, ch = SSM_CHUNK, SSM_GROUPS, SSM_STATE, SSM_GROUP
    lr = jnp.minimum(lam_re, -1e-4)
    li = lam_im
    dt = jnp.exp(log_dt)[:, None]
    mag = jnp.exp(lr * dt)
    ar = mag * jnp.cos(li * dt)
    ai = mag * jnp.sin(li * dt)
    nr, ni = ar - 1.0, ai
    den = lr * lr + li * li
    kr = (nr * lr + ni * li) / den
    ki = (ni * lr - nr * li) / den
    bbr = kr[..., None] * b_re - ki[..., None] * b_im
    bbi = kr[..., None] * b_im + ki[..., None] * b_re
    steps = jnp.arange(t_len + 1, dtype=F32)[None, :, None]
    pmag = jnp.exp(steps * (lr * dt)[:, None, :])
    ang = steps * (li * dt)[:, None, :]
    pr = pmag * jnp.cos(ang)
    pi = pmag * jnp.sin(ang)
    wr = pr[:, :t_len, :, None] * bbr[:, None] - pi[:, :t_len, :, None] * bbi[:, None]
    wi = pr[:, :t_len, :, None] * bbi[:, None] + pi[:, :t_len, :, None] * bbr[:, None]
    lagk = (jnp.einsum('gcp,gkpd->gkcd', c_re, wr, precision=hp)
            - jnp.einsum('gcp,gkpd->gkcd', c_im, wi, precision=hp))
    nt, gt = SSM_LANE_TILES, SSM_TILE_GROUPS
    lag_bd = jnp.einsum('jgkoc,gh->jkgcho', lagk.reshape(nt, gt, t_len, ch, ch), jnp.eye(gt, dtype=F32),
                        precision=hp).reshape(nt, t_len, LANES, LANES)

    def rows_tgc(w):
        return (w[:, ::-1].reshape(nt, gt, t_len, p_n, ch).transpose(0, 2, 1, 4, 3)
                .reshape(nt, t_len * LANES, p_n))

    inp = jnp.concatenate([rows_tgc(wr), rows_tgc(wi)], axis=-1)
    pr1 = pr[:, 1:, None, :]
    pi1 = pi[:, 1:, None, :]
    out_r = c_re[:, None] * pr1 - c_im[:, None] * pi1
    out_i = -(c_re[:, None] * pi1 + c_im[:, None] * pr1)

    def cols_tgc(w):
        return (w.reshape(nt, gt, t_len, ch, p_n).transpose(0, 4, 2, 1, 3)
                .reshape(nt, p_n, t_len * LANES))

    outp = jnp.concatenate([cols_tgc(out_r), cols_tgc(out_i)], axis=1)
    a1 = jnp.concatenate([pr[:, t_len], pr[:, t_len]], axis=-1).reshape(nt, gt, 2 * p_n)
    a2 = jnp.concatenate([-pi[:, t_len], pi[:, t_len]], axis=-1).reshape(nt, gt, 2 * p_n)
    return lag_bd, inp, outp, a1, a2, d_skip.reshape(1, SSM_WIDTH)


def _ssm_body(s_ref, lag_ref, inp_ref, outp_ref, a1_ref, a2_ref, d_ref, y_ref,
              u_ref, panel_ref, inpx_ref, outpx_ref, z_ref, zs_ref, sp_ref, *, batch):
    t_len, gt = SSM_CHUNK, SSM_TILE_GROUPS
    n_chunks = u_ref.shape[0]
    per_seq = n_chunks // batch
    flat = t_len * LANES

    for t in range(t_len):
        u_ref[:, t * LANES:(t + 1) * LANES] = s_ref[pl.ds(t, n_chunks, stride=t_len), :].astype(BF16)

    zero_blk = jnp.zeros((LANES, LANES), BF16)
    for r in range(t_len):
        left = lag_ref[t_len - 2 - r].astype(BF16) if r < t_len - 1 else zero_blk
        panel_ref[r * LANES:(r + 1) * LANES, :LANES] = left
        panel_ref[r * LANES:(r + 1) * LANES, LANES:] = lag_ref[t_len - 1 - r].astype(BF16)

    row_group = (lax.broadcasted_iota(jnp.int32, (flat, 1), 0) >> 4) & (gt - 1)
    col_group = (lax.broadcasted_iota(jnp.int32, (1, flat), 1) >> 4) & (gt - 1)
    inp = inp_ref[...]
    outp = outp_ref[...]
    for g in range(gt):
        inpx_ref[:, g * LANES:(g + 1) * LANES] = jnp.where(row_group == g, inp, 0.0).astype(BF16)
        outpx_ref[g * LANES:(g + 1) * LANES, :] = jnp.where(col_group == g, outp, 0.0).astype(BF16)

    z = jnp.dot(u_ref[...], inpx_ref[...], preferred_element_type=F32)
    for g in range(gt):
        zg = z[:, g * LANES:(g + 1) * LANES]
        z_ref[pl.ds(g, n_chunks, stride=gt), :] = zg
        zs_ref[pl.ds(g, n_chunks, stride=gt), :] = pltpu.roll(zg, SSM_STATE, axis=1)

    a1 = a1_ref[...]
    a2 = a2_ref[...]

    def step(c, carry):
        new = []
        for b in range(batch):
            v0, v1 = carry[b]
            row = pl.multiple_of((b * per_seq + c) * gt, gt)
            sp_ref[pl.ds(row, gt), :] = v0
            z0 = z_ref[pl.ds(row, gt), :]
            z1 = zs_ref[pl.ds(row, gt), :]
            new.append((a1 * v0 + a2 * v1 + z0, a1 * v1 - a2 * v0 + z1))
        return tuple(new)

    zero = jnp.zeros((gt, LANES), F32)
    lax.fori_loop(0, per_seq, step, tuple((zero, zero) for _ in range(batch)), unroll=4)

    sp = jnp.concatenate([sp_ref[pl.ds(g, n_chunks, stride=gt), :] for g in range(gt)], axis=1).astype(BF16)
    d2 = jnp.concatenate([d_ref[...], d_ref[...]], axis=1)
    for q in range(t_len // 2):
        cols = slice(2 * q * LANES, (2 * q + 2) * LANES)
        k_len = (2 * q + 2) * LANES
        y = jnp.dot(u_ref[:, :k_len], panel_ref[flat - k_len:, :], preferred_element_type=F32)
        y = y + jnp.dot(sp, outpx_ref[:, cols], preferred_element_type=F32)
        y = y + d2 * u_ref[:, cols].astype(F32)
        y_ref[pl.ds(2 * q, n_chunks, stride=t_len), :] = y[:, :LANES]
        y_ref[pl.ds(2 * q + 1, n_chunks, stride=t_len), :] = y[:, LANES:]


def _ssm(s_in, ops, batch):
    lag_bd, inp, outp, a1, a2, d_row = ops
    n = s_in.shape[0]
    n_chunks = n // SSM_CHUNK
    flat = SSM_CHUNK * LANES
    return pl.pallas_call(
        functools.partial(_ssm_body, batch=batch),
        grid=(SSM_LANE_TILES,),
        in_specs=[
            pl.BlockSpec((n, LANES), lambda j: (0, j)),
            pl.BlockSpec((None, SSM_CHUNK, LANES, LANES), lambda j: (j, 0, 0, 0)),
            pl.BlockSpec((None, flat, 2 * SSM_STATE), lambda j: (j, 0, 0)),
            pl.BlockSpec((None, 2 * SSM_STATE, flat), lambda j: (j, 0, 0)),
            pl.BlockSpec((None, SSM_TILE_GROUPS, 2 * SSM_STATE), lambda j: (j, 0, 0)),
            pl.BlockSpec((None, SSM_TILE_GROUPS, 2 * SSM_STATE), lambda j: (j, 0, 0)),
            pl.BlockSpec((1, LANES), lambda j: (0, j)),
        ],
        out_specs=pl.BlockSpec((n, LANES), lambda j: (0, j)),
        out_shape=jax.ShapeDtypeStruct((n, SSM_WIDTH), F32),
        scratch_shapes=[
            pltpu.VMEM((n_chunks, flat), BF16),
            pltpu.VMEM((flat, 2 * LANES), BF16),
            pltpu.VMEM((flat, SSM_TILE_GROUPS * 2 * SSM_STATE), BF16),
            pltpu.VMEM((SSM_TILE_GROUPS * 2 * SSM_STATE, flat), BF16),
            pltpu.VMEM((n_chunks * SSM_TILE_GROUPS, 2 * SSM_STATE), F32),
            pltpu.VMEM((n_chunks * SSM_TILE_GROUPS, 2 * SSM_STATE), F32),
            pltpu.VMEM((n_chunks * SSM_TILE_GROUPS, 2 * SSM_STATE), F32),
        ],
        compiler_params=_params(("parallel",)),
        name="ssm",
    )(s_in, lag_bd, inp, outp, a1, a2, d_row)


def _merge_body(h_ref, nw_ref, swa_ref, ssm_ref, mem_ref, wg0_ref, wg1_ref, wg2_ref, wswa_ref,
                wga_ref, wgb_ref, wmem_ref, wo_ref, o_ref, xn_ref, acc_ref):
    j = pl.program_id(1)

    @pl.when(j == 0)
    def _():
        xn_ref[...] = _rms_normalize(h_ref[...], nw_ref[...]).astype(BF16)
        acc_ref[...] = jnp.zeros_like(acc_ref)

    xn = xn_ref[...]
    ys = ssm_ref[...].astype(BF16)
    y_swa = jnp.dot(swa_ref[...], wswa_ref[...], preferred_element_type=F32)
    ga = jnp.dot(ys, wga_ref[...], preferred_element_type=F32)
    gb = jnp.dot(ys, wgb_ref[...], preferred_element_type=F32)
    y_ssm = ga * _sigmoid(gb)
    y_mem = jnp.dot(mem_ref[...], wmem_ref[...], preferred_element_type=F32)
    merged = _sigmoid(jnp.dot(xn, wg0_ref[...], preferred_element_type=F32)) * y_swa
    merged += _sigmoid(jnp.dot(xn, wg1_ref[...], preferred_element_type=F32)) * y_ssm
    merged += _sigmoid(jnp.dot(xn, wg2_ref[...], preferred_element_type=F32)) * y_mem
    acc_ref[...] += jnp.dot(merged.astype(BF16), wo_ref[...], preferred_element_type=F32)

    @pl.when(j == pl.num_programs(1) - 1)
    def _():
        o_ref[...] = h_ref[...] + acc_ref[...]


def _merge(h, norm_w, o_swa, y_s, o_mem, w_gates, w_swa_up, w_ssm_glu, w_mem_up, w_out, *, tm=512, tn=256):
    n = h.shape[0]
    nj = D_MODEL // tn
    row = lambda i, j: (i, 0)
    return pl.pallas_call(
        _merge_body,
        grid=(n // tm, nj),
        in_specs=[
            pl.BlockSpec((tm, D_MODEL), row),
            pl.BlockSpec((1, D_MODEL), lambda i, j: (0, 0)),
            pl.BlockSpec((tm, Q_WIDTH), row),
            pl.BlockSpec((tm, SSM_WIDTH), row),
            pl.BlockSpec((tm, MEM_WIDTH), row),
            pl.BlockSpec((D_MODEL, tn), lambda i, j: (0, j)),
            pl.BlockSpec((D_MODEL, tn), lambda i, j: (0, j + nj)),
            pl.BlockSpec((D_MODEL, tn), lambda i, j: (0, j + 2 * nj)),
            pl.BlockSpec((Q_WIDTH, tn), lambda i, j: (0, j)),
            pl.BlockSpec((SSM_WIDTH, tn), lambda i, j: (0, j)),
            pl.BlockSpec((SSM_WIDTH, tn), lambda i, j: (0, j + nj)),
            pl.BlockSpec((MEM_WIDTH, tn), lambda i, j: (0, j)),
            pl.BlockSpec((tn, D_MODEL), lambda i, j: (j, 0)),
        ],
        out_specs=pl.BlockSpec((tm, D_MODEL), row),
        out_shape=jax.ShapeDtypeStruct((n, D_MODEL), F32),
        scratch_shapes=[pltpu.VMEM((tm, D_MODEL), BF16), pltpu.VMEM((tm, D_MODEL), F32)],
        compiler_params=_params(("parallel", "arbitrary")),
        name="merge",
    )(h, norm_w, o_swa, y_s, o_mem, w_gates, w_gates, w_gates, w_swa_up, w_ssm_glu, w_ssm_glu, w_mem_up, w_out)


def kernel(x, mem, ffn1_norm, ffn1_w_in, ffn1_w_out, mix_norm, mem_norm, w_in, sinks, w_mem_kv, lam_re, lam_im, log_dt, b_re, b_im, c_re, c_im, d_skip, w_ssm_glu, w_swa_up, w_mem_up, w_out, ffn2_norm, ffn2_w_in, ffn2_w_out, final_norm):
    batch, seq = x.shape[0], x.shape[1]
    n = batch * seq
    h = x.reshape(n, D_MODEL)
    mem2 = mem.reshape(batch * N_MEM, D_MODEL)
    final_w = final_norm.reshape(1, D_MODEL)
    for l in range(DEPTH):
        mix_w = mix_norm[l].reshape(1, D_MODEL)
        ops = _ssm_operators(lam_re[l], lam_im[l], log_dt[l], b_re[l], b_im[l], c_re[l], c_im[l], d_skip[l])

        h = _ffn(h, ffn1_norm[l].reshape(1, D_MODEL), ffn1_w_in, ffn1_w_out, final_w, l, apply_final_norm=False)
        qkv = _norm_proj(h, mix_w, w_in, l, 0, QKV_WIDTH, BF16, tm=1024)
        s_in = _norm_proj(h, mix_w, w_in, l, SSM_OFFSET, SSM_WIDTH, F32, tm=1024)
        mq = _norm_proj(h, mix_w, w_in, l, MEMQ_OFFSET, MEM_WIDTH, BF16, tm=1024)
        mem_kv = _norm_proj(mem2, mem_norm[l].reshape(1, D_MODEL), w_mem_kv, l, 0, 2 * MEM_WIDTH, BF16,
                            tm=batch * N_MEM)
        o_swa = _swa(qkv, sinks[l], batch, seq)
        o_mem = _mem_attn(mq, mem_kv, batch, seq)
        y_s = _ssm(s_in, ops, batch)
        h = _merge(h, mix_w, o_swa, y_s, o_mem, w_in[l, :, GATE_OFFSET:].astype(BF16), w_swa_up[l].astype(BF16),
                   w_ssm_glu[l].astype(BF16), w_mem_up[l].astype(BF16), w_out[l].astype(BF16))
        h = _ffn(h, ffn2_norm[l].reshape(1, D_MODEL), ffn2_w_in, ffn2_w_out, final_w, l,
                 apply_final_norm=(l == DEPTH - 1))
    return h.reshape(batch, seq, D_MODEL)
```

```python
import functools
import math

import jax
import jax.numpy as jnp
from jax import lax
from jax.experimental import pallas as pl
from jax.experimental.pallas import tpu as pltpu

D_MODEL = 2048
DEPTH = 4
N_MEM = 256
D_FF = 5632
RMS_EPS = 1e-5

WINDOW = 128
HEAD_DIM = 64
N_Q_HEADS = 16
N_KV_HEADS = 4
GQA_REP = N_Q_HEADS // N_KV_HEADS
Q_WIDTH = N_Q_HEADS * HEAD_DIM
KV_WIDTH = N_KV_HEADS * HEAD_DIM

SSM_WIDTH = 1024
SSM_GROUP = 16
SSM_GROUPS = SSM_WIDTH // SSM_GROUP
SSM_STATE = 64
SSM_CHUNK = 16
LANES = 128
SSM_LANE_TILES = SSM_WIDTH // LANES
SSM_TILE_GROUPS = LANES // SSM_GROUP

MEM_HEADS = 4
MEM_HEAD_DIM = 256
MEM_WIDTH = MEM_HEADS * MEM_HEAD_DIM

N_BRANCHES = 3
NEG_INF = -1e30

QKV_WIDTH = Q_WIDTH + 2 * KV_WIDTH
SSM_OFFSET = QKV_WIDTH
MEMQ_OFFSET = SSM_OFFSET + SSM_WIDTH
GATE_OFFSET = MEMQ_OFFSET + MEM_WIDTH

VMEM_LIMIT_BYTES = 56 * 1024 * 1024

BF16 = jnp.bfloat16
F32 = jnp.float32


def _params(semantics):
    return pltpu.CompilerParams(dimension_semantics=semantics, vmem_limit_bytes=VMEM_LIMIT_BYTES)


def _rms_normalize(x, w):
    ms = jnp.mean(x * x, axis=-1, keepdims=True)
    return (x * lax.rsqrt(ms + RMS_EPS)) * w


def _sigmoid(x):
    return 1.0 / (1.0 + jnp.exp(-x))


FFN_ROW_CHUNK = 512


def _ffn_body(h_ref, nw_ref, wg_ref, wu_ref, wo_ref, fw_ref, o_ref, xn_ref, *, apply_final_norm):
    j = pl.program_id(1)

    @pl.when(j == 0)
    def _():
        h = h_ref[...]
        xn_ref[...] = _rms_normalize(h, nw_ref[...]).astype(BF16)
        o_ref[...] = h

    wg = wg_ref[...].astype(BF16)
    wu = wu_ref[...].astype(BF16)
    wo = wo_ref[...].astype(BF16)
    for r in range(o_ref.shape[0] // FFN_ROW_CHUNK):
        rows = pl.ds(r * FFN_ROW_CHUNK, FFN_ROW_CHUNK)
        xn = xn_ref[rows, :]
        g = jnp.dot(xn, wg, preferred_element_type=F32)
        u = jnp.dot(xn, wu, preferred_element_type=F32)
        a = ((0.5 * g) * _sigmoid(g)) * u
        o_ref[rows, :] += jnp.dot(a.astype(BF16), wo, preferred_element_type=F32)

    if apply_final_norm:
        @pl.when(j == pl.num_programs(1) - 1)
        def _():
            o_ref[...] = _rms_normalize(o_ref[...], fw_ref[...])


def _ffn(h, norm_w, w_in, w_out, final_w, layer, *, apply_final_norm, tm=1024, tf=512):
    n = h.shape[0]
    nf = D_FF // tf
    return pl.pallas_call(
        functools.partial(_ffn_body, apply_final_norm=apply_final_norm),
        grid=(n // tm, nf),
        in_specs=[
            pl.BlockSpec((tm, D_MODEL), lambda i, j: (i, 0), pipeline_mode=pl.Buffered(1)),
            pl.BlockSpec((1, D_MODEL), lambda i, j: (0, 0)),
            pl.BlockSpec((None, D_MODEL, tf), lambda i, j: (layer, 0, j)),
            pl.BlockSpec((None, D_MODEL, tf), lambda i, j: (layer, 0, j + nf)),
            pl.BlockSpec((None, tf, D_MODEL), lambda i, j: (layer, j, 0)),
            pl.BlockSpec((1, D_MODEL), lambda i, j: (0, 0)),
        ],
        out_specs=pl.BlockSpec((tm, D_MODEL), lambda i, j: (i, 0), pipeline_mode=pl.Buffered(1)),
        out_shape=jax.ShapeDtypeStruct((n, D_MODEL), F32),
        scratch_shapes=[pltpu.VMEM((tm, D_MODEL), BF16)],
        compiler_params=_params(("parallel", "arbitrary")),
        name="ffn",
    )(h, norm_w, w_in, w_in, w_out, final_w)


def _proj_body(h_ref, nw_ref, w_ref, o_ref, xn_ref):
    @pl.when(pl.program_id(1) == 0)
    def _():
        xn_ref[...] = _rms_normalize(h_ref[...], nw_ref[...]).astype(BF16)

    o_ref[...] = jnp.dot(xn_ref[...], w_ref[...].astype(BF16), preferred_element_type=F32).astype(o_ref.dtype)


def _norm_proj(h, norm_w, w, layer, col0, width, out_dtype, *, tm, tn=512):
    n = h.shape[0]
    c0 = col0 // tn
    return pl.pallas_call(
        _proj_body,
        grid=(n // tm, width // tn),
        in_specs=[
            pl.BlockSpec((tm, D_MODEL), lambda i, j: (i, 0)),
            pl.BlockSpec((1, D_MODEL), lambda i, j: (0, 0)),
            pl.BlockSpec((None, D_MODEL, tn), lambda i, j: (layer, 0, c0 + j)),
        ],
        out_specs=pl.BlockSpec((tm, tn), lambda i, j: (i, j)),
        out_shape=jax.ShapeDtypeStruct((n, width), out_dtype),
        scratch_shapes=[pltpu.VMEM((tm, D_MODEL), BF16)],
        compiler_params=_params(("parallel", "arbitrary")),
        name="norm_proj",
    )(h, norm_w, w)


def _swa_body(sinks_ref, q_ref, kvc_ref, kvp_ref, o_ref):
    blk = pl.program_id(1)
    q = q_ref[...]
    kvc = kvc_ref[...]
    kvp = kvp_ref[...]
    qi = lax.broadcasted_iota(jnp.int32, (WINDOW, 2 * WINDOW), 0)
    kj = lax.broadcasted_iota(jnp.int32, (WINDOW, 2 * WINDOW), 1)
    first_key = jnp.where(blk > 0, 0, WINDOW)
    valid = (kj > qi) & (kj <= qi + WINDOW) & (kj >= first_key)
    scale = HEAD_DIM ** -0.5
    for g in range(N_KV_HEADS):
        ks = slice(g * HEAD_DIM, (g + 1) * HEAD_DIM)
        vs = slice(KV_WIDTH + g * HEAD_DIM, KV_WIDTH + (g + 1) * HEAD_DIM)
        k = jnp.concatenate([kvp[:, ks], kvc[:, ks]], axis=0)
        v = jnp.concatenate([kvp[:, vs], kvc[:, vs]], axis=0)
        for r in range(GQA_REP):
            h = g * GQA_REP + r
            qh = q[:, h * HEAD_DIM:(h + 1) * HEAD_DIM]
            s = lax.dot_general(qh, k, (((1,), (1,)), ((), ())), preferred_element_type=F32) * scale
            s = jnp.where(valid, s, NEG_INF)
            sink = sinks_ref[h]
            m = jnp.maximum(jnp.max(s, axis=-1, keepdims=True), sink)
            p = jnp.exp(s - m)
            denom = jnp.sum(p, axis=-1, keepdims=True) + jnp.exp(sink - m)
            o = jnp.dot(p.astype(BF16), v, preferred_element_type=F32) / denom
            o_ref[:, h * HEAD_DIM:(h + 1) * HEAD_DIM] = o.astype(BF16)


def _swa(qkv, sinks, batch, seq):
    nb = seq // WINDOW
    kv_col = Q_WIDTH // (2 * KV_WIDTH)
    return pl.pallas_call(
        _swa_body,
        grid=(batch, nb),
        in_specs=[
            pl.BlockSpec(memory_space=pltpu.SMEM),
            pl.BlockSpec((WINDOW, Q_WIDTH), lambda b, n: (b * nb + n, 0)),
            pl.BlockSpec((WINDOW, 2 * KV_WIDTH), lambda b, n: (b * nb + n, kv_col)),
            pl.BlockSpec((WINDOW, 2 * KV_WIDTH), lambda b, n: (b * nb + jnp.maximum(n - 1, 0), kv_col)),
        ],
        out_specs=pl.BlockSpec((WINDOW, Q_WIDTH), lambda b, n: (b * nb + n, 0)),
        out_shape=jax.ShapeDtypeStruct((batch * seq, Q_WIDTH), BF16),
        compiler_params=_params(("parallel", "arbitrary")),
        name="swa",
    )(sinks, qkv, qkv, qkv)


def _mem_attn_body(q_ref, kv_ref, o_ref):
    scale = MEM_HEAD_DIM ** -0.5
    for h in range(MEM_HEADS):
        cs = slice(h * MEM_HEAD_DIM, (h + 1) * MEM_HEAD_DIM)
        vs = slice(MEM_WIDTH + h * MEM_HEAD_DIM, MEM_WIDTH + (h + 1) * MEM_HEAD_DIM)
        s = lax.dot_general(q_ref[:, cs], kv_ref[:, cs], (((1,), (1,)), ((), ())),
                            preferred_element_type=F32) * scale
        m = jnp.max(s, axis=-1, keepdims=True)
        p = jnp.exp(s - m)
        denom = jnp.sum(p, axis=-1, keepdims=True)
        o = jnp.dot(p.astype(BF16), kv_ref[:, vs], preferred_element_type=F32) / denom
        o_ref[:, cs] = o.astype(BF16)


def _mem_attn(mq, mem_kv, batch, seq, *, tq=512):
    nq = seq // tq
    return pl.pallas_call(
        _mem_attn_body,
        grid=(batch, nq),
        in_specs=[
            pl.BlockSpec((tq, MEM_WIDTH), lambda b, i: (b * nq + i, 0)),
            pl.BlockSpec((N_MEM, 2 * MEM_WIDTH), lambda b, i: (b, 0)),
        ],
        out_specs=pl.BlockSpec((tq, MEM_WIDTH), lambda b, i: (b * nq + i, 0)),
        out_shape=jax.ShapeDtypeStruct((batch * seq, MEM_WIDTH), BF16),
        compiler_params=_params(("parallel", "arbitrary")),
        name="mem_attn",
    )(mq, mem_kv)


def _ssm_operators(lam_re, lam_im, log_dt, b_re, b_im, c_re, c_im, d_skip):
    hp = lax.Precision.HIGHEST
    t_len, g_n, p_n, ch = SSM_CHUNK, SSM_GROUPS, SSM_STATE, SSM_GROUP
    lr = jnp.minimum(lam_re, -1e-4)
    li = lam_im
    dt = jnp.exp(log_dt)[:, None]
    mag = jnp.exp(lr * dt)
    ar = mag * jnp.cos(li * dt)
    ai = mag * jnp.sin(li * dt)
    nr, ni = ar - 1.0, ai
    den = lr * lr + li * li
    kr = (nr * lr + ni * li) / den
    ki = (ni * lr - nr * li) / den
    bbr = kr[..., None] * b_re - ki[..., None] * b_im
    bbi = kr[..., None] * b_im + ki[..., None] * b_re
    steps = jnp.arange(t_len + 1, dtype=F32)[None, :, None]
    pmag = jnp.exp(steps * (lr * dt)[:, None, :])
    ang = steps * (li * dt)[:, None, :]
    pr = pmag * jnp.cos(ang)
    pi = pmag * jnp.sin(ang)
    wr = pr[:, :t_len, :, None] * bbr[:, None] - pi[:, :t_len, :, None] * bbi[:, None]
    wi = pr[:, :t_len, :, None] * bbi[:, None] + pi[:, :t_len, :, None] * bbr[:, None]
    lagk = (jnp.einsum('gcp,gkpd->gkcd', c_re, wr, precision=hp)
            - jnp.einsum('gcp,gkpd->gkcd', c_im, wi, precision=hp))
    nt, gt = SSM_LANE_TILES, SSM_TILE_GROUPS
    lag_bd = jnp.einsum('jgkoc,gh->jkgcho', lagk.reshape(nt, gt, t_len, ch, ch), jnp.eye(gt, dtype=F32),
                        precision=hp).reshape(nt, t_len, LANES, LANES)

    def rows_tgc(w):
        return (w[:, ::-1].reshape(nt, gt, t_len, p_n, ch).transpose(0, 2, 1, 4, 3)
                .reshape(nt, t_len * LANES, p_n))

    inp = jnp.concatenate([rows_tgc(wr), rows_tgc(wi)], axis=-1)
    pr1 = pr[:, 1:, None, :]
    pi1 = pi[:, 1:, None, :]
    out_r = c_re[:, None] * pr1 - c_im[:, None] * pi1
    out_i = -(c_re[:, None] * pi1 + c_im[:, None] * pr1)

    def cols_tgc(w):
        return (w.reshape(nt, gt, t_len, ch, p_n).transpose(0, 4, 2, 1, 3)
                .reshape(nt, p_n, t_len * LANES))

    outp = jnp.concatenate([cols_tgc(out_r), cols_tgc(out_i)], axis=1)
    a1 = jnp.concatenate([pr[:, t_len], pr[:, t_len]], axis=-1).reshape(nt, gt, 2 * p_n)
    a2 = jnp.concatenate([-pi[:, t_len], pi[:, t_len]], axis=-1).reshape(nt, gt, 2 * p_n)
    return lag_bd, inp, outp, a1, a2, d_skip.reshape(1, SSM_WIDTH)


def _ssm_body(s_ref, lag_ref, inp_ref, outp_ref, a1_ref, a2_ref, d_ref, y_ref,
              u_ref, panel_ref, inpx_ref, outpx_ref, z_ref, zs_ref, sp_ref, *, batch):
    t_len, gt = SSM_CHUNK, SSM_TILE_GROUPS
    n_chunks = u_ref.shape[0]
    per_seq = n_chunks // batch
    flat = t_len * LANES

    for t in range(t_len):
        u_ref[:, t * LANES:(t + 1) * LANES] = s_ref[pl.ds(t, n_chunks, stride=t_len), :].astype(BF16)

    zero_blk = jnp.zeros((LANES, LANES), BF16)
    for r in range(t_len):
        left = lag_ref[t_len - 2 - r].astype(BF16) if r < t_len - 1 else zero_blk
        panel_ref[r * LANES:(r + 1) * LANES, :LANES] = left
        panel_ref[r * LANES:(r + 1) * LANES, LANES:] = lag_ref[t_len - 1 - r].astype(BF16)

    row_group = (lax.broadcasted_iota(jnp.int32, (flat, 1), 0) >> 4) & (gt - 1)
    col_group = (lax.broadcasted_iota(jnp.int32, (1, flat), 1) >> 4) & (gt - 1)
    inp = inp_ref[...]
    outp = outp_ref[...]
    for g in range(gt):
        inpx_ref[:, g * LANES:(g + 1) * LANES] = jnp.where(row_group == g, inp, 0.0).astype(BF16)
        outpx_ref[g * LANES:(g + 1) * LANES, :] = jnp.where(col_group == g, outp, 0.0).astype(BF16)

    z = jnp.dot(u_ref[...], inpx_ref[...], preferred_element_type=F32)
    for g in range(gt):
        zg = z[:, g * LANES:(g + 1) * LANES]
        z_ref[pl.ds(g, n_chunks, stride=gt), :] = zg
        zs_ref[pl.ds(g, n_chunks, stride=gt), :] = pltpu.roll(zg, SSM_STATE, axis=1)

    a1 = a1_ref[...]
    a2 = a2_ref[...]

    def step(c, carry):
        new = []
        for b in range(batch):
            v0, v1 = carry[b]
            row = pl.multiple_of((b * per_seq + c) * gt, gt)
            sp_ref[pl.ds(row, gt), :] = v0
            z0 = z_ref[pl.ds(row, gt), :]
            z1 = zs_ref[pl.ds(row, gt), :]
            new.append((a1 * v0 + a2 * v1 + z0, a1 * v1 - a2 * v0 + z1))
        return tuple(new)

    zero = jnp.zeros((gt, LANES), F32)
    lax.fori_loop(0, per_seq, step, tuple((zero, zero) for _ in range(batch)), unroll=4)

    sp = jnp.concatenate([sp_ref[pl.ds(g, n_chunks, stride=gt), :] for g in range(gt)], axis=1).astype(BF16)
    d2 = jnp.concatenate([d_ref[...], d_ref[...]], axis=1)
    for q in range(t_len // 2):
        cols = slice(2 * q * LANES, (2 * q + 2) * LANES)
        k_len = (2 * q + 2) * LANES
        y = jnp.dot(u_ref[:, :k_len], panel_ref[flat - k_len:, :], preferred_element_type=F32)
        y = y + jnp.dot(sp, outpx_ref[:, cols], preferred_element_type=F32)
        y = y + d2 * u_ref[:, cols].astype(F32)
        y_ref[pl.ds(2 * q, n_chunks, stride=t_len), :] = y[:, :LANES]
        y_ref[pl.ds(2 * q + 1, n_chunks, stride=t_len), :] = y[:, LANES:]


def _ssm(s_in, ops, batch):
    lag_bd, inp, outp, a1, a2, d_row = ops
    n = s_in.shape[0]
    n_chunks = n // SSM_CHUNK
    flat = SSM_CHUNK * LANES
    return pl.pallas_call(
        functools.partial(_ssm_body, batch=batch),
        grid=(SSM_LANE_TILES,),
        in_specs=[
            pl.BlockSpec((n, LANES), lambda j: (0, j)),
            pl.BlockSpec((None, SSM_CHUNK, LANES, LANES), lambda j: (j, 0, 0, 0)),
            pl.BlockSpec((None, flat, 2 * SSM_STATE), lambda j: (j, 0, 0)),
            pl.BlockSpec((None, 2 * SSM_STATE, flat), lambda j: (j, 0, 0)),
            pl.BlockSpec((None, SSM_TILE_GROUPS, 2 * SSM_STATE), lambda j: (j, 0, 0)),
            pl.BlockSpec((None, SSM_TILE_GROUPS, 2 * SSM_STATE), lambda j: (j, 0, 0)),
            pl.BlockSpec((1, LANES), lambda j: (0, j)),
        ],
        out_specs=pl.BlockSpec((n, LANES), lambda j: (0, j)),
        out_shape=jax.ShapeDtypeStruct((n, SSM_WIDTH), F32),
        scratch_shapes=[
            pltpu.VMEM((n_chunks, flat), BF16),
            pltpu.VMEM((flat, 2 * LANES), BF16),
            pltpu.VMEM((flat, SSM_TILE_GROUPS * 2 * SSM_STATE), BF16),
            pltpu.VMEM((SSM_TILE_GROUPS * 2 * SSM_STATE, flat), BF16),
            pltpu.VMEM((n_chunks * SSM_TILE_GROUPS, 2 * SSM_STATE), F32),
            pltpu.VMEM((n_chunks * SSM_TILE_GROUPS, 2 * SSM_STATE), F32),
            pltpu.VMEM((n_chunks * SSM_TILE_GROUPS, 2 * SSM_STATE), F32),
        ],
        compiler_params=_params(("parallel",)),
        name="ssm",
    )(s_in, lag_bd, inp, outp, a1, a2, d_row)


MERGE_ROW_CHUNK = 512


def _merge_body(h_ref, nw_ref, swa_ref, ssm_ref, mem_ref, wg0_ref, wg1_ref, wg2_ref, wswa_ref,
                wga_ref, wgb_ref, wmem_ref, wo_ref, o_ref, xn_ref):
    j = pl.program_id(1)

    @pl.when(j == 0)
    def _():
        h = h_ref[...]
        xn_ref[...] = _rms_normalize(h, nw_ref[...]).astype(BF16)
        o_ref[...] = h

    for r in range(o_ref.shape[0] // MERGE_ROW_CHUNK):
        rows = pl.ds(r * MERGE_ROW_CHUNK, MERGE_ROW_CHUNK)
        xn = xn_ref[rows, :]
        ys = ssm_ref[rows, :].astype(BF16)
        y_swa = jnp.dot(swa_ref[rows, :], wswa_ref[...], preferred_element_type=F32)
        ga = jnp.dot(ys, wga_ref[...], preferred_element_type=F32)
        gb = jnp.dot(ys, wgb_ref[...], preferred_element_type=F32)
        y_ssm = ga * _sigmoid(gb)
        y_mem = jnp.dot(mem_ref[rows, :], wmem_ref[...], preferred_element_type=F32)
        merged = _sigmoid(jnp.dot(xn, wg0_ref[...], preferred_element_type=F32)) * y_swa
        merged += _sigmoid(jnp.dot(xn, wg1_ref[...], preferred_element_type=F32)) * y_ssm
        merged += _sigmoid(jnp.dot(xn, wg2_ref[...], preferred_element_type=F32)) * y_mem
        o_ref[rows, :] += jnp.dot(merged.astype(BF16), wo_ref[...], preferred_element_type=F32)


def _merge(h, norm_w, o_swa, y_s, o_mem, w_gates, w_swa_up, w_ssm_glu, w_mem_up, w_out, *, tm=1024, tn=256):
    n = h.shape[0]
    nj = D_MODEL // tn
    row = lambda i, j: (i, 0)
    once = pl.Buffered(1)
    return pl.pallas_call(
        _merge_body,
        grid=(n // tm, nj),
        in_specs=[
            pl.BlockSpec((tm, D_MODEL), row, pipeline_mode=once),
            pl.BlockSpec((1, D_MODEL), lambda i, j: (0, 0)),
            pl.BlockSpec((tm, Q_WIDTH), row, pipeline_mode=once),
            pl.BlockSpec((tm, SSM_WIDTH), row, pipeline_mode=once),
            pl.BlockSpec((tm, MEM_WIDTH), row, pipeline_mode=once),
            pl.BlockSpec((D_MODEL, tn), lambda i, j: (0, j)),
            pl.BlockSpec((D_MODEL, tn), lambda i, j: (0, j + nj)),
            pl.BlockSpec((D_MODEL, tn), lambda i, j: (0, j + 2 * nj)),
            pl.BlockSpec((Q_WIDTH, tn), lambda i, j: (0, j)),
            pl.BlockSpec((SSM_WIDTH, tn), lambda i, j: (0, j)),
            pl.BlockSpec((SSM_WIDTH, tn), lambda i, j: (0, j + nj)),
            pl.BlockSpec((MEM_WIDTH, tn), lambda i, j: (0, j)),
            pl.BlockSpec((tn, D_MODEL), lambda i, j: (j, 0)),
        ],
        out_specs=pl.BlockSpec((tm, D_MODEL), row),
        out_shape=jax.ShapeDtypeStruct((n, D_MODEL), F32),
        scratch_shapes=[pltpu.VMEM((tm, D_MODEL), BF16)],
        compiler_params=_params(("parallel", "arbitrary")),
        name="merge",
    )(h, norm_w, o_swa, y_s, o_mem, w_gates, w_gates, w_gates, w_swa_up, w_ssm_glu, w_ssm_glu, w_mem_up, w_out)


def kernel(x, mem, ffn1_norm, ffn1_w_in, ffn1_w_out, mix_norm, mem_norm, w_in, sinks, w_mem_kv, lam_re, lam_im, log_dt, b_re, b_im, c_re, c_im, d_skip, w_ssm_glu, w_swa_up, w_mem_up, w_out, ffn2_norm, ffn2_w_in, ffn2_w_out, final_norm):
    batch, seq = x.shape[0], x.shape[1]
    n = batch * seq
    h = x.reshape(n, D_MODEL)
    mem2 = mem.reshape(batch * N_MEM, D_MODEL)
    final_w = final_norm.reshape(1, D_MODEL)
    for l in range(DEPTH):
        mix_w = mix_norm[l].reshape(1, D_MODEL)
        ops = _ssm_operators(lam_re[l], lam_im[l], log_dt[l], b_re[l], b_im[l], c_re[l], c_im[l], d_skip[l])

        h = _ffn(h, ffn1_norm[l].reshape(1, D_MODEL), ffn1_w_in, ffn1_w_out, final_w, l, apply_final_norm=False)
        qkv = _norm_proj(h, mix_w, w_in, l, 0, QKV_WIDTH, BF16, tm=1024)
        s_in = _norm_proj(h, mix_w, w_in, l, SSM_OFFSET, SSM_WIDTH, F32, tm=1024)
        mq = _norm_proj(h, mix_w, w_in, l, MEMQ_OFFSET, MEM_WIDTH, BF16, tm=1024)
        mem_kv = _norm_proj(mem2, mem_norm[l].reshape(1, D_MODEL), w_mem_kv, l, 0, 2 * MEM_WIDTH, BF16,
                            tm=batch * N_MEM)
        o_swa = _swa(qkv, sinks[l], batch, seq)
        o_mem = _mem_attn(mq, mem_kv, batch, seq)
        y_s = _ssm(s_in, ops, batch)
        h = _merge(h, mix_w, o_swa, y_s, o_mem, w_in[l, :, GATE_OFFSET:].astype(BF16), w_swa_up[l].astype(BF16),
                   w_ssm_glu[l].astype(BF16), w_mem_up[l].astype(BF16), w_out[l].astype(BF16))
        h = _ffn(h, ffn2_norm[l].reshape(1, D_MODEL), ffn2_w_in, ffn2_w_out, final_w, l,
                 apply_final_norm=(l == DEPTH - 1))
    return h.reshape(batch, seq, D_MODEL)
```

```python
import functools
import math

import jax
import jax.numpy as jnp
from jax import lax
from jax.experimental import pallas as pl
from jax.experimental.pallas import tpu as pltpu

D_MODEL = 2048
DEPTH = 4
N_MEM = 256
D_FF = 5632
RMS_EPS = 1e-5

WINDOW = 128
HEAD_DIM = 64
N_Q_HEADS = 16
N_KV_HEADS = 4
GQA_REP = N_Q_HEADS // N_KV_HEADS
Q_WIDTH = N_Q_HEADS * HEAD_DIM
KV_WIDTH = N_KV_HEADS * HEAD_DIM

SSM_WIDTH = 1024
SSM_GROUP = 16
SSM_GROUPS = SSM_WIDTH // SSM_GROUP
SSM_STATE = 64
SSM_CHUNK = 16
LANES = 128
SSM_LANE_TILES = SSM_WIDTH // LANES
SSM_TILE_GROUPS = LANES // SSM_GROUP

MEM_HEADS = 4
MEM_HEAD_DIM = 256
MEM_WIDTH = MEM_HEADS * MEM_HEAD_DIM

N_BRANCHES = 3
NEG_INF = -1e30

QKV_WIDTH = Q_WIDTH + 2 * KV_WIDTH
SSM_OFFSET = QKV_WIDTH
MEMQ_OFFSET = SSM_OFFSET + SSM_WIDTH
GATE_OFFSET = MEMQ_OFFSET + MEM_WIDTH

VMEM_LIMIT_BYTES = 56 * 1024 * 1024

BF16 = jnp.bfloat16
F32 = jnp.float32


def _params(semantics):
    return pltpu.CompilerParams(dimension_semantics=semantics, vmem_limit_bytes=VMEM_LIMIT_BYTES)


def _rms_normalize(x, w):
    ms = jnp.mean(x * x, axis=-1, keepdims=True)
    return (x * lax.rsqrt(ms + RMS_EPS)) * w


def _sigmoid(x):
    return 1.0 / (1.0 + jnp.exp(-x))


FFN_ROW_CHUNK = 512


def _ffn_body(h_ref, nw_ref, wg_ref, wu_ref, wo_ref, fw_ref, o_ref, xn_ref, *, apply_final_norm):
    j = pl.program_id(1)

    @pl.when(j == 0)
    def _():
        h = h_ref[...]
        xn_ref[...] = _rms_normalize(h, nw_ref[...]).astype(BF16)
        o_ref[...] = h

    wg = wg_ref[...].astype(BF16)
    wu = wu_ref[...].astype(BF16)
    wo = wo_ref[...].astype(BF16)
    for r in range(o_ref.shape[0] // FFN_ROW_CHUNK):
        rows = pl.ds(r * FFN_ROW_CHUNK, FFN_ROW_CHUNK)
        xn = xn_ref[rows, :]
        g = jnp.dot(xn, wg, preferred_element_type=F32)
        u = jnp.dot(xn, wu, preferred_element_type=F32)
        a = ((0.5 * g) * _sigmoid(g)) * u
        o_ref[rows, :] += jnp.dot(a.astype(BF16), wo, preferred_element_type=F32)

    if apply_final_norm:
        @pl.when(j == pl.num_programs(1) - 1)
        def _():
            o_ref[...] = _rms_normalize(o_ref[...], fw_ref[...])


def _ffn(h, norm_w, w_in, w_out, final_w, layer, *, apply_final_norm, tm=1024, tf=512):
    n = h.shape[0]
    nf = D_FF // tf
    return pl.pallas_call(
        functools.partial(_ffn_body, apply_final_norm=apply_final_norm),
        grid=(n // tm, nf),
        in_specs=[
            pl.BlockSpec((tm, D_MODEL), lambda i, j: (i, 0), pipeline_mode=pl.Buffered(1)),
            pl.BlockSpec((1, D_MODEL), lambda i, j: (0, 0)),
            pl.BlockSpec((None, D_MODEL, tf), lambda i, j: (layer, 0, j)),
            pl.BlockSpec((None, D_MODEL, tf), lambda i, j: (layer, 0, j + nf)),
            pl.BlockSpec((None, tf, D_MODEL), lambda i, j: (layer, j, 0)),
            pl.BlockSpec((1, D_MODEL), lambda i, j: (0, 0)),
        ],
        out_specs=pl.BlockSpec((tm, D_MODEL), lambda i, j: (i, 0), pipeline_mode=pl.Buffered(1)),
        out_shape=jax.ShapeDtypeStruct((n, D_MODEL), F32),
        scratch_shapes=[pltpu.VMEM((tm, D_MODEL), BF16)],
        compiler_params=_params(("parallel", "arbitrary")),
        name="ffn",
    )(h, norm_w, w_in, w_in, w_out, final_w)


def _proj_body(h_ref, nw_ref, w_ref, o_ref, xn_ref):
    @pl.when(pl.program_id(1) == 0)
    def _():
        xn_ref[...] = _rms_normalize(h_ref[...], nw_ref[...]).astype(BF16)

    o_ref[...] = jnp.dot(xn_ref[...], w_ref[...].astype(BF16), preferred_element_type=F32).astype(o_ref.dtype)


def _norm_proj(h, norm_w, w, layer, col0, width, out_dtype, *, tm, tn=512):
    n = h.shape[0]
    c0 = col0 // tn
    return pl.pallas_call(
        _proj_body,
        grid=(n // tm, width // tn),
        in_specs=[
            pl.BlockSpec((tm, D_MODEL), lambda i, j: (i, 0)),
            pl.BlockSpec((1, D_MODEL), lambda i, j: (0, 0)),
            pl.BlockSpec((None, D_MODEL, tn), lambda i, j: (layer, 0, c0 + j)),
        ],
        out_specs=pl.BlockSpec((tm, tn), lambda i, j: (i, j)),
        out_shape=jax.ShapeDtypeStruct((n, width), out_dtype),
        scratch_shapes=[pltpu.VMEM((tm, D_MODEL), BF16)],
        compiler_params=_params(("parallel", "arbitrary")),
        name="norm_proj",
    )(h, norm_w, w)


def _mix_proj_body(h_ref, nw_ref, w_ref, qkv_ref, s_ref, mq_ref, xn_ref, *, n_qkv, n_ssm):
    j = pl.program_id(1)

    @pl.when(j == 0)
    def _():
        xn_ref[...] = _rms_normalize(h_ref[...], nw_ref[...]).astype(BF16)

    y = jnp.dot(xn_ref[...], w_ref[...].astype(BF16), preferred_element_type=F32)

    @pl.when(j < n_qkv)
    def _():
        qkv_ref[...] = y.astype(BF16)

    @pl.when((j >= n_qkv) & (j < n_qkv + n_ssm))
    def _():
        s_ref[...] = y

    @pl.when(j >= n_qkv + n_ssm)
    def _():
        mq_ref[...] = y.astype(BF16)


def _mix_proj(h, norm_w, w_in, layer, *, tm=1024, tn=512):
    n = h.shape[0]
    n_qkv, n_ssm, n_mq = QKV_WIDTH // tn, SSM_WIDTH // tn, MEM_WIDTH // tn
    return pl.pallas_call(
        functools.partial(_mix_proj_body, n_qkv=n_qkv, n_ssm=n_ssm),
        grid=(n // tm, n_qkv + n_ssm + n_mq),
        in_specs=[
            pl.BlockSpec((tm, D_MODEL), lambda i, j: (i, 0)),
            pl.BlockSpec((1, D_MODEL), lambda i, j: (0, 0)),
            pl.BlockSpec((None, D_MODEL, tn), lambda i, j: (layer, 0, j)),
        ],
        out_specs=[
            pl.BlockSpec((tm, tn), lambda i, j: (i, jnp.minimum(j, n_qkv - 1))),
            pl.BlockSpec((tm, tn), lambda i, j: (i, jnp.clip(j - n_qkv, 0, n_ssm - 1))),
            pl.BlockSpec((tm, tn), lambda i, j: (i, jnp.clip(j - n_qkv - n_ssm, 0, n_mq - 1))),
        ],
        out_shape=[
            jax.ShapeDtypeStruct((n, QKV_WIDTH), BF16),
            jax.ShapeDtypeStruct((n, SSM_WIDTH), F32),
            jax.ShapeDtypeStruct((n, MEM_WIDTH), BF16),
        ],
        scratch_shapes=[pltpu.VMEM((tm, D_MODEL), BF16)],
        compiler_params=_params(("parallel", "arbitrary")),
        name="mix_proj",
    )(h, norm_w, w_in)


def _swa_body(sinks_ref, q_ref, kvc_ref, kvp_ref, o_ref):
    blk = pl.program_id(1)
    q = q_ref[...]
    kvc = kvc_ref[...]
    kvp = kvp_ref[...]
    qi = lax.broadcasted_iota(jnp.int32, (WINDOW, 2 * WINDOW), 0)
    kj = lax.broadcasted_iota(jnp.int32, (WINDOW, 2 * WINDOW), 1)
    first_key = jnp.where(blk > 0, 0, WINDOW)
    valid = (kj > qi) & (kj <= qi + WINDOW) & (kj >= first_key)
    scale = HEAD_DIM ** -0.5
    for g in range(N_KV_HEADS):
        ks = slice(g * HEAD_DIM, (g + 1) * HEAD_DIM)
        vs = slice(KV_WIDTH + g * HEAD_DIM, KV_WIDTH + (g + 1) * HEAD_DIM)
        k = jnp.concatenate([kvp[:, ks], kvc[:, ks]], axis=0)
        v = jnp.concatenate([kvp[:, vs], kvc[:, vs]], axis=0)
        for r in range(GQA_REP):
            h = g * GQA_REP + r
            qh = q[:, h * HEAD_DIM:(h + 1) * HEAD_DIM]
            s = lax.dot_general(qh, k, (((1,), (1,)), ((), ())), preferred_element_type=F32) * scale
            s = jnp.where(valid, s, NEG_INF)
            sink = sinks_ref[h]
            m = jnp.maximum(jnp.max(s, axis=-1, keepdims=True), sink)
            p = jnp.exp(s - m)
            denom = jnp.sum(p, axis=-1, keepdims=True) + jnp.exp(sink - m)
            o = jnp.dot(p.astype(BF16), v, preferred_element_type=F32) / denom
            o_ref[:, h * HEAD_DIM:(h + 1) * HEAD_DIM] = o.astype(BF16)


def _swa(qkv, sinks, batch, seq):
    nb = seq // WINDOW
    kv_col = Q_WIDTH // (2 * KV_WIDTH)
    return pl.pallas_call(
        _swa_body,
        grid=(batch, nb),
        in_specs=[
            pl.BlockSpec(memory_space=pltpu.SMEM),
            pl.BlockSpec((WINDOW, Q_WIDTH), lambda b, n: (b * nb + n, 0)),
            pl.BlockSpec((WINDOW, 2 * KV_WIDTH), lambda b, n: (b * nb + n, kv_col)),
            pl.BlockSpec((WINDOW, 2 * KV_WIDTH), lambda b, n: (b * nb + jnp.maximum(n - 1, 0), kv_col)),
        ],
        out_specs=pl.BlockSpec((WINDOW, Q_WIDTH), lambda b, n: (b * nb + n, 0)),
        out_shape=jax.ShapeDtypeStruct((batch * seq, Q_WIDTH), BF16),
        compiler_params=_params(("parallel", "arbitrary")),
        name="swa",
    )(sinks, qkv, qkv, qkv)


def _mem_attn_body(q_ref, kv_ref, o_ref):
    scale = MEM_HEAD_DIM ** -0.5
    for h in range(MEM_HEADS):
        cs = slice(h * MEM_HEAD_DIM, (h + 1) * MEM_HEAD_DIM)
        vs = slice(MEM_WIDTH + h * MEM_HEAD_DIM, MEM_WIDTH + (h + 1) * MEM_HEAD_DIM)
        s = lax.dot_general(q_ref[:, cs], kv_ref[:, cs], (((1,), (1,)), ((), ())),
                            preferred_element_type=F32) * scale
        m = jnp.max(s, axis=-1, keepdims=True)
        p = jnp.exp(s - m)
        denom = jnp.sum(p, axis=-1, keepdims=True)
        o = jnp.dot(p.astype(BF16), kv_ref[:, vs], preferred_element_type=F32) / denom
        o_ref[:, cs] = o.astype(BF16)


def _mem_attn(mq, mem_kv, batch, seq, *, tq=512):
    nq = seq // tq
    return pl.pallas_call(
        _mem_attn_body,
        grid=(batch, nq),
        in_specs=[
            pl.BlockSpec((tq, MEM_WIDTH), lambda b, i: (b * nq + i, 0)),
            pl.BlockSpec((N_MEM, 2 * MEM_WIDTH), lambda b, i: (b, 0)),
        ],
        out_specs=pl.BlockSpec((tq, MEM_WIDTH), lambda b, i: (b * nq + i, 0)),
        out_shape=jax.ShapeDtypeStruct((batch * seq, MEM_WIDTH), BF16),
        compiler_params=_params(("parallel", "arbitrary")),
        name="mem_attn",
    )(mq, mem_kv)


def _ssm_operators(lam_re, lam_im, log_dt, b_re, b_im, c_re, c_im, d_skip):
    hp = lax.Precision.HIGHEST
    t_len, g_n, p_n, ch = SSM_CHUNK, SSM_GROUPS, SSM_STATE, SSM_GROUP
    lr = jnp.minimum(lam_re, -1e-4)
    li = lam_im
    dt = jnp.exp(log_dt)[:, None]
    mag = jnp.exp(lr * dt)
    ar = mag * jnp.cos(li * dt)
    ai = mag * jnp.sin(li * dt)
    nr, ni = ar - 1.0, ai
    den = lr * lr + li * li
    kr = (nr * lr + ni * li) / den
    ki = (ni * lr - nr * li) / den
    bbr = kr[..., None] * b_re - ki[..., None] * b_im
    bbi = kr[..., None] * b_im + ki[..., None] * b_re
    steps = jnp.arange(t_len + 1, dtype=F32)[None, :, None]
    pmag = jnp.exp(steps * (lr * dt)[:, None, :])
    ang = steps * (li * dt)[:, None, :]
    pr = pmag * jnp.cos(ang)
    pi = pmag * jnp.sin(ang)
    wr = pr[:, :t_len, :, None] * bbr[:, None] - pi[:, :t_len, :, None] * bbi[:, None]
    wi = pr[:, :t_len, :, None] * bbi[:, None] + pi[:, :t_len, :, None] * bbr[:, None]
    lagk = (jnp.einsum('gcp,gkpd->gkcd', c_re, wr, precision=hp)
            - jnp.einsum('gcp,gkpd->gkcd', c_im, wi, precision=hp))
    nt, gt = SSM_LANE_TILES, SSM_TILE_GROUPS
    lag_t = lagk.reshape(nt, gt, t_len, ch, ch).transpose(0, 2, 1, 4, 3)
    lag_bd = (lag_t[:, :, :, :, None, :] * jnp.eye(gt, dtype=F32)[None, None, :, None, :, None]
              ).reshape(nt, t_len, LANES, LANES)

    def tile_tgp(w):
        return w.reshape(nt, gt, t_len, p_n).transpose(0, 2, 1, 3)[:, :, :, None, :]

    def tile_gcp(b):
        return b.transpose(0, 2, 1).reshape(nt, gt, ch, p_n)[:, None]

    back = jnp.arange(t_len - 1, -1, -1, dtype=F32)[None, :, None]
    bmag = jnp.exp(back * (lr * dt)[:, None, :])
    bang = back * (li * dt)[:, None, :]
    prk, pik = tile_tgp(bmag * jnp.cos(bang)), tile_tgp(bmag * jnp.sin(bang))
    bbr_t, bbi_t = tile_gcp(bbr), tile_gcp(bbi)
    inp = jnp.concatenate([prk * bbr_t - pik * bbi_t, prk * bbi_t + pik * bbr_t], axis=-1
                          ).reshape(nt, t_len * LANES, 2 * p_n)

    def tile_ptg(w):
        return w[:, 1:].reshape(nt, gt, t_len, p_n).transpose(0, 3, 2, 1)[..., None]

    def tile_pgc(c):
        return c.reshape(nt, gt, ch, p_n).transpose(0, 3, 1, 2)[:, :, None]

    pr1, pi1, cr_t, ci_t = tile_ptg(pr), tile_ptg(pi), tile_pgc(c_re), tile_pgc(c_im)
    outp = jnp.concatenate([cr_t * pr1 - ci_t * pi1, -(cr_t * pi1 + ci_t * pr1)], axis=1
                           ).reshape(nt, 2 * p_n, t_len * LANES)
    a1 = jnp.concatenate([pr[:, t_len], pr[:, t_len]], axis=-1).reshape(nt, gt, 2 * p_n)
    a2 = jnp.concatenate([-pi[:, t_len], pi[:, t_len]], axis=-1).reshape(nt, gt, 2 * p_n)
    return lag_bd, inp, outp, a1, a2, d_skip.reshape(1, SSM_WIDTH)


def _ssm_body(s_ref, lag_ref, inp_ref, outp_ref, a1_ref, a2_ref, d_ref, y_ref,
              u_ref, panel_ref, inpx_ref, outpx_ref, z_ref, zs_ref, sp_ref, *, batch):
    t_len, gt = SSM_CHUNK, SSM_TILE_GROUPS
    n_chunks = u_ref.shape[0]
    per_seq = n_chunks // batch
    flat = t_len * LANES

    for t in range(t_len):
        u_ref[:, t * LANES:(t + 1) * LANES] = s_ref[pl.ds(t, n_chunks, stride=t_len), :].astype(BF16)

    zero_blk = jnp.zeros((LANES, LANES), BF16)
    for r in range(t_len):
        left = lag_ref[t_len - 2 - r].astype(BF16) if r < t_len - 1 else zero_blk
        panel_ref[r * LANES:(r + 1) * LANES, :LANES] = left
        panel_ref[r * LANES:(r + 1) * LANES, LANES:] = lag_ref[t_len - 1 - r].astype(BF16)

    row_group = (lax.broadcasted_iota(jnp.int32, (flat, 1), 0) >> 4) & (gt - 1)
    col_group = (lax.broadcasted_iota(jnp.int32, (1, flat), 1) >> 4) & (gt - 1)
    inp = inp_ref[...]
    outp = outp_ref[...]
    for g in range(gt):
        inpx_ref[:, g * LANES:(g + 1) * LANES] = jnp.where(row_group == g, inp, 0.0).astype(BF16)
        outpx_ref[g * LANES:(g + 1) * LANES, :] = jnp.where(col_group == g, outp, 0.0).astype(BF16)

    z = jnp.dot(u_ref[...], inpx_ref[...], preferred_element_type=F32)
    for g in range(gt):
        zg = z[:, g * LANES:(g + 1) * LANES]
        z_ref[pl.ds(g, n_chunks, stride=gt), :] = zg
        zs_ref[pl.ds(g, n_chunks, stride=gt), :] = pltpu.roll(zg, SSM_STATE, axis=1)

    a1 = a1_ref[...]
    a2 = a2_ref[...]

    def step(c, carry):
        new = []
        for b in range(batch):
            v0, v1 = carry[b]
            row = pl.multiple_of((b * per_seq + c) * gt, gt)
            sp_ref[pl.ds(row, gt), :] = v0
            z0 = z_ref[pl.ds(row, gt), :]
            z1 = zs_ref[pl.ds(row, gt), :]
            new.append((a1 * v0 + a2 * v1 + z0, a1 * v1 - a2 * v0 + z1))
        return tuple(new)

    zero = jnp.zeros((gt, LANES), F32)
    lax.fori_loop(0, per_seq, step, tuple((zero, zero) for _ in range(batch)), unroll=4)

    sp = jnp.concatenate([sp_ref[pl.ds(g, n_chunks, stride=gt), :] for g in range(gt)], axis=1).astype(BF16)
    d2 = jnp.concatenate([d_ref[...], d_ref[...]], axis=1)
    for q in range(t_len // 2):
        cols = slice(2 * q * LANES, (2 * q + 2) * LANES)
        k_len = (2 * q + 2) * LANES
        y = jnp.dot(u_ref[:, :k_len], panel_ref[flat - k_len:, :], preferred_element_type=F32)
        y = y + jnp.dot(sp, outpx_ref[:, cols], preferred_element_type=F32)
        y = y + d2 * u_ref[:, cols].astype(F32)
        y_ref[pl.ds(2 * q, n_chunks, stride=t_len), :] = y[:, :LANES]
        y_ref[pl.ds(2 * q + 1, n_chunks, stride=t_len), :] = y[:, LANES:]


def _ssm(s_in, ops, layer, batch):
    lag_bd, inp, outp, a1, a2, d_row = ops
    n = s_in.shape[0]
    n_chunks = n // SSM_CHUNK
    flat = SSM_CHUNK * LANES
    return pl.pallas_call(
        functools.partial(_ssm_body, batch=batch),
        grid=(SSM_LANE_TILES,),
        in_specs=[
            pl.BlockSpec((n, LANES), lambda j: (0, j)),
            pl.BlockSpec((None, None, SSM_CHUNK, LANES, LANES), lambda j: (layer, j, 0, 0, 0)),
            pl.BlockSpec((None, None, flat, 2 * SSM_STATE), lambda j: (layer, j, 0, 0)),
            pl.BlockSpec((None, None, 2 * SSM_STATE, flat), lambda j: (layer, j, 0, 0)),
            pl.BlockSpec((None, None, SSM_TILE_GROUPS, 2 * SSM_STATE), lambda j: (layer, j, 0, 0)),
            pl.BlockSpec((None, None, SSM_TILE_GROUPS, 2 * SSM_STATE), lambda j: (layer, j, 0, 0)),
            pl.BlockSpec((None, 1, LANES), lambda j: (layer, 0, j)),
        ],
        out_specs=pl.BlockSpec((n, LANES), lambda j: (0, j)),
        out_shape=jax.ShapeDtypeStruct((n, SSM_WIDTH), F32),
        scratch_shapes=[
            pltpu.VMEM((n_chunks, flat), BF16),
            pltpu.VMEM((flat, 2 * LANES), BF16),
            pltpu.VMEM((flat, SSM_TILE_GROUPS * 2 * SSM_STATE), BF16),
            pltpu.VMEM((SSM_TILE_GROUPS * 2 * SSM_STATE, flat), BF16),
            pltpu.VMEM((n_chunks * SSM_TILE_GROUPS, 2 * SSM_STATE), F32),
            pltpu.VMEM((n_chunks * SSM_TILE_GROUPS, 2 * SSM_STATE), F32),
            pltpu.VMEM((n_chunks * SSM_TILE_GROUPS, 2 * SSM_STATE), F32),
        ],
        compiler_params=_params(("parallel",)),
        name="ssm",
    )(s_in, lag_bd, inp, outp, a1, a2, d_row)


MERGE_ROW_CHUNK = 512


def _merge_body(h_ref, nw_ref, swa_ref, ssm_ref, mem_ref, wg0_ref, wg1_ref, wg2_ref, wswa_ref,
                wga_ref, wgb_ref, wmem_ref, wo_ref, o_ref, xn_ref):
    j = pl.program_id(1)

    @pl.when(j == 0)
    def _():
        h = h_ref[...]
        xn_ref[...] = _rms_normalize(h, nw_ref[...]).astype(BF16)
        o_ref[...] = h

    for r in range(o_ref.shape[0] // MERGE_ROW_CHUNK):
        rows = pl.ds(r * MERGE_ROW_CHUNK, MERGE_ROW_CHUNK)
        xn = xn_ref[rows, :]
        ys = ssm_ref[rows, :].astype(BF16)
        y_swa = jnp.dot(swa_ref[rows, :], wswa_ref[...], preferred_element_type=F32)
        ga = jnp.dot(ys, wga_ref[...], preferred_element_type=F32)
        gb = jnp.dot(ys, wgb_ref[...], preferred_element_type=F32)
        y_ssm = ga * _sigmoid(gb)
        y_mem = jnp.dot(mem_ref[rows, :], wmem_ref[...], preferred_element_type=F32)
        merged = _sigmoid(jnp.dot(xn, wg0_ref[...], preferred_element_type=F32)) * y_swa
        merged += _sigmoid(jnp.dot(xn, wg1_ref[...], preferred_element_type=F32)) * y_ssm
        merged += _sigmoid(jnp.dot(xn, wg2_ref[...], preferred_element_type=F32)) * y_mem
        o_ref[rows, :] += jnp.dot(merged.astype(BF16), wo_ref[...], preferred_element_type=F32)


def _merge(h, norm_w, o_swa, y_s, o_mem, w_gates, w_swa_up, w_ssm_glu, w_mem_up, w_out, *, tm=1024, tn=256):
    n = h.shape[0]
    nj = D_MODEL // tn
    row = lambda i, j: (i, 0)
    once = pl.Buffered(1)
    return pl.pallas_call(
        _merge_body,
        grid=(n // tm, nj),
        in_specs=[
            pl.BlockSpec((tm, D_MODEL), row, pipeline_mode=once),
            pl.BlockSpec((1, D_MODEL), lambda i, j: (0, 0)),
            pl.BlockSpec((tm, Q_WIDTH), row, pipeline_mode=once),
            pl.BlockSpec((tm, SSM_WIDTH), row, pipeline_mode=once),
            pl.BlockSpec((tm, MEM_WIDTH), row, pipeline_mode=once),
            pl.BlockSpec((D_MODEL, tn), lambda i, j: (0, j)),
            pl.BlockSpec((D_MODEL, tn), lambda i, j: (0, j + nj)),
            pl.BlockSpec((D_MODEL, tn), lambda i, j: (0, j + 2 * nj)),
            pl.BlockSpec((Q_WIDTH, tn), lambda i, j: (0, j)),
            pl.BlockSpec((SSM_WIDTH, tn), lambda i, j: (0, j)),
            pl.BlockSpec((SSM_WIDTH, tn), lambda i, j: (0, j + nj)),
            pl.BlockSpec((MEM_WIDTH, tn), lambda i, j: (0, j)),
            pl.BlockSpec((tn, D_MODEL), lambda i, j: (j, 0)),
        ],
        out_specs=pl.BlockSpec((tm, D_MODEL), row),
        out_shape=jax.ShapeDtypeStruct((n, D_MODEL), F32),
        scratch_shapes=[pltpu.VMEM((tm, D_MODEL), BF16)],
        compiler_params=_params(("parallel", "arbitrary")),
        name="merge",
    )(h, norm_w, o_swa, y_s, o_mem, w_gates, w_gates, w_gates, w_swa_up, w_ssm_glu, w_ssm_glu, w_mem_up, w_out)


def kernel(x, mem, ffn1_norm, ffn1_w_in, ffn1_w_out, mix_norm, mem_norm, w_in, sinks, w_mem_kv, lam_re, lam_im, log_dt, b_re, b_im, c_re, c_im, d_skip, w_ssm_glu, w_swa_up, w_mem_up, w_out, ffn2_norm, ffn2_w_in, ffn2_w_out, final_norm):
    batch, seq = x.shape[0], x.shape[1]
    n = batch * seq
    h = x.reshape(n, D_MODEL)
    mem2 = mem.reshape(batch * N_MEM, D_MODEL)
    final_w = final_norm.reshape(1, D_MODEL)
    ssm_ops = jax.vmap(_ssm_operators)(lam_re, lam_im, log_dt, b_re, b_im, c_re, c_im, d_skip)
    for l in range(DEPTH):
        mix_w = mix_norm[l].reshape(1, D_MODEL)

        h = _ffn(h, ffn1_norm[l].reshape(1, D_MODEL), ffn1_w_in, ffn1_w_out, final_w, l, apply_final_norm=False)
        qkv, s_in, mq = _mix_proj(h, mix_w, w_in, l)
        mem_kv = _norm_proj(mem2, mem_norm[l].reshape(1, D_MODEL), w_mem_kv, l, 0, 2 * MEM_WIDTH, BF16,
                            tm=batch * N_MEM)
        o_swa = _swa(qkv, sinks[l], batch, seq)
        o_mem = _mem_attn(mq, mem_kv, batch, seq)
        y_s = _ssm(s_in, ssm_ops, l, batch)
        h = _merge(h, mix_w, o_swa, y_s, o_mem, w_in[l, :, GATE_OFFSET:].astype(BF16), w_swa_up[l].astype(BF16),
                   w_ssm_glu[l].astype(BF16), w_mem_up[l].astype(BF16), w_out[l].astype(BF16))
        h = _ffn(h, ffn2_norm[l].reshape(1, D_MODEL), ffn2_w_in, ffn2_w_out, final_w, l,
                 apply_final_norm=(l == DEPTH - 1))
    return h.reshape(batch, seq, D_MODEL)
```

```python
import functools
import math

import jax
import jax.numpy as jnp
from jax import lax
from jax.experimental import pallas as pl
from jax.experimental.pallas import tpu as pltpu

D_MODEL = 2048
DEPTH = 4
N_MEM = 256
D_FF = 5632
RMS_EPS = 1e-5

WINDOW = 128
HEAD_DIM = 64
N_Q_HEADS = 16
N_KV_HEADS = 4
GQA_REP = N_Q_HEADS // N_KV_HEADS
Q_WIDTH = N_Q_HEADS * HEAD_DIM
KV_WIDTH = N_KV_HEADS * HEAD_DIM

SSM_WIDTH = 1024
SSM_GROUP = 16
SSM_GROUPS = SSM_WIDTH // SSM_GROUP
SSM_STATE = 64
SSM_CHUNK = 16
LANES = 128
SSM_LANE_TILES = SSM_WIDTH // LANES
SSM_TILE_GROUPS = LANES // SSM_GROUP

MEM_HEADS = 4
MEM_HEAD_DIM = 256
MEM_WIDTH = MEM_HEADS * MEM_HEAD_DIM

N_BRANCHES = 3
NEG_INF = -1e30

QKV_WIDTH = Q_WIDTH + 2 * KV_WIDTH
SSM_OFFSET = QKV_WIDTH
MEMQ_OFFSET = SSM_OFFSET + SSM_WIDTH
GATE_OFFSET = MEMQ_OFFSET + MEM_WIDTH

VMEM_LIMIT_BYTES = 56 * 1024 * 1024

BF16 = jnp.bfloat16
F32 = jnp.float32


def _params(semantics):
    return pltpu.CompilerParams(dimension_semantics=semantics, vmem_limit_bytes=VMEM_LIMIT_BYTES)


def _rms_normalize(x, w):
    ms = jnp.mean(x * x, axis=-1, keepdims=True)
    return (x * lax.rsqrt(ms + RMS_EPS)) * w


def _sigmoid(x):
    return 1.0 / (1.0 + jnp.exp(-x))


FFN_ROW_CHUNK = 512


def _ffn_body(h_ref, nw_ref, wg_ref, wu_ref, wo_ref, fw_ref, o_ref, xn_ref, *, apply_final_norm):
    j = pl.program_id(1)

    @pl.when(j == 0)
    def _():
        h = h_ref[...]
        xn_ref[...] = _rms_normalize(h, nw_ref[...]).astype(BF16)
        o_ref[...] = h

    wg = wg_ref[...].astype(BF16)
    wu = wu_ref[...].astype(BF16)
    wo = wo_ref[...].astype(BF16)
    for r in range(o_ref.shape[0] // FFN_ROW_CHUNK):
        rows = pl.ds(r * FFN_ROW_CHUNK, FFN_ROW_CHUNK)
        xn = xn_ref[rows, :]
        g = jnp.dot(xn, wg, preferred_element_type=F32)
        u = jnp.dot(xn, wu, preferred_element_type=F32)
        a = ((0.5 * g) * _sigmoid(g)) * u
        o_ref[rows, :] += jnp.dot(a.astype(BF16), wo, preferred_element_type=F32)

    if apply_final_norm:
        @pl.when(j == pl.num_programs(1) - 1)
        def _():
            o_ref[...] = _rms_normalize(o_ref[...], fw_ref[...])


def _ffn(h, norm_w, w_in, w_out, final_w, layer, *, apply_final_norm, tm=1024, tf=512):
    n = h.shape[0]
    nf = D_FF // tf
    return pl.pallas_call(
        functools.partial(_ffn_body, apply_final_norm=apply_final_norm),
        grid=(n // tm, nf),
        in_specs=[
            pl.BlockSpec((tm, D_MODEL), lambda i, j: (i, 0), pipeline_mode=pl.Buffered(1)),
            pl.BlockSpec((1, D_MODEL), lambda i, j: (0, 0)),
            pl.BlockSpec((None, D_MODEL, tf), lambda i, j: (layer, 0, j)),
            pl.BlockSpec((None, D_MODEL, tf), lambda i, j: (layer, 0, j + nf)),
            pl.BlockSpec((None, tf, D_MODEL), lambda i, j: (layer, j, 0)),
            pl.BlockSpec((1, D_MODEL), lambda i, j: (0, 0)),
        ],
        out_specs=pl.BlockSpec((tm, D_MODEL), lambda i, j: (i, 0), pipeline_mode=pl.Buffered(1)),
        out_shape=jax.ShapeDtypeStruct((n, D_MODEL), F32),
        scratch_shapes=[pltpu.VMEM((tm, D_MODEL), BF16)],
        compiler_params=_params(("parallel", "arbitrary")),
        name="ffn",
    )(h, norm_w, w_in, w_in, w_out, final_w)


def _proj_body(h_ref, nw_ref, w_ref, o_ref, xn_ref):
    @pl.when(pl.program_id(1) == 0)
    def _():
        xn_ref[...] = _rms_normalize(h_ref[...], nw_ref[...]).astype(BF16)

    o_ref[...] = jnp.dot(xn_ref[...], w_ref[...].astype(BF16), preferred_element_type=F32).astype(o_ref.dtype)


def _norm_proj(h, norm_w, w, layer, col0, width, out_dtype, *, tm, tn=512):
    n = h.shape[0]
    c0 = col0 // tn
    return pl.pallas_call(
        _proj_body,
        grid=(n // tm, width // tn),
        in_specs=[
            pl.BlockSpec((tm, D_MODEL), lambda i, j: (i, 0)),
            pl.BlockSpec((1, D_MODEL), lambda i, j: (0, 0)),
            pl.BlockSpec((None, D_MODEL, tn), lambda i, j: (layer, 0, c0 + j)),
        ],
        out_specs=pl.BlockSpec((tm, tn), lambda i, j: (i, j)),
        out_shape=jax.ShapeDtypeStruct((n, width), out_dtype),
        scratch_shapes=[pltpu.VMEM((tm, D_MODEL), BF16)],
        compiler_params=_params(("parallel", "arbitrary")),
        name="norm_proj",
    )(h, norm_w, w)


def _mix_proj_body(h_ref, nw_ref, w_ref, qkv_ref, s_ref, mq_ref, xn_ref, *, n_qkv, n_ssm):
    j = pl.program_id(1)

    @pl.when(j == 0)
    def _():
        xn_ref[...] = _rms_normalize(h_ref[...], nw_ref[...]).astype(BF16)

    y = jnp.dot(xn_ref[...], w_ref[...].astype(BF16), preferred_element_type=F32)

    @pl.when(j < n_qkv)
    def _():
        qkv_ref[...] = y.astype(BF16)

    @pl.when((j >= n_qkv) & (j < n_qkv + n_ssm))
    def _():
        s_ref[...] = y

    @pl.when(j >= n_qkv + n_ssm)
    def _():
        mq_ref[...] = y.astype(BF16)


def _mix_proj(h, norm_w, w_in, layer, *, tm=1024, tn=512):
    n = h.shape[0]
    n_qkv, n_ssm, n_mq = QKV_WIDTH // tn, SSM_WIDTH // tn, MEM_WIDTH // tn
    return pl.pallas_call(
        functools.partial(_mix_proj_body, n_qkv=n_qkv, n_ssm=n_ssm),
        grid=(n // tm, n_qkv + n_ssm + n_mq),
        in_specs=[
            pl.BlockSpec((tm, D_MODEL), lambda i, j: (i, 0)),
            pl.BlockSpec((1, D_MODEL), lambda i, j: (0, 0)),
            pl.BlockSpec((None, D_MODEL, tn), lambda i, j: (layer, 0, j)),
        ],
        out_specs=[
            pl.BlockSpec((tm, tn), lambda i, j: (i, jnp.minimum(j, n_qkv - 1))),
            pl.BlockSpec((tm, tn), lambda i, j: (i, jnp.clip(j - n_qkv, 0, n_ssm - 1))),
            pl.BlockSpec((tm, tn), lambda i, j: (i, jnp.clip(j - n_qkv - n_ssm, 0, n_mq - 1))),
        ],
        out_shape=[
            jax.ShapeDtypeStruct((n, QKV_WIDTH), BF16),
            jax.ShapeDtypeStruct((n, SSM_WIDTH), F32),
            jax.ShapeDtypeStruct((n, MEM_WIDTH), BF16),
        ],
        scratch_shapes=[pltpu.VMEM((tm, D_MODEL), BF16)],
        compiler_params=_params(("parallel", "arbitrary")),
        name="mix_proj",
    )(h, norm_w, w_in)


def _swa_body(sinks_ref, q_ref, kvc_ref, kvp_ref, o_ref):
    blk = pl.program_id(1)
    q = q_ref[...]
    kvc = kvc_ref[...]
    kvp = kvp_ref[...]
    qi = lax.broadcasted_iota(jnp.int32, (WINDOW, 2 * WINDOW), 0)
    kj = lax.broadcasted_iota(jnp.int32, (WINDOW, 2 * WINDOW), 1)
    first_key = jnp.where(blk > 0, 0, WINDOW)
    valid = (kj > qi) & (kj <= qi + WINDOW) & (kj >= first_key)
    scale = HEAD_DIM ** -0.5
    for g in range(N_KV_HEADS):
        ks = slice(g * HEAD_DIM, (g + 1) * HEAD_DIM)
        vs = slice(KV_WIDTH + g * HEAD_DIM, KV_WIDTH + (g + 1) * HEAD_DIM)
        k = jnp.concatenate([kvp[:, ks], kvc[:, ks]], axis=0)
        v = jnp.concatenate([kvp[:, vs], kvc[:, vs]], axis=0)
        for r in range(GQA_REP):
            h = g * GQA_REP + r
            qh = q[:, h * HEAD_DIM:(h + 1) * HEAD_DIM]
            s = lax.dot_general(qh, k, (((1,), (1,)), ((), ())), preferred_element_type=F32) * scale
            s = jnp.where(valid, s, NEG_INF)
            sink = sinks_ref[h]
            m = jnp.maximum(jnp.max(s, axis=-1, keepdims=True), sink)
            p = jnp.exp(s - m)
            denom = jnp.sum(p, axis=-1, keepdims=True) + jnp.exp(sink - m)
            o = jnp.dot(p.astype(BF16), v, preferred_element_type=F32) / denom
            o_ref[:, h * HEAD_DIM:(h + 1) * HEAD_DIM] = o.astype(BF16)


def _swa(qkv, sinks, batch, seq):
    nb = seq // WINDOW
    kv_col = Q_WIDTH // (2 * KV_WIDTH)
    return pl.pallas_call(
        _swa_body,
        grid=(batch, nb),
        in_specs=[
            pl.BlockSpec(memory_space=pltpu.SMEM),
            pl.BlockSpec((WINDOW, Q_WIDTH), lambda b, n: (b * nb + n, 0)),
            pl.BlockSpec((WINDOW, 2 * KV_WIDTH), lambda b, n: (b * nb + n, kv_col)),
            pl.BlockSpec((WINDOW, 2 * KV_WIDTH), lambda b, n: (b * nb + jnp.maximum(n - 1, 0), kv_col)),
        ],
        out_specs=pl.BlockSpec((WINDOW, Q_WIDTH), lambda b, n: (b * nb + n, 0)),
        out_shape=jax.ShapeDtypeStruct((batch * seq, Q_WIDTH), BF16),
        compiler_params=_params(("parallel", "arbitrary")),
        name="swa",
    )(sinks, qkv, qkv, qkv)


def _mem_attn_body(q_ref, kv_ref, o_ref):
    scale = MEM_HEAD_DIM ** -0.5
    for h in range(MEM_HEADS):
        cs = slice(h * MEM_HEAD_DIM, (h + 1) * MEM_HEAD_DIM)
        vs = slice(MEM_WIDTH + h * MEM_HEAD_DIM, MEM_WIDTH + (h + 1) * MEM_HEAD_DIM)
        s = lax.dot_general(q_ref[:, cs], kv_ref[:, cs], (((1,), (1,)), ((), ())),
                            preferred_element_type=F32) * scale
        m = jnp.max(s, axis=-1, keepdims=True)
        p = jnp.exp(s - m)
        denom = jnp.sum(p, axis=-1, keepdims=True)
        o = jnp.dot(p.astype(BF16), kv_ref[:, vs], preferred_element_type=F32) / denom
        o_ref[:, cs] = o.astype(BF16)


def _mem_attn(mq, mem_kv, batch, seq, *, tq=512):
    nq = seq // tq
    return pl.pallas_call(
        _mem_attn_body,
        grid=(batch, nq),
        in_specs=[
            pl.BlockSpec((tq, MEM_WIDTH), lambda b, i: (b * nq + i, 0)),
            pl.BlockSpec((N_MEM, 2 * MEM_WIDTH), lambda b, i: (b, 0)),
        ],
        out_specs=pl.BlockSpec((tq, MEM_WIDTH), lambda b, i: (b * nq + i, 0)),
        out_shape=jax.ShapeDtypeStruct((batch * seq, MEM_WIDTH), BF16),
        compiler_params=_params(("parallel", "arbitrary")),
        name="mem_attn",
    )(mq, mem_kv)


def _ssm_operators(lam_re, lam_im, log_dt, b_re, b_im, c_re, c_im, d_skip):
    hp = lax.Precision.HIGHEST
    t_len, g_n, p_n, ch = SSM_CHUNK, SSM_GROUPS, SSM_STATE, SSM_GROUP
    lr = jnp.minimum(lam_re, -1e-4)
    li = lam_im
    dt = jnp.exp(log_dt)[:, None]
    mag = jnp.exp(lr * dt)
    ar = mag * jnp.cos(li * dt)
    ai = mag * jnp.sin(li * dt)
    nr, ni = ar - 1.0, ai
    den = lr * lr + li * li
    kr = (nr * lr + ni * li) / den
    ki = (ni * lr - nr * li) / den
    bbr = kr[..., None] * b_re - ki[..., None] * b_im
    bbi = kr[..., None] * b_im + ki[..., None] * b_re
    steps = jnp.arange(t_len + 1, dtype=F32)[None, :, None]
    pmag = jnp.exp(steps * (lr * dt)[:, None, :])
    ang = steps * (li * dt)[:, None, :]
    pr = pmag * jnp.cos(ang)
    pi = pmag * jnp.sin(ang)
    wr = pr[:, :t_len, :, None] * bbr[:, None] - pi[:, :t_len, :, None] * bbi[:, None]
    wi = pr[:, :t_len, :, None] * bbi[:, None] + pi[:, :t_len, :, None] * bbr[:, None]
    lagk = (jnp.einsum('gcp,gkpd->gkcd', c_re, wr, precision=hp)
            - jnp.einsum('gcp,gkpd->gkcd', c_im, wi, precision=hp))
    nt, gt = SSM_LANE_TILES, SSM_TILE_GROUPS
    lag_c = lagk.reshape(nt, gt, t_len, ch, ch).transpose(0, 2, 1, 4, 3).reshape(nt, t_len * LANES, ch)

    def pair(x, y):
        return jnp.concatenate([x, y], axis=-1)

    def per_step(w):
        return w.reshape(nt, gt, t_len, 2 * p_n).transpose(0, 2, 1, 3)[:, :, :, None, :]

    def per_channel(w):
        return w.reshape(nt, gt, ch, 2 * p_n)[:, None]

    back = jnp.arange(t_len - 1, -1, -1, dtype=F32)[None, :, None]
    bmag = jnp.exp(back * (lr * dt)[:, None, :])
    bang = back * (li * dt)[:, None, :]
    qr, qi = bmag * jnp.cos(bang), bmag * jnp.sin(bang)
    bbr_t, bbi_t = bbr.transpose(0, 2, 1), bbi.transpose(0, 2, 1)
    inp = (per_step(pair(qr, qr)) * per_channel(pair(bbr_t, bbi_t))
           + per_step(pair(-qi, qi)) * per_channel(pair(bbi_t, bbr_t))
           ).reshape(nt, t_len * LANES, 2 * p_n)
    pr1, pi1 = pr[:, 1:], pi[:, 1:]
    outp = (per_step(pair(pr1, pi1)) * per_channel(pair(c_re, -c_re))
            - per_step(pair(pi1, pr1)) * per_channel(pair(c_im, c_im))
            ).reshape(nt, t_len * LANES, 2 * p_n)
    a1 = jnp.concatenate([pr[:, t_len], pr[:, t_len]], axis=-1).reshape(nt, gt, 2 * p_n)
    a2 = jnp.concatenate([-pi[:, t_len], pi[:, t_len]], axis=-1).reshape(nt, gt, 2 * p_n)
    return lag_c, inp, outp, a1, a2, d_skip.reshape(1, SSM_WIDTH)


def _ssm_body(s_ref, lag_ref, inp_ref, outp_ref, a1_ref, a2_ref, d_ref, y_ref,
              u_ref, panel_ref, inpx_ref, outpx_ref, z_ref, zs_ref, sp_ref, *, batch):
    t_len, gt = SSM_CHUNK, SSM_TILE_GROUPS
    n_chunks = u_ref.shape[0]
    per_seq = n_chunks // batch
    flat = t_len * LANES

    for t in range(t_len):
        u_ref[:, t * LANES:(t + 1) * LANES] = s_ref[pl.ds(t, n_chunks, stride=t_len), :].astype(BF16)

    row_group = (lax.broadcasted_iota(jnp.int32, (flat, 1), 0) >> 4) & (gt - 1)
    col_group = lax.broadcasted_iota(jnp.int32, (1, LANES), 1) >> 4
    spread = (lax.broadcasted_iota(jnp.int32, (SSM_GROUP, LANES), 1) & (SSM_GROUP - 1)
              == lax.broadcasted_iota(jnp.int32, (SSM_GROUP, LANES), 0)).astype(BF16)
    lag = jnp.dot(lag_ref[...].astype(BF16), spread, preferred_element_type=F32)
    lag = jnp.where(row_group == col_group, lag, 0.0).astype(BF16)

    for r in range(t_len):
        k_left, k_right = t_len - 2 - r, t_len - 1 - r
        left = lag[k_left * LANES:(k_left + 1) * LANES] if k_left >= 0 else jnp.zeros((LANES, LANES), BF16)
        panel_ref[r * LANES:(r + 1) * LANES, :LANES] = left
        panel_ref[r * LANES:(r + 1) * LANES, LANES:] = lag[k_right * LANES:(k_right + 1) * LANES]

    inp = inp_ref[...]
    outp = outp_ref[...]
    for g in range(gt):
        inpx_ref[:, g * LANES:(g + 1) * LANES] = jnp.where(row_group == g, inp, 0.0).astype(BF16)
        outpx_ref[:, g * LANES:(g + 1) * LANES] = jnp.where(row_group == g, outp, 0.0).astype(BF16)

    z = jnp.dot(u_ref[...], inpx_ref[...], preferred_element_type=F32)
    for g in range(gt):
        zg = z[:, g * LANES:(g + 1) * LANES]
        z_ref[pl.ds(g, n_chunks, stride=gt), :] = zg
        zs_ref[pl.ds(g, n_chunks, stride=gt), :] = pltpu.roll(zg, SSM_STATE, axis=1)

    a1 = a1_ref[...]
    a2 = a2_ref[...]

    def step(c, carry):
        new = []
        for b in range(batch):
            v0, v1 = carry[b]
            row = pl.multiple_of((b * per_seq + c) * gt, gt)
            sp_ref[pl.ds(row, gt), :] = v0
            z0 = z_ref[pl.ds(row, gt), :]
            z1 = zs_ref[pl.ds(row, gt), :]
            new.append((a1 * v0 + a2 * v1 + z0, a1 * v1 - a2 * v0 + z1))
        return tuple(new)

    zero = jnp.zeros((gt, LANES), F32)
    lax.fori_loop(0, per_seq, step, tuple((zero, zero) for _ in range(batch)), unroll=4)

    sp = jnp.concatenate([sp_ref[pl.ds(g, n_chunks, stride=gt), :] for g in range(gt)], axis=1).astype(BF16)
    d2 = jnp.concatenate([d_ref[...], d_ref[...]], axis=1)
    for q in range(t_len // 2):
        cols = slice(2 * q * LANES, (2 * q + 2) * LANES)
        k_len = (2 * q + 2) * LANES
        y = jnp.dot(u_ref[:, :k_len], panel_ref[flat - k_len:, :], preferred_element_type=F32)
        y = y + lax.dot_general(sp, outpx_ref[cols, :], (((1,), (1,)), ((), ())), preferred_element_type=F32)
        y = y + d2 * u_ref[:, cols].astype(F32)
        y_ref[pl.ds(2 * q, n_chunks, stride=t_len), :] = y[:, :LANES]
        y_ref[pl.ds(2 * q + 1, n_chunks, stride=t_len), :] = y[:, LANES:]


def _ssm(s_in, ops, layer, batch):
    lag_c, inp, outp, a1, a2, d_row = ops
    n = s_in.shape[0]
    n_chunks = n // SSM_CHUNK
    flat = SSM_CHUNK * LANES
    return pl.pallas_call(
        functools.partial(_ssm_body, batch=batch),
        grid=(SSM_LANE_TILES,),
        in_specs=[
            pl.BlockSpec((n, LANES), lambda j: (0, j)),
            pl.BlockSpec((None, None, flat, SSM_GROUP), lambda j: (layer, j, 0, 0)),
            pl.BlockSpec((None, None, flat, 2 * SSM_STATE), lambda j: (layer, j, 0, 0)),
            pl.BlockSpec((None, None, flat, 2 * SSM_STATE), lambda j: (layer, j, 0, 0)),
            pl.BlockSpec((None, None, SSM_TILE_GROUPS, 2 * SSM_STATE), lambda j: (layer, j, 0, 0)),
            pl.BlockSpec((None, None, SSM_TILE_GROUPS, 2 * SSM_STATE), lambda j: (layer, j, 0, 0)),
            pl.BlockSpec((None, 1, LANES), lambda j: (layer, 0, j)),
        ],
        out_specs=pl.BlockSpec((n, LANES), lambda j: (0, j)),
        out_shape=jax.ShapeDtypeStruct((n, SSM_WIDTH), F32),
        scratch_shapes=[
            pltpu.VMEM((n_chunks, flat), BF16),
            pltpu.VMEM((flat, 2 * LANES), BF16),
            pltpu.VMEM((flat, SSM_TILE_GROUPS * 2 * SSM_STATE), BF16),
            pltpu.VMEM((flat, SSM_TILE_GROUPS * 2 * SSM_STATE), BF16),
            pltpu.VMEM((n_chunks * SSM_TILE_GROUPS, 2 * SSM_STATE), F32),
            pltpu.VMEM((n_chunks * SSM_TILE_GROUPS, 2 * SSM_STATE), F32),
            pltpu.VMEM((n_chunks * SSM_TILE_GROUPS, 2 * SSM_STATE), F32),
        ],
        compiler_params=_params(("parallel",)),
        name="ssm",
    )(s_in, lag_c, inp, outp, a1, a2, d_row)


MERGE_ROW_CHUNK = 512


def _merge_body(h_ref, nw_ref, swa_ref, ssm_ref, mem_ref, wg0_ref, wg1_ref, wg2_ref, wswa_ref,
                wga_ref, wgb_ref, wmem_ref, wo_ref, o_ref, xn_ref):
    j = pl.program_id(1)

    @pl.when(j == 0)
    def _():
        h = h_ref[...]
        xn_ref[...] = _rms_normalize(h, nw_ref[...]).astype(BF16)
        o_ref[...] = h

    wg0 = wg0_ref[...].astype(BF16)
    wg1 = wg1_ref[...].astype(BF16)
    wg2 = wg2_ref[...].astype(BF16)
    for r in range(o_ref.shape[0] // MERGE_ROW_CHUNK):
        rows = pl.ds(r * MERGE_ROW_CHUNK, MERGE_ROW_CHUNK)
        xn = xn_ref[rows, :]
        ys = ssm_ref[rows, :].astype(BF16)
        y_swa = jnp.dot(swa_ref[rows, :], wswa_ref[...], preferred_element_type=F32)
        ga = jnp.dot(ys, wga_ref[...], preferred_element_type=F32)
        gb = jnp.dot(ys, wgb_ref[...], preferred_element_type=F32)
        y_ssm = ga * _sigmoid(gb)
        y_mem = jnp.dot(mem_ref[rows, :], wmem_ref[...], preferred_element_type=F32)
        merged = _sigmoid(jnp.dot(xn, wg0, preferred_element_type=F32)) * y_swa
        merged += _sigmoid(jnp.dot(xn, wg1, preferred_element_type=F32)) * y_ssm
        merged += _sigmoid(jnp.dot(xn, wg2, preferred_element_type=F32)) * y_mem
        o_ref[rows, :] += jnp.dot(merged.astype(BF16), wo_ref[...], preferred_element_type=F32)


def _merge(h, norm_w, o_swa, y_s, o_mem, w_in, layer, w_swa_up, w_ssm_glu, w_mem_up, w_out, *, tm=1024, tn=256):
    n = h.shape[0]
    nj = D_MODEL // tn
    g0 = GATE_OFFSET // tn
    row = lambda i, j: (i, 0)
    once = pl.Buffered(1)
    return pl.pallas_call(
        _merge_body,
        grid=(n // tm, nj),
        in_specs=[
            pl.BlockSpec((tm, D_MODEL), row, pipeline_mode=once),
            pl.BlockSpec((1, D_MODEL), lambda i, j: (0, 0)),
            pl.BlockSpec((tm, Q_WIDTH), row, pipeline_mode=once),
            pl.BlockSpec((tm, SSM_WIDTH), row, pipeline_mode=once),
            pl.BlockSpec((tm, MEM_WIDTH), row, pipeline_mode=once),
            pl.BlockSpec((None, D_MODEL, tn), lambda i, j: (layer, 0, g0 + j)),
            pl.BlockSpec((None, D_MODEL, tn), lambda i, j: (layer, 0, g0 + j + nj)),
            pl.BlockSpec((None, D_MODEL, tn), lambda i, j: (layer, 0, g0 + j + 2 * nj)),
            pl.BlockSpec((Q_WIDTH, tn), lambda i, j: (0, j)),
            pl.BlockSpec((SSM_WIDTH, tn), lambda i, j: (0, j)),
            pl.BlockSpec((SSM_WIDTH, tn), lambda i, j: (0, j + nj)),
            pl.BlockSpec((MEM_WIDTH, tn), lambda i, j: (0, j)),
            pl.BlockSpec((tn, D_MODEL), lambda i, j: (j, 0)),
        ],
        out_specs=pl.BlockSpec((tm, D_MODEL), row, pipeline_mode=once),
        out_shape=jax.ShapeDtypeStruct((n, D_MODEL), F32),
        scratch_shapes=[pltpu.VMEM((tm, D_MODEL), BF16)],
        compiler_params=_params(("parallel", "arbitrary")),
        name="merge",
    )(h, norm_w, o_swa, y_s, o_mem, w_in, w_in, w_in, w_swa_up, w_ssm_glu, w_ssm_glu, w_mem_up, w_out)


def kernel(x, mem, ffn1_norm, ffn1_w_in, ffn1_w_out, mix_norm, mem_norm, w_in, sinks, w_mem_kv, lam_re, lam_im, log_dt, b_re, b_im, c_re, c_im, d_skip, w_ssm_glu, w_swa_up, w_mem_up, w_out, ffn2_norm, ffn2_w_in, ffn2_w_out, final_norm):
    batch, seq = x.shape[0], x.shape[1]
    n = batch * seq
    h = x.reshape(n, D_MODEL)
    mem2 = mem.reshape(batch * N_MEM, D_MODEL)
    final_w = final_norm.reshape(1, D_MODEL)
    ssm_ops = jax.vmap(_ssm_operators)(lam_re, lam_im, log_dt, b_re, b_im, c_re, c_im, d_skip)
    for l in range(DEPTH):
        mix_w = mix_norm[l].reshape(1, D_MODEL)

        h = _ffn(h, ffn1_norm[l].reshape(1, D_MODEL), ffn1_w_in, ffn1_w_out, final_w, l, apply_final_norm=False)
        qkv, s_in, mq = _mix_proj(h, mix_w, w_in, l)
        mem_kv = _norm_proj(mem2, mem_norm[l].reshape(1, D_MODEL), w_mem_kv, l, 0, 2 * MEM_WIDTH, BF16,
                            tm=batch * N_MEM)
        o_swa = _swa(qkv, sinks[l], batch, seq)
        o_mem = _mem_attn(mq, mem_kv, batch, seq)
        y_s = _ssm(s_in, ssm_ops, l, batch)
        h = _merge(h, mix_w, o_swa, y_s, o_mem, w_in, l, w_swa_up[l].astype(BF16),
                   w_ssm_glu[l].astype(BF16), w_mem_up[l].astype(BF16), w_out[l].astype(BF16))
        h = _ffn(h, ffn2_norm[l].reshape(1, D_MODEL), ffn2_w_in, ffn2_w_out, final_w, l,
                 apply_final_norm=(l == DEPTH - 1))
    return h.reshape(batch, seq, D_MODEL)
```

```python
import functools
import math

import jax
import jax.numpy as jnp
from jax import lax
from jax.experimental import pallas as pl
from jax.experimental.pallas import tpu as pltpu

D_MODEL = 2048
DEPTH = 4
N_MEM = 256
D_FF = 5632
RMS_EPS = 1e-5

WINDOW = 128
HEAD_DIM = 64
N_Q_HEADS = 16
N_KV_HEADS = 4
GQA_REP = N_Q_HEADS // N_KV_HEADS
Q_WIDTH = N_Q_HEADS * HEAD_DIM
KV_WIDTH = N_KV_HEADS * HEAD_DIM

SSM_WIDTH = 1024
SSM_GROUP = 16
SSM_GROUPS = SSM_WIDTH // SSM_GROUP
SSM_STATE = 64
SSM_CHUNK = 16
LANES = 128
SSM_LANE_TILES = SSM_WIDTH // LANES
SSM_TILE_GROUPS = LANES // SSM_GROUP

MEM_HEADS = 4
MEM_HEAD_DIM = 256
MEM_WIDTH = MEM_HEADS * MEM_HEAD_DIM

N_BRANCHES = 3
NEG_INF = -1e30

QKV_WIDTH = Q_WIDTH + 2 * KV_WIDTH
SSM_OFFSET = QKV_WIDTH
MEMQ_OFFSET = SSM_OFFSET + SSM_WIDTH
GATE_OFFSET = MEMQ_OFFSET + MEM_WIDTH

VMEM_LIMIT_BYTES = 56 * 1024 * 1024

BF16 = jnp.bfloat16
F32 = jnp.float32


def _params(semantics):
    return pltpu.CompilerParams(dimension_semantics=semantics, vmem_limit_bytes=VMEM_LIMIT_BYTES)


def _rms_normalize(x, w):
    ms = jnp.mean(x * x, axis=-1, keepdims=True)
    return (x * lax.rsqrt(ms + RMS_EPS)) * w


def _sigmoid(x):
    return 1.0 / (1.0 + jnp.exp(-x))


FFN_ROW_CHUNK = 512


class _RowTileIO:
    def __init__(self, h_hbm, out_hbm, hbuf, acc_ref, sems):
        self.h_hbm, self.out_hbm, self.hbuf, self.acc_ref, self.sems = h_hbm, out_hbm, hbuf, acc_ref, sems
        self.tm = acc_ref.shape[0]

    def _fetch(self, tile):
        return pltpu.make_async_copy(self.h_hbm.at[pl.ds(tile * self.tm, self.tm), :], self.hbuf, self.sems.at[0])

    def _write_back(self, tile):
        return pltpu.make_async_copy(self.acc_ref, self.out_hbm.at[pl.ds(tile * self.tm, self.tm), :],
                                     self.sems.at[1])

    def begin(self, xn_ref, nw_ref):
        i = pl.program_id(0)

        @pl.when(i == 0)
        def _():
            self._fetch(0).start()

        self._fetch(i).wait()
        xn_ref[...] = _rms_normalize(self.hbuf[...], nw_ref[...]).astype(BF16)

        @pl.when(i > 0)
        def _():
            self._write_back(i - 1).wait()

        self.acc_ref[...] = self.hbuf[...]

        @pl.when(i + 1 < pl.num_programs(0))
        def _():
            self._fetch(i + 1).start()

    def end(self):
        i = pl.program_id(0)
        self._write_back(i).start()

        @pl.when(i == pl.num_programs(0) - 1)
        def _():
            self._write_back(i).wait()


def _ffn_body(h_hbm, nw_ref, wg_ref, wu_ref, wo_ref, fw_ref, out_hbm, xn_ref, hbuf, acc_ref, sems, *,
              apply_final_norm):
    j = pl.program_id(1)
    io = _RowTileIO(h_hbm, out_hbm, hbuf, acc_ref, sems)

    @pl.when(j == 0)
    def _():
        io.begin(xn_ref, nw_ref)

    wg = wg_ref[...].astype(BF16)
    wu = wu_ref[...].astype(BF16)
    wo = wo_ref[...].astype(BF16)
    for r in range(acc_ref.shape[0] // FFN_ROW_CHUNK):
        rows = pl.ds(r * FFN_ROW_CHUNK, FFN_ROW_CHUNK)
        xn = xn_ref[rows, :]
        g = jnp.dot(xn, wg, preferred_element_type=F32)
        u = jnp.dot(xn, wu, preferred_element_type=F32)
        a = ((0.5 * g) * _sigmoid(g)) * u
        acc_ref[rows, :] += jnp.dot(a.astype(BF16), wo, preferred_element_type=F32)

    @pl.when(j == pl.num_programs(1) - 1)
    def _():
        if apply_final_norm:
            acc_ref[...] = _rms_normalize(acc_ref[...], fw_ref[...])
        io.end()


def _ffn(h, norm_w, w_in, w_out, final_w, layer, *, apply_final_norm, tm=1024, tf=512):
    n = h.shape[0]
    nf = D_FF // tf
    return pl.pallas_call(
        functools.partial(_ffn_body, apply_final_norm=apply_final_norm),
        grid=(n // tm, nf),
        in_specs=[
            pl.BlockSpec(memory_space=pl.ANY),
            pl.BlockSpec((1, D_MODEL), lambda i, j: (0, 0)),
            pl.BlockSpec((None, D_MODEL, tf), lambda i, j: (layer, 0, j)),
            pl.BlockSpec((None, D_MODEL, tf), lambda i, j: (layer, 0, j + nf)),
            pl.BlockSpec((None, tf, D_MODEL), lambda i, j: (layer, j, 0)),
            pl.BlockSpec((1, D_MODEL), lambda i, j: (0, 0)),
        ],
        out_specs=pl.BlockSpec(memory_space=pl.ANY),
        out_shape=jax.ShapeDtypeStruct((n, D_MODEL), F32),
        scratch_shapes=[pltpu.VMEM((tm, D_MODEL), BF16), pltpu.VMEM((tm, D_MODEL), F32),
                        pltpu.VMEM((tm, D_MODEL), F32), pltpu.SemaphoreType.DMA((2,))],
        compiler_params=_params(("arbitrary", "arbitrary")),
        name="ffn",
    )(h, norm_w, w_in, w_in, w_out, final_w)


def _proj_body(h_ref, nw_ref, w_ref, o_ref, xn_ref):
    @pl.when(pl.program_id(1) == 0)
    def _():
        xn_ref[...] = _rms_normalize(h_ref[...], nw_ref[...]).astype(BF16)

    o_ref[...] = jnp.dot(xn_ref[...], w_ref[...].astype(BF16), preferred_element_type=F32).astype(o_ref.dtype)


def _norm_proj(h, norm_w, w, layer, col0, width, out_dtype, *, tm, tn=512):
    n = h.shape[0]
    c0 = col0 // tn
    return pl.pallas_call(
        _proj_body,
        grid=(n // tm, width // tn),
        in_specs=[
            pl.BlockSpec((tm, D_MODEL), lambda i, j: (i, 0)),
            pl.BlockSpec((1, D_MODEL), lambda i, j: (0, 0)),
            pl.BlockSpec((None, D_MODEL, tn), lambda i, j: (layer, 0, c0 + j)),
        ],
        out_specs=pl.BlockSpec((tm, tn), lambda i, j: (i, j)),
        out_shape=jax.ShapeDtypeStruct((n, width), out_dtype),
        scratch_shapes=[pltpu.VMEM((tm, D_MODEL), BF16)],
        compiler_params=_params(("parallel", "arbitrary")),
        name="norm_proj",
    )(h, norm_w, w)


def _mix_proj_body(h_ref, nw_ref, w_ref, qkv_ref, s_ref, mq_ref, xn_ref, *, n_qkv, n_ssm):
    j = pl.program_id(1)

    @pl.when(j == 0)
    def _():
        xn_ref[...] = _rms_normalize(h_ref[...], nw_ref[...]).astype(BF16)

    y = jnp.dot(xn_ref[...], w_ref[...].astype(BF16), preferred_element_type=F32)

    @pl.when(j < n_qkv)
    def _():
        qkv_ref[...] = y.astype(BF16)

    @pl.when((j >= n_qkv) & (j < n_qkv + n_ssm))
    def _():
        s_ref[...] = y

    @pl.when(j >= n_qkv + n_ssm)
    def _():
        mq_ref[...] = y.astype(BF16)


def _mix_proj(h, norm_w, w_in, layer, *, tm=1024, tn=512):
    n = h.shape[0]
    n_qkv, n_ssm, n_mq = QKV_WIDTH // tn, SSM_WIDTH // tn, MEM_WIDTH // tn
    return pl.pallas_call(
        functools.partial(_mix_proj_body, n_qkv=n_qkv, n_ssm=n_ssm),
        grid=(n // tm, n_qkv + n_ssm + n_mq),
        in_specs=[
            pl.BlockSpec((tm, D_MODEL), lambda i, j: (i, 0)),
            pl.BlockSpec((1, D_MODEL), lambda i, j: (0, 0)),
            pl.BlockSpec((None, D_MODEL, tn), lambda i, j: (layer, 0, j)),
        ],
        out_specs=[
            pl.BlockSpec((tm, tn), lambda i, j: (i, jnp.minimum(j, n_qkv - 1))),
            pl.BlockSpec((tm, tn), lambda i, j: (i, jnp.clip(j - n_qkv, 0, n_ssm - 1))),
            pl.BlockSpec((tm, tn), lambda i, j: (i, jnp.clip(j - n_qkv - n_ssm, 0, n_mq - 1))),
        ],
        out_shape=[
            jax.ShapeDtypeStruct((n, QKV_WIDTH), BF16),
            jax.ShapeDtypeStruct((n, SSM_WIDTH), F32),
            jax.ShapeDtypeStruct((n, MEM_WIDTH), BF16),
        ],
        scratch_shapes=[pltpu.VMEM((tm, D_MODEL), BF16)],
        compiler_params=_params(("parallel", "arbitrary")),
        name="mix_proj",
    )(h, norm_w, w_in)


def _swa_body(sinks_ref, q_ref, kvc_ref, kvp_ref, o_ref):
    blk = pl.program_id(1)
    q = q_ref[...]
    kvc = kvc_ref[...]
    kvp = kvp_ref[...]
    qi = lax.broadcasted_iota(jnp.int32, (WINDOW, 2 * WINDOW), 0)
    kj = lax.broadcasted_iota(jnp.int32, (WINDOW, 2 * WINDOW), 1)
    first_key = jnp.where(blk > 0, 0, WINDOW)
    valid = (kj > qi) & (kj <= qi + WINDOW) & (kj >= first_key)
    scale = HEAD_DIM ** -0.5
    for g in range(N_KV_HEADS):
        ks = slice(g * HEAD_DIM, (g + 1) * HEAD_DIM)
        vs = slice(KV_WIDTH + g * HEAD_DIM, KV_WIDTH + (g + 1) * HEAD_DIM)
        k = jnp.concatenate([kvp[:, ks], kvc[:, ks]], axis=0)
        v = jnp.concatenate([kvp[:, vs], kvc[:, vs]], axis=0)
        for r in range(GQA_REP):
            h = g * GQA_REP + r
            qh = q[:, h * HEAD_DIM:(h + 1) * HEAD_DIM]
            s = lax.dot_general(qh, k, (((1,), (1,)), ((), ())), preferred_element_type=F32) * scale
            s = jnp.where(valid, s, NEG_INF)
            sink = sinks_ref[h]
            m = jnp.maximum(jnp.max(s, axis=-1, keepdims=True), sink)
            p = jnp.exp(s - m)
            denom = jnp.sum(p, axis=-1, keepdims=True) + jnp.exp(sink - m)
            o = jnp.dot(p.astype(BF16), v, preferred_element_type=F32) / denom
            o_ref[:, h * HEAD_DIM:(h + 1) * HEAD_DIM] = o.astype(BF16)


def _swa(qkv, sinks, batch, seq):
    nb = seq // WINDOW
    kv_col = Q_WIDTH // (2 * KV_WIDTH)
    return pl.pallas_call(
        _swa_body,
        grid=(batch, nb),
        in_specs=[
            pl.BlockSpec(memory_space=pltpu.SMEM),
            pl.BlockSpec((WINDOW, Q_WIDTH), lambda b, n: (b * nb + n, 0)),
            pl.BlockSpec((WINDOW, 2 * KV_WIDTH), lambda b, n: (b * nb + n, kv_col)),
            pl.BlockSpec((WINDOW, 2 * KV_WIDTH), lambda b, n: (b * nb + jnp.maximum(n - 1, 0), kv_col)),
        ],
        out_specs=pl.BlockSpec((WINDOW, Q_WIDTH), lambda b, n: (b * nb + n, 0)),
        out_shape=jax.ShapeDtypeStruct((batch * seq, Q_WIDTH), BF16),
        compiler_params=_params(("parallel", "arbitrary")),
        name="swa",
    )(sinks, qkv, qkv, qkv)


def _mem_attn_body(q_ref, kv_ref, o_ref):
    scale = MEM_HEAD_DIM ** -0.5
    for h in range(MEM_HEADS):
        cs = slice(h * MEM_HEAD_DIM, (h + 1) * MEM_HEAD_DIM)
        vs = slice(MEM_WIDTH + h * MEM_HEAD_DIM, MEM_WIDTH + (h + 1) * MEM_HEAD_DIM)
        s = lax.dot_general(q_ref[:, cs], kv_ref[:, cs], (((1,), (1,)), ((), ())),
                            preferred_element_type=F32) * scale
        m = jnp.max(s, axis=-1, keepdims=True)
        p = jnp.exp(s - m)
        denom = jnp.sum(p, axis=-1, keepdims=True)
        o = jnp.dot(p.astype(BF16), kv_ref[:, vs], preferred_element_type=F32) / denom
        o_ref[:, cs] = o.astype(BF16)


def _mem_attn(mq, mem_kv, batch, seq, *, tq=512):
    nq = seq // tq
    return pl.pallas_call(
        _mem_attn_body,
        grid=(batch, nq),
        in_specs=[
            pl.BlockSpec((tq, MEM_WIDTH), lambda b, i: (b * nq + i, 0)),
            pl.BlockSpec((N_MEM, 2 * MEM_WIDTH), lambda b, i: (b, 0)),
        ],
        out_specs=pl.BlockSpec((tq, MEM_WIDTH), lambda b, i: (b * nq + i, 0)),
        out_shape=jax.ShapeDtypeStruct((batch * seq, MEM_WIDTH), BF16),
        compiler_params=_params(("parallel", "arbitrary")),
        name="mem_attn",
    )(mq, mem_kv)


def _ssm_operators(lam_re, lam_im, log_dt, b_re, b_im, c_re, c_im, d_skip):
    hp = lax.Precision.HIGHEST
    t_len, g_n, p_n, ch = SSM_CHUNK, SSM_GROUPS, SSM_STATE, SSM_GROUP
    lr = jnp.minimum(lam_re, -1e-4)
    li = lam_im
    dt = jnp.exp(log_dt)[:, None]
    mag = jnp.exp(lr * dt)
    ar = mag * jnp.cos(li * dt)
    ai = mag * jnp.sin(li * dt)
    nr, ni = ar - 1.0, ai
    den = lr * lr + li * li
    kr = (nr * lr + ni * li) / den
    ki = (ni * lr - nr * li) / den
    bbr = kr[..., None] * b_re - ki[..., None] * b_im
    bbi = kr[..., None] * b_im + ki[..., None] * b_re
    steps = jnp.arange(t_len + 1, dtype=F32)[None, :, None]
    pmag = jnp.exp(steps * (lr * dt)[:, None, :])
    ang = steps * (li * dt)[:, None, :]
    pr = pmag * jnp.cos(ang)
    pi = pmag * jnp.sin(ang)
    wr = pr[:, :t_len, :, None] * bbr[:, None] - pi[:, :t_len, :, None] * bbi[:, None]
    wi = pr[:, :t_len, :, None] * bbi[:, None] + pi[:, :t_len, :, None] * bbr[:, None]
    lagk = (jnp.einsum('gcp,gkpd->gkcd', c_re, wr, precision=hp)
            - jnp.einsum('gcp,gkpd->gkcd', c_im, wi, precision=hp))
    nt, gt = SSM_LANE_TILES, SSM_TILE_GROUPS
    lag_c = lagk.reshape(nt, gt, t_len, ch, ch).transpose(0, 2, 1, 4, 3).reshape(nt, t_len * LANES, ch)

    def pair(x, y):
        return jnp.concatenate([x, y], axis=-1)

    def per_step(w):
        return w.reshape(nt, gt, t_len, 2 * p_n).transpose(0, 2, 1, 3)[:, :, :, None, :]

    def per_channel(w):
        return w.reshape(nt, gt, ch, 2 * p_n)[:, None]

    back = jnp.arange(t_len - 1, -1, -1, dtype=F32)[None, :, None]
    bmag = jnp.exp(back * (lr * dt)[:, None, :])
    bang = back * (li * dt)[:, None, :]
    qr, qi = bmag * jnp.cos(bang), bmag * jnp.sin(bang)
    bbr_t, bbi_t = bbr.transpose(0, 2, 1), bbi.transpose(0, 2, 1)
    inp = (per_step(pair(qr, qr)) * per_channel(pair(bbr_t, bbi_t))
           + per_step(pair(-qi, qi)) * per_channel(pair(bbi_t, bbr_t))
           ).reshape(nt, t_len * LANES, 2 * p_n)
    pr1, pi1 = pr[:, 1:], pi[:, 1:]
    outp = (per_step(pair(pr1, pi1)) * per_channel(pair(c_re, -c_re))
            - per_step(pair(pi1, pr1)) * per_channel(pair(c_im, c_im))
            ).reshape(nt, t_len * LANES, 2 * p_n)
    a1 = jnp.concatenate([pr[:, t_len], pr[:, t_len]], axis=-1).reshape(nt, gt, 2 * p_n)
    a2 = jnp.concatenate([-pi[:, t_len], pi[:, t_len]], axis=-1).reshape(nt, gt, 2 * p_n)
    return lag_c, inp, outp, a1, a2, d_skip.reshape(1, SSM_WIDTH)


def _ssm_body(s_ref, lag_ref, inp_ref, outp_ref, a1_ref, a2_ref, d_ref, y_ref,
              u_ref, panel_ref, inpx_ref, outpx_ref, z_ref, zs_ref, sp_ref, *, batch):
    t_len, gt = SSM_CHUNK, SSM_TILE_GROUPS
    n_chunks = u_ref.shape[0]
    per_seq = n_chunks // batch
    flat = t_len * LANES

    for t in range(t_len):
        u_ref[:, t * LANES:(t + 1) * LANES] = s_ref[pl.ds(t, n_chunks, stride=t_len), :].astype(BF16)

    row_group = (lax.broadcasted_iota(jnp.int32, (flat, 1), 0) >> 4) & (gt - 1)
    col_group = lax.broadcasted_iota(jnp.int32, (1, LANES), 1) >> 4
    spread = (lax.broadcasted_iota(jnp.int32, (SSM_GROUP, LANES), 1) & (SSM_GROUP - 1)
              == lax.broadcasted_iota(jnp.int32, (SSM_GROUP, LANES), 0)).astype(BF16)
    lag = jnp.dot(lag_ref[...].astype(BF16), spread, preferred_element_type=F32)
    lag = jnp.where(row_group == col_group, lag, 0.0).astype(BF16)

    for r in range(t_len):
        k_left, k_right = t_len - 2 - r, t_len - 1 - r
        left = lag[k_left * LANES:(k_left + 1) * LANES] if k_left >= 0 else jnp.zeros((LANES, LANES), BF16)
        panel_ref[r * LANES:(r + 1) * LANES, :LANES] = left
        panel_ref[r * LANES:(r + 1) * LANES, LANES:] = lag[k_right * LANES:(k_right + 1) * LANES]

    inp = inp_ref[...]
    outp = outp_ref[...]
    for g in range(gt):
        inpx_ref[:, g * LANES:(g + 1) * LANES] = jnp.where(row_group == g, inp, 0.0).astype(BF16)
        outpx_ref[:, g * LANES:(g + 1) * LANES] = jnp.where(row_group == g, outp, 0.0).astype(BF16)

    z = jnp.dot(u_ref[...], inpx_ref[...], preferred_element_type=F32)
    for g in range(gt):
        zg = z[:, g * LANES:(g + 1) * LANES]
        z_ref[pl.ds(g, n_chunks, stride=gt), :] = zg
        zs_ref[pl.ds(g, n_chunks, stride=gt), :] = pltpu.roll(zg, SSM_STATE, axis=1)

    a1 = a1_ref[...]
    a2 = a2_ref[...]

    def step(c, carry):
        new = []
        for b in range(batch):
            v0, v1 = carry[b]
            row = pl.multiple_of((b * per_seq + c) * gt, gt)
            sp_ref[pl.ds(row, gt), :] = v0
            z0 = z_ref[pl.ds(row, gt), :]
            z1 = zs_ref[pl.ds(row, gt), :]
            new.append((a1 * v0 + a2 * v1 + z0, a1 * v1 - a2 * v0 + z1))
        return tuple(new)

    zero = jnp.zeros((gt, LANES), F32)
    lax.fori_loop(0, per_seq, step, tuple((zero, zero) for _ in range(batch)), unroll=4)

    sp = jnp.concatenate([sp_ref[pl.ds(g, n_chunks, stride=gt), :] for g in range(gt)], axis=1).astype(BF16)
    d2 = jnp.concatenate([d_ref[...], d_ref[...]], axis=1)
    for q in range(t_len // 2):
        cols = slice(2 * q * LANES, (2 * q + 2) * LANES)
        k_len = (2 * q + 2) * LANES
        y = jnp.dot(u_ref[:, :k_len], panel_ref[flat - k_len:, :], preferred_element_type=F32)
        y = y + lax.dot_general(sp, outpx_ref[cols, :], (((1,), (1,)), ((), ())), preferred_element_type=F32)
        y = y + d2 * u_ref[:, cols].astype(F32)
        y_ref[pl.ds(2 * q, n_chunks, stride=t_len), :] = y[:, :LANES]
        y_ref[pl.ds(2 * q + 1, n_chunks, stride=t_len), :] = y[:, LANES:]


def _ssm(s_in, ops, layer, batch):
    lag_c, inp, outp, a1, a2, d_row = ops
    n = s_in.shape[0]
    n_chunks = n // SSM_CHUNK
    flat = SSM_CHUNK * LANES
    return pl.pallas_call(
        functools.partial(_ssm_body, batch=batch),
        grid=(SSM_LANE_TILES,),
        in_specs=[
            pl.BlockSpec((n, LANES), lambda j: (0, j)),
            pl.BlockSpec((None, None, flat, SSM_GROUP), lambda j: (layer, j, 0, 0)),
            pl.BlockSpec((None, None, flat, 2 * SSM_STATE), lambda j: (layer, j, 0, 0)),
            pl.BlockSpec((None, None, flat, 2 * SSM_STATE), lambda j: (layer, j, 0, 0)),
            pl.BlockSpec((None, None, SSM_TILE_GROUPS, 2 * SSM_STATE), lambda j: (layer, j, 0, 0)),
            pl.BlockSpec((None, None, SSM_TILE_GROUPS, 2 * SSM_STATE), lambda j: (layer, j, 0, 0)),
            pl.BlockSpec((None, 1, LANES), lambda j: (layer, 0, j)),
        ],
        out_specs=pl.BlockSpec((n, LANES), lambda j: (0, j)),
        out_shape=jax.ShapeDtypeStruct((n, SSM_WIDTH), F32),
        scratch_shapes=[
            pltpu.VMEM((n_chunks, flat), BF16),
            pltpu.VMEM((flat, 2 * LANES), BF16),
            pltpu.VMEM((flat, SSM_TILE_GROUPS * 2 * SSM_STATE), BF16),
            pltpu.VMEM((flat, SSM_TILE_GROUPS * 2 * SSM_STATE), BF16),
            pltpu.VMEM((n_chunks * SSM_TILE_GROUPS, 2 * SSM_STATE), F32),
            pltpu.VMEM((n_chunks * SSM_TILE_GROUPS, 2 * SSM_STATE), F32),
            pltpu.VMEM((n_chunks * SSM_TILE_GROUPS, 2 * SSM_STATE), F32),
        ],
        compiler_params=_params(("parallel",)),
        name="ssm",
    )(s_in, lag_c, inp, outp, a1, a2, d_row)


MERGE_ROW_CHUNK = 512


def _merge_body(h_hbm, nw_ref, swa_ref, ssm_ref, mem_ref, wg0_ref, wg1_ref, wg2_ref, wswa_ref,
                wga_ref, wgb_ref, wmem_ref, wo_ref, out_hbm, xn_ref, hbuf, o_ref, sems):
    j = pl.program_id(1)
    io = _RowTileIO(h_hbm, out_hbm, hbuf, o_ref, sems)

    @pl.when(j == 0)
    def _():
        io.begin(xn_ref, nw_ref)

    wg0 = wg0_ref[...].astype(BF16)
    wg1 = wg1_ref[...].astype(BF16)
    wg2 = wg2_ref[...].astype(BF16)
    for r in range(o_ref.shape[0] // MERGE_ROW_CHUNK):
        rows = pl.ds(r * MERGE_ROW_CHUNK, MERGE_ROW_CHUNK)
        xn = xn_ref[rows, :]
        ys = ssm_ref[rows, :].astype(BF16)
        y_swa = jnp.dot(swa_ref[rows, :], wswa_ref[...], preferred_element_type=F32)
        ga = jnp.dot(ys, wga_ref[...], preferred_element_type=F32)
        gb = jnp.dot(ys, wgb_ref[...], preferred_element_type=F32)
        y_ssm = ga * _sigmoid(gb)
        y_mem = jnp.dot(mem_ref[rows, :], wmem_ref[...], preferred_element_type=F32)
        merged = _sigmoid(jnp.dot(xn, wg0, preferred_element_type=F32)) * y_swa
        merged += _sigmoid(jnp.dot(xn, wg1, preferred_element_type=F32)) * y_ssm
        merged += _sigmoid(jnp.dot(xn, wg2, preferred_element_type=F32)) * y_mem
        o_ref[rows, :] += jnp.dot(merged.astype(BF16), wo_ref[...], preferred_element_type=F32)

    @pl.when(j == pl.num_programs(1) - 1)
    def _():
        io.end()


def _merge(h, norm_w, o_swa, y_s, o_mem, w_in, layer, w_swa_up, w_ssm_glu, w_mem_up, w_out, *, tm=1024, tn=256):
    n = h.shape[0]
    nj = D_MODEL // tn
    g0 = GATE_OFFSET // tn
    row = lambda i, j: (i, 0)
    once = pl.Buffered(1)
    return pl.pallas_call(
        _merge_body,
        grid=(n // tm, nj),
        in_specs=[
            pl.BlockSpec(memory_space=pl.ANY),
            pl.BlockSpec((1, D_MODEL), lambda i, j: (0, 0)),
            pl.BlockSpec((tm, Q_WIDTH), row, pipeline_mode=once),
            pl.BlockSpec((tm, SSM_WIDTH), row, pipeline_mode=once),
            pl.BlockSpec((tm, MEM_WIDTH), row, pipeline_mode=once),
            pl.BlockSpec((None, D_MODEL, tn), lambda i, j: (layer, 0, g0 + j)),
            pl.BlockSpec((None, D_MODEL, tn), lambda i, j: (layer, 0, g0 + j + nj)),
            pl.BlockSpec((None, D_MODEL, tn), lambda i, j: (layer, 0, g0 + j + 2 * nj)),
            pl.BlockSpec((Q_WIDTH, tn), lambda i, j: (0, j)),
            pl.BlockSpec((SSM_WIDTH, tn), lambda i, j: (0, j)),
            pl.BlockSpec((SSM_WIDTH, tn), lambda i, j: (0, j + nj)),
            pl.BlockSpec((MEM_WIDTH, tn), lambda i, j: (0, j)),
            pl.BlockSpec((tn, D_MODEL), lambda i, j: (j, 0)),
        ],
        out_specs=pl.BlockSpec(memory_space=pl.ANY),
        out_shape=jax.ShapeDtypeStruct((n, D_MODEL), F32),
        scratch_shapes=[pltpu.VMEM((tm, D_MODEL), BF16), pltpu.VMEM((tm, D_MODEL), F32),
                        pltpu.VMEM((tm, D_MODEL), F32), pltpu.SemaphoreType.DMA((2,))],
        compiler_params=_params(("arbitrary", "arbitrary")),
        name="merge",
    )(h, norm_w, o_swa, y_s, o_mem, w_in, w_in, w_in, w_swa_up, w_ssm_glu, w_ssm_glu, w_mem_up, w_out)


def kernel(x, mem, ffn1_norm, ffn1_w_in, ffn1_w_out, mix_norm, mem_norm, w_in, sinks, w_mem_kv, lam_re, lam_im, log_dt, b_re, b_im, c_re, c_im, d_skip, w_ssm_glu, w_swa_up, w_mem_up, w_out, ffn2_norm, ffn2_w_in, ffn2_w_out, final_norm):
    batch, seq = x.shape[0], x.shape[1]
    n = batch * seq
    h = x.reshape(n, D_MODEL)
    mem2 = mem.reshape(batch * N_MEM, D_MODEL)
    final_w = final_norm.reshape(1, D_MODEL)
    ssm_ops = jax.vmap(_ssm_operators)(lam_re, lam_im, log_dt, b_re, b_im, c_re, c_im, d_skip)
    for l in range(DEPTH):
        mix_w = mix_norm[l].reshape(1, D_MODEL)

        h = _ffn(h, ffn1_norm[l].reshape(1, D_MODEL), ffn1_w_in, ffn1_w_out, final_w, l, apply_final_norm=False)
        qkv, s_in, mq = _mix_proj(h, mix_w, w_in, l)
        mem_kv = _norm_proj(mem2, mem_norm[l].reshape(1, D_MODEL), w_mem_kv, l, 0, 2 * MEM_WIDTH, BF16,
                            tm=batch * N_MEM)
        o_swa = _swa(qkv, sinks[l], batch, seq)
        o_mem = _mem_attn(mq, mem_kv, batch, seq)
        y_s = _ssm(s_in, ssm_ops, l, batch)
        h = _merge(h, mix_w, o_swa, y_s, o_mem, w_in, l, w_swa_up[l].astype(BF16),
                   w_ssm_glu[l].astype(BF16), w_mem_up[l].astype(BF16), w_out[l].astype(BF16))
        h = _ffn(h, ffn2_norm[l].reshape(1, D_MODEL), ffn2_w_in, ffn2_w_out, final_w, l,
                 apply_final_norm=(l == DEPTH - 1))
    return h.reshape(batch, seq, D_MODEL)
```

```python
import functools
import math

import jax
import jax.numpy as jnp
from jax import lax
from jax.experimental import pallas as pl
from jax.experimental.pallas import tpu as pltpu

D_MODEL = 2048
DEPTH = 4
N_MEM = 256
D_FF = 5632
RMS_EPS = 1e-5

WINDOW = 128
HEAD_DIM = 64
N_Q_HEADS = 16
N_KV_HEADS = 4
GQA_REP = N_Q_HEADS // N_KV_HEADS
Q_WIDTH = N_Q_HEADS * HEAD_DIM
KV_WIDTH = N_KV_HEADS * HEAD_DIM

SSM_WIDTH = 1024
SSM_GROUP = 16
SSM_GROUPS = SSM_WIDTH // SSM_GROUP
SSM_STATE = 64
SSM_CHUNK = 16
LANES = 128
SSM_LANE_TILES = SSM_WIDTH // LANES
SSM_TILE_GROUPS = LANES // SSM_GROUP

MEM_HEADS = 4
MEM_HEAD_DIM = 256
MEM_WIDTH = MEM_HEADS * MEM_HEAD_DIM

N_BRANCHES = 3
NEG_INF = -1e30

QKV_WIDTH = Q_WIDTH + 2 * KV_WIDTH
SSM_OFFSET = QKV_WIDTH
MEMQ_OFFSET = SSM_OFFSET + SSM_WIDTH
GATE_OFFSET = MEMQ_OFFSET + MEM_WIDTH

VMEM_LIMIT_BYTES = 56 * 1024 * 1024

BF16 = jnp.bfloat16
F32 = jnp.float32


def _params(semantics):
    return pltpu.CompilerParams(dimension_semantics=semantics, vmem_limit_bytes=VMEM_LIMIT_BYTES)


def _rms_normalize(x, w):
    ms = jnp.mean(x * x, axis=-1, keepdims=True)
    return (x * lax.rsqrt(ms + RMS_EPS)) * w


def _sigmoid(x):
    return 0.5 * jnp.tanh(0.5 * x) + 0.5


FFN_ROW_CHUNK = 512


class _RowTileIO:
    def __init__(self, h_hbm, out_hbm, hbuf, acc_ref, sems):
        self.h_hbm, self.out_hbm, self.hbuf, self.acc_ref, self.sems = h_hbm, out_hbm, hbuf, acc_ref, sems
        self.tm = acc_ref.shape[0]

    def _fetch(self, tile):
        return pltpu.make_async_copy(self.h_hbm.at[pl.ds(tile * self.tm, self.tm), :], self.hbuf, self.sems.at[0])

    def _write_back(self, tile):
        return pltpu.make_async_copy(self.acc_ref, self.out_hbm.at[pl.ds(tile * self.tm, self.tm), :],
                                     self.sems.at[1])

    def begin(self, xn_ref, nw_ref):
        i = pl.program_id(0)

        @pl.when(i == 0)
        def _():
            self._fetch(0).start()

        self._fetch(i).wait()
        xn_ref[...] = _rms_normalize(self.hbuf[...], nw_ref[...]).astype(BF16)

        @pl.when(i > 0)
        def _():
            self._write_back(i - 1).wait()

        self.acc_ref[...] = self.hbuf[...]

        @pl.when(i + 1 < pl.num_programs(0))
        def _():
            self._fetch(i + 1).start()

    def end(self):
        i = pl.program_id(0)
        self._write_back(i).start()

        @pl.when(i == pl.num_programs(0) - 1)
        def _():
            self._write_back(i).wait()


def _ffn_body(h_hbm, nw_ref, wg_ref, wu_ref, wo_ref, fw_ref, out_hbm, xn_ref, hbuf, acc_ref, sems, *,
              apply_final_norm):
    j = pl.program_id(1)
    io = _RowTileIO(h_hbm, out_hbm, hbuf, acc_ref, sems)

    @pl.when(j == 0)
    def _():
        io.begin(xn_ref, nw_ref)

    wg = wg_ref[...].astype(BF16)
    wu = wu_ref[...].astype(BF16)
    wo = wo_ref[...].astype(BF16)
    for r in range(acc_ref.shape[0] // FFN_ROW_CHUNK):
        rows = pl.ds(r * FFN_ROW_CHUNK, FFN_ROW_CHUNK)
        xn = xn_ref[rows, :]
        g = jnp.dot(xn, wg, preferred_element_type=F32)
        u = jnp.dot(xn, wu, preferred_element_type=F32)
        a = ((0.5 * g) * _sigmoid(g)) * u
        acc_ref[rows, :] += jnp.dot(a.astype(BF16), wo, preferred_element_type=F32)

    @pl.when(j == pl.num_programs(1) - 1)
    def _():
        if apply_final_norm:
            acc_ref[...] = _rms_normalize(acc_ref[...], fw_ref[...])
        io.end()


def _ffn(h, norm_w, w_in, w_out, final_w, layer, *, apply_final_norm, tm=1024, tf=512):
    n = h.shape[0]
    nf = D_FF // tf
    return pl.pallas_call(
        functools.partial(_ffn_body, apply_final_norm=apply_final_norm),
        grid=(n // tm, nf),
        in_specs=[
            pl.BlockSpec(memory_space=pl.ANY),
            pl.BlockSpec((1, D_MODEL), lambda i, j: (0, 0)),
            pl.BlockSpec((None, D_MODEL, tf), lambda i, j: (layer, 0, j)),
            pl.BlockSpec((None, D_MODEL, tf), lambda i, j: (layer, 0, j + nf)),
            pl.BlockSpec((None, tf, D_MODEL), lambda i, j: (layer, j, 0)),
            pl.BlockSpec((1, D_MODEL), lambda i, j: (0, 0)),
        ],
        out_specs=pl.BlockSpec(memory_space=pl.ANY),
        out_shape=jax.ShapeDtypeStruct((n, D_MODEL), F32),
        scratch_shapes=[pltpu.VMEM((tm, D_MODEL), BF16), pltpu.VMEM((tm, D_MODEL), F32),
                        pltpu.VMEM((tm, D_MODEL), F32), pltpu.SemaphoreType.DMA((2,))],
        compiler_params=_params(("arbitrary", "arbitrary")),
        name="ffn",
    )(h, norm_w, w_in, w_in, w_out, final_w)


def _proj_body(h_ref, nw_ref, w_ref, o_ref, xn_ref):
    @pl.when(pl.program_id(1) == 0)
    def _():
        xn_ref[...] = _rms_normalize(h_ref[...], nw_ref[...]).astype(BF16)

    o_ref[...] = jnp.dot(xn_ref[...], w_ref[...].astype(BF16), preferred_element_type=F32).astype(o_ref.dtype)


def _norm_proj(h, norm_w, w, layer, col0, width, out_dtype, *, tm, tn=512):
    n = h.shape[0]
    c0 = col0 // tn
    return pl.pallas_call(
        _proj_body,
        grid=(n // tm, width // tn),
        in_specs=[
            pl.BlockSpec((tm, D_MODEL), lambda i, j: (i, 0)),
            pl.BlockSpec((1, D_MODEL), lambda i, j: (0, 0)),
            pl.BlockSpec((None, D_MODEL, tn), lambda i, j: (layer, 0, c0 + j)),
        ],
        out_specs=pl.BlockSpec((tm, tn), lambda i, j: (i, j)),
        out_shape=jax.ShapeDtypeStruct((n, width), out_dtype),
        scratch_shapes=[pltpu.VMEM((tm, D_MODEL), BF16)],
        compiler_params=_params(("parallel", "arbitrary")),
        name="norm_proj",
    )(h, norm_w, w)


def _mix_proj_body(h_ref, nw_ref, w_ref, qkv_ref, s_ref, mq_ref, xn_ref, *, n_qkv, n_ssm):
    j = pl.program_id(1)

    @pl.when(j == 0)
    def _():
        xn_ref[...] = _rms_normalize(h_ref[...], nw_ref[...]).astype(BF16)

    y = jnp.dot(xn_ref[...], w_ref[...].astype(BF16), preferred_element_type=F32)

    @pl.when(j < n_qkv)
    def _():
        qkv_ref[...] = y.astype(BF16)

    @pl.when((j >= n_qkv) & (j < n_qkv + n_ssm))
    def _():
        s_ref[...] = y

    @pl.when(j >= n_qkv + n_ssm)
    def _():
        mq_ref[...] = y.astype(BF16)


def _mix_proj(h, norm_w, w_in, layer, *, tm=2048, tn=512):
    n = h.shape[0]
    n_qkv, n_ssm, n_mq = QKV_WIDTH // tn, SSM_WIDTH // tn, MEM_WIDTH // tn
    return pl.pallas_call(
        functools.partial(_mix_proj_body, n_qkv=n_qkv, n_ssm=n_ssm),
        grid=(n // tm, n_qkv + n_ssm + n_mq),
        in_specs=[
            pl.BlockSpec((tm, D_MODEL), lambda i, j: (i, 0), pipeline_mode=pl.Buffered(1)),
            pl.BlockSpec((1, D_MODEL), lambda i, j: (0, 0)),
            pl.BlockSpec((None, D_MODEL, tn), lambda i, j: (layer, 0, j)),
        ],
        out_specs=[
            pl.BlockSpec((tm, tn), lambda i, j: (i, jnp.minimum(j, n_qkv - 1))),
            pl.BlockSpec((tm, tn), lambda i, j: (i, jnp.clip(j - n_qkv, 0, n_ssm - 1))),
            pl.BlockSpec((tm, tn), lambda i, j: (i, jnp.clip(j - n_qkv - n_ssm, 0, n_mq - 1))),
        ],
        out_shape=[
            jax.ShapeDtypeStruct((n, QKV_WIDTH), BF16),
            jax.ShapeDtypeStruct((n, SSM_WIDTH), F32),
            jax.ShapeDtypeStruct((n, MEM_WIDTH), BF16),
        ],
        scratch_shapes=[pltpu.VMEM((tm, D_MODEL), BF16)],
        compiler_params=_params(("parallel", "arbitrary")),
        name="mix_proj",
    )(h, norm_w, w_in)


def _swa_body(sinks_ref, q_ref, kvc_ref, kvp_ref, o_ref):
    blk = pl.program_id(1)
    q = q_ref[...]
    kvc = kvc_ref[...]
    kvp = kvp_ref[...]
    qi = lax.broadcasted_iota(jnp.int32, (WINDOW, 2 * WINDOW), 0)
    kj = lax.broadcasted_iota(jnp.int32, (WINDOW, 2 * WINDOW), 1)
    first_key = jnp.where(blk > 0, 0, WINDOW)
    valid = (kj > qi) & (kj <= qi + WINDOW) & (kj >= first_key)
    scale = HEAD_DIM ** -0.5
    for g in range(N_KV_HEADS):
        ks = slice(g * HEAD_DIM, (g + 1) * HEAD_DIM)
        vs = slice(KV_WIDTH + g * HEAD_DIM, KV_WIDTH + (g + 1) * HEAD_DIM)
        k = jnp.concatenate([kvp[:, ks], kvc[:, ks]], axis=0)
        v = jnp.concatenate([kvp[:, vs], kvc[:, vs]], axis=0)
        for r in range(GQA_REP):
            h = g * GQA_REP + r
            qh = q[:, h * HEAD_DIM:(h + 1) * HEAD_DIM]
            s = lax.dot_general(qh, k, (((1,), (1,)), ((), ())), preferred_element_type=F32) * scale
            s = jnp.where(valid, s, NEG_INF)
            sink = sinks_ref[h]
            m = jnp.maximum(jnp.max(s, axis=-1, keepdims=True), sink)
            p = jnp.exp(s - m)
            denom = jnp.sum(p, axis=-1, keepdims=True) + jnp.exp(sink - m)
            o = jnp.dot(p.astype(BF16), v, preferred_element_type=F32) / denom
            o_ref[:, h * HEAD_DIM:(h + 1) * HEAD_DIM] = o.astype(BF16)


def _swa(qkv, sinks, batch, seq):
    nb = seq // WINDOW
    kv_col = Q_WIDTH // (2 * KV_WIDTH)
    return pl.pallas_call(
        _swa_body,
        grid=(batch, nb),
        in_specs=[
            pl.BlockSpec(memory_space=pltpu.SMEM),
            pl.BlockSpec((WINDOW, Q_WIDTH), lambda b, n: (b * nb + n, 0)),
            pl.BlockSpec((WINDOW, 2 * KV_WIDTH), lambda b, n: (b * nb + n, kv_col)),
            pl.BlockSpec((WINDOW, 2 * KV_WIDTH), lambda b, n: (b * nb + jnp.maximum(n - 1, 0), kv_col)),
        ],
        out_specs=pl.BlockSpec((WINDOW, Q_WIDTH), lambda b, n: (b * nb + n, 0)),
        out_shape=jax.ShapeDtypeStruct((batch * seq, Q_WIDTH), BF16),
        compiler_params=_params(("parallel", "arbitrary")),
        name="swa",
    )(sinks, qkv, qkv, qkv)


def _mem_attn_body(q_ref, kv_ref, o_ref):
    scale = MEM_HEAD_DIM ** -0.5
    for h in range(MEM_HEADS):
        cs = slice(h * MEM_HEAD_DIM, (h + 1) * MEM_HEAD_DIM)
        vs = slice(MEM_WIDTH + h * MEM_HEAD_DIM, MEM_WIDTH + (h + 1) * MEM_HEAD_DIM)
        s = lax.dot_general(q_ref[:, cs], kv_ref[:, cs], (((1,), (1,)), ((), ())),
                            preferred_element_type=F32) * scale
        m = jnp.max(s, axis=-1, keepdims=True)
        p = jnp.exp(s - m)
        denom = jnp.sum(p, axis=-1, keepdims=True)
        o = jnp.dot(p.astype(BF16), kv_ref[:, vs], preferred_element_type=F32) / denom
        o_ref[:, cs] = o.astype(BF16)


def _mem_attn(mq, mem_kv, batch, seq, *, tq=512):
    nq = seq // tq
    return pl.pallas_call(
        _mem_attn_body,
        grid=(batch, nq),
        in_specs=[
            pl.BlockSpec((tq, MEM_WIDTH), lambda b, i: (b * nq + i, 0)),
            pl.BlockSpec((N_MEM, 2 * MEM_WIDTH), lambda b, i: (b, 0)),
        ],
        out_specs=pl.BlockSpec((tq, MEM_WIDTH), lambda b, i: (b * nq + i, 0)),
        out_shape=jax.ShapeDtypeStruct((batch * seq, MEM_WIDTH), BF16),
        compiler_params=_params(("parallel", "arbitrary")),
        name="mem_attn",
    )(mq, mem_kv)


def _ssm_operators(lam_re, lam_im, log_dt, b_re, b_im, c_re, c_im, d_skip):
    hp = lax.Precision.HIGHEST
    t_len, g_n, p_n, ch = SSM_CHUNK, SSM_GROUPS, SSM_STATE, SSM_GROUP
    lr = jnp.minimum(lam_re, -1e-4)
    li = lam_im
    dt = jnp.exp(log_dt)[:, None]
    mag = jnp.exp(lr * dt)
    ar = mag * jnp.cos(li * dt)
    ai = mag * jnp.sin(li * dt)
    nr, ni = ar - 1.0, ai
    den = lr * lr + li * li
    kr = (nr * lr + ni * li) / den
    ki = (ni * lr - nr * li) / den
    bbr = kr[..., None] * b_re - ki[..., None] * b_im
    bbi = kr[..., None] * b_im + ki[..., None] * b_re
    steps = jnp.arange(t_len + 1, dtype=F32)[None, :, None]
    pmag = jnp.exp(steps * (lr * dt)[:, None, :])
    ang = steps * (li * dt)[:, None, :]
    pr = pmag * jnp.cos(ang)
    pi = pmag * jnp.sin(ang)
    wr = pr[:, :t_len, :, None] * bbr[:, None] - pi[:, :t_len, :, None] * bbi[:, None]
    wi = pr[:, :t_len, :, None] * bbi[:, None] + pi[:, :t_len, :, None] * bbr[:, None]
    lagk = (jnp.einsum('gcp,gkpd->gkcd', c_re, wr, precision=hp)
            - jnp.einsum('gcp,gkpd->gkcd', c_im, wi, precision=hp))
    nt, gt = SSM_LANE_TILES, SSM_TILE_GROUPS
    lag_c = lagk.reshape(nt, gt, t_len, ch, ch).transpose(0, 2, 1, 4, 3).reshape(nt, t_len * LANES, ch)

    def pair(x, y):
        return jnp.concatenate([x, y], axis=-1)

    def per_step(w):
        return w.reshape(nt, gt, t_len, 2 * p_n).transpose(0, 2, 1, 3)[:, :, :, None, :]

    def per_channel(w):
        return w.reshape(nt, gt, ch, 2 * p_n)[:, None]

    back = jnp.arange(t_len - 1, -1, -1, dtype=F32)[None, :, None]
    bmag = jnp.exp(back * (lr * dt)[:, None, :])
    bang = back * (li * dt)[:, None, :]
    qr, qi = bmag * jnp.cos(bang), bmag * jnp.sin(bang)
    bbr_t, bbi_t = bbr.transpose(0, 2, 1), bbi.transpose(0, 2, 1)
    inp = (per_step(pair(qr, qr)) * per_channel(pair(bbr_t, bbi_t))
           + per_step(pair(-qi, qi)) * per_channel(pair(bbi_t, bbr_t))
           ).reshape(nt, t_len * LANES, 2 * p_n)
    pr1, pi1 = pr[:, 1:], pi[:, 1:]
    outp = (per_step(pair(pr1, pi1)) * per_channel(pair(c_re, -c_re))
            - per_step(pair(pi1, pr1)) * per_channel(pair(c_im, c_im))
            ).reshape(nt, t_len * LANES, 2 * p_n)
    a1 = jnp.concatenate([pr[:, t_len], pr[:, t_len]], axis=-1).reshape(nt, gt, 2 * p_n)
    a2 = jnp.concatenate([-pi[:, t_len], pi[:, t_len]], axis=-1).reshape(nt, gt, 2 * p_n)
    return lag_c, inp, outp, a1, a2, d_skip.reshape(1, SSM_WIDTH)


def _ssm_body(s_ref, lag_ref, inp_ref, outp_ref, a1_ref, a2_ref, d_ref, y_ref,
              u_ref, panel_ref, inpx_ref, outpx_ref, z_ref, zs_ref, sp_ref, *, batch):
    t_len, gt = SSM_CHUNK, SSM_TILE_GROUPS
    n_chunks = u_ref.shape[0]
    per_seq = n_chunks // batch
    flat = t_len * LANES

    for t in range(t_len):
        u_ref[:, t * LANES:(t + 1) * LANES] = s_ref[pl.ds(t, n_chunks, stride=t_len), :].astype(BF16)

    row_group = (lax.broadcasted_iota(jnp.int32, (flat, 1), 0) >> 4) & (gt - 1)
    col_group = lax.broadcasted_iota(jnp.int32, (1, LANES), 1) >> 4
    spread = (lax.broadcasted_iota(jnp.int32, (SSM_GROUP, LANES), 1) & (SSM_GROUP - 1)
              == lax.broadcasted_iota(jnp.int32, (SSM_GROUP, LANES), 0)).astype(BF16)
    lag = jnp.dot(lag_ref[...].astype(BF16), spread, preferred_element_type=F32)
    lag = jnp.where(row_group == col_group, lag, 0.0).astype(BF16)

    for r in range(t_len):
        k_left, k_right = t_len - 2 - r, t_len - 1 - r
        left = lag[k_left * LANES:(k_left + 1) * LANES] if k_left >= 0 else jnp.zeros((LANES, LANES), BF16)
        panel_ref[r * LANES:(r + 1) * LANES, :LANES] = left
        panel_ref[r * LANES:(r + 1) * LANES, LANES:] = lag[k_right * LANES:(k_right + 1) * LANES]

    inp = inp_ref[...]
    outp = outp_ref[...]
    for g in range(gt):
        inpx_ref[:, g * LANES:(g + 1) * LANES] = jnp.where(row_group == g, inp, 0.0).astype(BF16)
        outpx_ref[:, g * LANES:(g + 1) * LANES] = jnp.where(row_group == g, outp, 0.0).astype(BF16)

    z = jnp.dot(u_ref[...], inpx_ref[...], preferred_element_type=F32)
    for g in range(gt):
        zg = z[:, g * LANES:(g + 1) * LANES]
        z_ref[pl.ds(g, n_chunks, stride=gt), :] = zg
        zs_ref[pl.ds(g, n_chunks, stride=gt), :] = pltpu.roll(zg, SSM_STATE, axis=1)

    a1 = a1_ref[...]
    a2 = a2_ref[...]

    def step(c, carry):
        new = []
        for b in range(batch):
            v0, v1 = carry[b]
            row = pl.multiple_of((b * per_seq + c) * gt, gt)
            sp_ref[pl.ds(row, gt), :] = v0
            z0 = z_ref[pl.ds(row, gt), :]
            z1 = zs_ref[pl.ds(row, gt), :]
            new.append((a1 * v0 + a2 * v1 + z0, a1 * v1 - a2 * v0 + z1))
        return tuple(new)

    zero = jnp.zeros((gt, LANES), F32)
    lax.fori_loop(0, per_seq, step, tuple((zero, zero) for _ in range(batch)), unroll=4)

    sp = jnp.concatenate([sp_ref[pl.ds(g, n_chunks, stride=gt), :] for g in range(gt)], axis=1).astype(BF16)
    d2 = jnp.concatenate([d_ref[...], d_ref[...]], axis=1)
    for q in range(t_len // 2):
        cols = slice(2 * q * LANES, (2 * q + 2) * LANES)
        k_len = (2 * q + 2) * LANES
        y = jnp.dot(u_ref[:, :k_len], panel_ref[flat - k_len:, :], preferred_element_type=F32)
        y = y + lax.dot_general(sp, outpx_ref[cols, :], (((1,), (1,)), ((), ())), preferred_element_type=F32)
        y = y + d2 * u_ref[:, cols].astype(F32)
        y_ref[pl.ds(2 * q, n_chunks, stride=t_len), :] = y[:, :LANES]
        y_ref[pl.ds(2 * q + 1, n_chunks, stride=t_len), :] = y[:, LANES:]


def _ssm(s_in, ops, layer, batch):
    lag_c, inp, outp, a1, a2, d_row = ops
    n = s_in.shape[0]
    n_chunks = n // SSM_CHUNK
    flat = SSM_CHUNK * LANES
    return pl.pallas_call(
        functools.partial(_ssm_body, batch=batch),
        grid=(SSM_LANE_TILES,),
        in_specs=[
            pl.BlockSpec((n, LANES), lambda j: (0, j)),
            pl.BlockSpec((None, None, flat, SSM_GROUP), lambda j: (layer, j, 0, 0)),
            pl.BlockSpec((None, None, flat, 2 * SSM_STATE), lambda j: (layer, j, 0, 0)),
            pl.BlockSpec((None, None, flat, 2 * SSM_STATE), lambda j: (layer, j, 0, 0)),
            pl.BlockSpec((None, None, SSM_TILE_GROUPS, 2 * SSM_STATE), lambda j: (layer, j, 0, 0)),
            pl.BlockSpec((None, None, SSM_TILE_GROUPS, 2 * SSM_STATE), lambda j: (layer, j, 0, 0)),
            pl.BlockSpec((None, 1, LANES), lambda j: (layer, 0, j)),
        ],
        out_specs=pl.BlockSpec((n, LANES), lambda j: (0, j)),
        out_shape=jax.ShapeDtypeStruct((n, SSM_WIDTH), F32),
        scratch_shapes=[
            pltpu.VMEM((n_chunks, flat), BF16),
            pltpu.VMEM((flat, 2 * LANES), BF16),
            pltpu.VMEM((flat, SSM_TILE_GROUPS * 2 * SSM_STATE), BF16),
            pltpu.VMEM((flat, SSM_TILE_GROUPS * 2 * SSM_STATE), BF16),
            pltpu.VMEM((n_chunks * SSM_TILE_GROUPS, 2 * SSM_STATE), F32),
            pltpu.VMEM((n_chunks * SSM_TILE_GROUPS, 2 * SSM_STATE), F32),
            pltpu.VMEM((n_chunks * SSM_TILE_GROUPS, 2 * SSM_STATE), F32),
        ],
        compiler_params=_params(("parallel",)),
        name="ssm",
    )(s_in, lag_c, inp, outp, a1, a2, d_row)


MERGE_ROW_CHUNK = 512


def _merge_body(h_hbm, nw_ref, swa_ref, ssm_ref, mem_ref, wg0_ref, wg1_ref, wg2_ref, wswa_ref,
                wga_ref, wgb_ref, wmem_ref, wo_ref, out_hbm, xn_ref, hbuf, o_ref, sems):
    j = pl.program_id(1)
    io = _RowTileIO(h_hbm, out_hbm, hbuf, o_ref, sems)

    @pl.when(j == 0)
    def _():
        io.begin(xn_ref, nw_ref)

    wg0 = wg0_ref[...].astype(BF16)
    wg1 = wg1_ref[...].astype(BF16)
    wg2 = wg2_ref[...].astype(BF16)
    for r in range(o_ref.shape[0] // MERGE_ROW_CHUNK):
        rows = pl.ds(r * MERGE_ROW_CHUNK, MERGE_ROW_CHUNK)
        xn = xn_ref[rows, :]
        ys = ssm_ref[rows, :].astype(BF16)
        y_swa = jnp.dot(swa_ref[rows, :], wswa_ref[...], preferred_element_type=F32)
        ga = jnp.dot(ys, wga_ref[...], preferred_element_type=F32)
        gb = jnp.dot(ys, wgb_ref[...], preferred_element_type=F32)
        y_ssm = ga * _sigmoid(gb)
        y_mem = jnp.dot(mem_ref[rows, :], wmem_ref[...], preferred_element_type=F32)
        merged = _sigmoid(jnp.dot(xn, wg0, preferred_element_type=F32)) * y_swa
        merged += _sigmoid(jnp.dot(xn, wg1, preferred_element_type=F32)) * y_ssm
        merged += _sigmoid(jnp.dot(xn, wg2, preferred_element_type=F32)) * y_mem
        o_ref[rows, :] += jnp.dot(merged.astype(BF16), wo_ref[...], preferred_element_type=F32)

    @pl.when(j == pl.num_programs(1) - 1)
    def _():
        io.end()


def _merge(h, norm_w, o_swa, y_s, o_mem, w_in, layer, w_swa_up, w_ssm_glu, w_mem_up, w_out, *, tm=1024, tn=256):
    n = h.shape[0]
    nj = D_MODEL // tn
    g0 = GATE_OFFSET // tn
    row = lambda i, j: (i, 0)
    once = pl.Buffered(1)
    return pl.pallas_call(
        _merge_body,
        grid=(n // tm, nj),
        in_specs=[
            pl.BlockSpec(memory_space=pl.ANY),
            pl.BlockSpec((1, D_MODEL), lambda i, j: (0, 0)),
            pl.BlockSpec((tm, Q_WIDTH), row, pipeline_mode=once),
            pl.BlockSpec((tm, SSM_WIDTH), row, pipeline_mode=once),
            pl.BlockSpec((tm, MEM_WIDTH), row, pipeline_mode=once),
            pl.BlockSpec((None, D_MODEL, tn), lambda i, j: (layer, 0, g0 + j)),
            pl.BlockSpec((None, D_MODEL, tn), lambda i, j: (layer, 0, g0 + j + nj)),
            pl.BlockSpec((None, D_MODEL, tn), lambda i, j: (layer, 0, g0 + j + 2 * nj)),
            pl.BlockSpec((Q_WIDTH, tn), lambda i, j: (0, j)),
            pl.BlockSpec((SSM_WIDTH, tn), lambda i, j: (0, j)),
            pl.BlockSpec((SSM_WIDTH, tn), lambda i, j: (0, j + nj)),
            pl.BlockSpec((MEM_WIDTH, tn), lambda i, j: (0, j)),
            pl.BlockSpec((tn, D_MODEL), lambda i, j: (j, 0)),
        ],
        out_specs=pl.BlockSpec(memory_space=pl.ANY),
        out_shape=jax.ShapeDtypeStruct((n, D_MODEL), F32),
        scratch_shapes=[pltpu.VMEM((tm, D_MODEL), BF16), pltpu.VMEM((tm, D_MODEL), F32),
                        pltpu.VMEM((tm, D_MODEL), F32), pltpu.SemaphoreType.DMA((2,))],
        compiler_params=_params(("arbitrary", "arbitrary")),
        name="merge",
    )(h, norm_w, o_swa, y_s, o_mem, w_in, w_in, w_in, w_swa_up, w_ssm_glu, w_ssm_glu, w_mem_up, w_out)


def kernel(x, mem, ffn1_norm, ffn1_w_in, ffn1_w_out, mix_norm, mem_norm, w_in, sinks, w_mem_kv, lam_re, lam_im, log_dt, b_re, b_im, c_re, c_im, d_skip, w_ssm_glu, w_swa_up, w_mem_up, w_out, ffn2_norm, ffn2_w_in, ffn2_w_out, final_norm):
    batch, seq = x.shape[0], x.shape[1]
    n = batch * seq
    h = x.reshape(n, D_MODEL)
    mem2 = mem.reshape(batch * N_MEM, D_MODEL)
    final_w = final_norm.reshape(1, D_MODEL)
    ssm_ops = jax.vmap(_ssm_operators)(lam_re, lam_im, log_dt, b_re, b_im, c_re, c_im, d_skip)
    for l in range(DEPTH):
        mix_w = mix_norm[l].reshape(1, D_MODEL)

        h = _ffn(h, ffn1_norm[l].reshape(1, D_MODEL), ffn1_w_in, ffn1_w_out, final_w, l, apply_final_norm=False)
        qkv, s_in, mq = _mix_proj(h, mix_w, w_in, l)
        mem_kv = _norm_proj(mem2, mem_norm[l].reshape(1, D_MODEL), w_mem_kv, l, 0, 2 * MEM_WIDTH, BF16,
                            tm=batch * N_MEM)
        o_swa = _swa(qkv, sinks[l], batch, seq)
        o_mem = _mem_attn(mq, mem_kv, batch, seq)
        y_s = _ssm(s_in, ssm_ops, l, batch)
        h = _merge(h, mix_w, o_swa, y_s, o_mem, w_in, l, w_swa_up[l].astype(BF16),
                   w_ssm_glu[l].astype(BF16), w_mem_up[l].astype(BF16), w_out[l].astype(BF16))
        h = _ffn(h, ffn2_norm[l].reshape(1, D_MODEL), ffn2_w_in, ffn2_w_out, final_w, l,
                 apply_final_norm=(l == DEPTH - 1))
    return h.reshape(batch, seq, D_MODEL)
```

```python
import functools
import math

import jax
import jax.numpy as jnp
from jax import lax
from jax.experimental import pallas as pl
from jax.experimental.pallas import tpu as pltpu

D_MODEL = 2048
DEPTH = 4
N_MEM = 256
D_FF = 5632
RMS_EPS = 1e-5

WINDOW = 128
HEAD_DIM = 64
N_Q_HEADS = 16
N_KV_HEADS = 4
GQA_REP = N_Q_HEADS // N_KV_HEADS
Q_WIDTH = N_Q_HEADS * HEAD_DIM
KV_WIDTH = N_KV_HEADS * HEAD_DIM

SSM_WIDTH = 1024
SSM_GROUP = 16
SSM_GROUPS = SSM_WIDTH // SSM_GROUP
SSM_STATE = 64
SSM_CHUNK = 16
LANES = 128
SSM_LANE_TILES = SSM_WIDTH // LANES
SSM_TILE_GROUPS = LANES // SSM_GROUP

MEM_HEADS = 4
MEM_HEAD_DIM = 256
MEM_WIDTH = MEM_HEADS * MEM_HEAD_DIM

N_BRANCHES = 3
NEG_INF = -1e30

QKV_WIDTH = Q_WIDTH + 2 * KV_WIDTH
SSM_OFFSET = QKV_WIDTH
MEMQ_OFFSET = SSM_OFFSET + SSM_WIDTH
GATE_OFFSET = MEMQ_OFFSET + MEM_WIDTH

VMEM_LIMIT_BYTES = 56 * 1024 * 1024

BF16 = jnp.bfloat16
F32 = jnp.float32


def _params(semantics):
    return pltpu.CompilerParams(dimension_semantics=semantics, vmem_limit_bytes=VMEM_LIMIT_BYTES)


def _rms_normalize(x, w):
    ms = jnp.mean(x * x, axis=-1, keepdims=True)
    return (x * lax.rsqrt(ms + RMS_EPS)) * w


def _sigmoid(x):
    return 0.5 * jnp.tanh(0.5 * x) + 0.5


FFN_ROW_CHUNK = 512


class _RowTileIO:
    def __init__(self, h_hbm, out_hbm, hbuf, acc_ref, sems):
        self.h_hbm, self.out_hbm, self.hbuf, self.acc_ref, self.sems = h_hbm, out_hbm, hbuf, acc_ref, sems
        self.tm = acc_ref.shape[0]

    def _fetch(self, tile):
        return pltpu.make_async_copy(self.h_hbm.at[pl.ds(tile * self.tm, self.tm), :], self.hbuf, self.sems.at[0])

    def _write_back(self, tile):
        return pltpu.make_async_copy(self.acc_ref, self.out_hbm.at[pl.ds(tile * self.tm, self.tm), :],
                                     self.sems.at[1])

    def begin(self, xn_ref, nw_ref):
        i = pl.program_id(0)

        @pl.when(i == 0)
        def _():
            self._fetch(0).start()

        self._fetch(i).wait()
        xn_ref[...] = _rms_normalize(self.hbuf[...], nw_ref[...]).astype(BF16)

        @pl.when(i > 0)
        def _():
            self._write_back(i - 1).wait()

        self.acc_ref[...] = self.hbuf[...]

        @pl.when(i + 1 < pl.num_programs(0))
        def _():
            self._fetch(i + 1).start()

    def end(self):
        i = pl.program_id(0)
        self._write_back(i).start()

        @pl.when(i == pl.num_programs(0) - 1)
        def _():
            self._write_back(i).wait()


def _ffn_body(h_hbm, nw_ref, wg_ref, wu_ref, wo_ref, fw_ref, out_hbm, xn_ref, hbuf, acc_ref, sems, *,
              apply_final_norm):
    j = pl.program_id(1)
    io = _RowTileIO(h_hbm, out_hbm, hbuf, acc_ref, sems)

    @pl.when(j == 0)
    def _():
        io.begin(xn_ref, nw_ref)

    wg = wg_ref[...].astype(BF16)
    wu = wu_ref[...].astype(BF16)
    wo = wo_ref[...].astype(BF16)
    for r in range(acc_ref.shape[0] // FFN_ROW_CHUNK):
        rows = pl.ds(r * FFN_ROW_CHUNK, FFN_ROW_CHUNK)
        xn = xn_ref[rows, :]
        g = jnp.dot(xn, wg, preferred_element_type=F32)
        u = jnp.dot(xn, wu, preferred_element_type=F32)
        a = ((0.5 * g) * _sigmoid(g)) * u
        acc_ref[rows, :] += jnp.dot(a.astype(BF16), wo, preferred_element_type=F32)

    @pl.when(j == pl.num_programs(1) - 1)
    def _():
        if apply_final_norm:
            acc_ref[...] = _rms_normalize(acc_ref[...], fw_ref[...])
        io.end()


def _ffn(h, norm_w, w_in, w_out, final_w, layer, *, apply_final_norm, tm=1024, tf=512):
    n = h.shape[0]
    nf = D_FF // tf
    return pl.pallas_call(
        functools.partial(_ffn_body, apply_final_norm=apply_final_norm),
        grid=(n // tm, nf),
        in_specs=[
            pl.BlockSpec(memory_space=pl.ANY),
            pl.BlockSpec((1, D_MODEL), lambda i, j: (0, 0)),
            pl.BlockSpec((None, D_MODEL, tf), lambda i, j: (layer, 0, j)),
            pl.BlockSpec((None, D_MODEL, tf), lambda i, j: (layer, 0, j + nf)),
            pl.BlockSpec((None, tf, D_MODEL), lambda i, j: (layer, j, 0)),
            pl.BlockSpec((1, D_MODEL), lambda i, j: (0, 0)),
        ],
        out_specs=pl.BlockSpec(memory_space=pl.ANY),
        out_shape=jax.ShapeDtypeStruct((n, D_MODEL), F32),
        scratch_shapes=[pltpu.VMEM((tm, D_MODEL), BF16), pltpu.VMEM((tm, D_MODEL), F32),
                        pltpu.VMEM((tm, D_MODEL), F32), pltpu.SemaphoreType.DMA((2,))],
        compiler_params=_params(("arbitrary", "arbitrary")),
        name="ffn",
    )(h, norm_w, w_in, w_in, w_out, final_w)


def _proj_body(h_ref, nw_ref, w_ref, o_ref, xn_ref):
    @pl.when(pl.program_id(1) == 0)
    def _():
        xn_ref[...] = _rms_normalize(h_ref[...], nw_ref[...]).astype(BF16)

    o_ref[...] = jnp.dot(xn_ref[...], w_ref[...].astype(BF16), preferred_element_type=F32).astype(o_ref.dtype)


def _norm_proj(h, norm_w, w, layer, col0, width, out_dtype, *, tm, tn=512):
    n = h.shape[0]
    c0 = col0 // tn
    return pl.pallas_call(
        _proj_body,
        grid=(n // tm, width // tn),
        in_specs=[
            pl.BlockSpec((tm, D_MODEL), lambda i, j: (i, 0)),
            pl.BlockSpec((1, D_MODEL), lambda i, j: (0, 0)),
            pl.BlockSpec((None, D_MODEL, tn), lambda i, j: (layer, 0, c0 + j)),
        ],
        out_specs=pl.BlockSpec((tm, tn), lambda i, j: (i, j)),
        out_shape=jax.ShapeDtypeStruct((n, width), out_dtype),
        scratch_shapes=[pltpu.VMEM((tm, D_MODEL), BF16)],
        compiler_params=_params(("parallel", "arbitrary")),
        name="norm_proj",
    )(h, norm_w, w)


def _mix_proj_body(h_ref, nw_ref, w_ref, qkv_ref, s_ref, mq_ref, xn_ref, *, n_qkv, n_ssm):
    j = pl.program_id(1)

    @pl.when(j == 0)
    def _():
        xn_ref[...] = _rms_normalize(h_ref[...], nw_ref[...]).astype(BF16)

    y = jnp.dot(xn_ref[...], w_ref[...].astype(BF16), preferred_element_type=F32)

    @pl.when(j < n_qkv)
    def _():
        qkv_ref[...] = y.astype(BF16)

    @pl.when((j >= n_qkv) & (j < n_qkv + n_ssm))
    def _():
        s_ref[...] = y

    @pl.when(j >= n_qkv + n_ssm)
    def _():
        mq_ref[...] = y.astype(BF16)


def _mix_proj(h, norm_w, w_in, layer, *, tm=2048, tn=512):
    n = h.shape[0]
    n_qkv, n_ssm, n_mq = QKV_WIDTH // tn, SSM_WIDTH // tn, MEM_WIDTH // tn
    return pl.pallas_call(
        functools.partial(_mix_proj_body, n_qkv=n_qkv, n_ssm=n_ssm),
        grid=(n // tm, n_qkv + n_ssm + n_mq),
        in_specs=[
            pl.BlockSpec((tm, D_MODEL), lambda i, j: (i, 0), pipeline_mode=pl.Buffered(1)),
            pl.BlockSpec((1, D_MODEL), lambda i, j: (0, 0)),
            pl.BlockSpec((None, D_MODEL, tn), lambda i, j: (layer, 0, j)),
        ],
        out_specs=[
            pl.BlockSpec((tm, tn), lambda i, j: (i, jnp.minimum(j, n_qkv - 1))),
            pl.BlockSpec((tm, tn), lambda i, j: (i, jnp.clip(j - n_qkv, 0, n_ssm - 1))),
            pl.BlockSpec((tm, tn), lambda i, j: (i, jnp.clip(j - n_qkv - n_ssm, 0, n_mq - 1))),
        ],
        out_shape=[
            jax.ShapeDtypeStruct((n, QKV_WIDTH), BF16),
            jax.ShapeDtypeStruct((n, SSM_WIDTH), F32),
            jax.ShapeDtypeStruct((n, MEM_WIDTH), BF16),
        ],
        scratch_shapes=[pltpu.VMEM((tm, D_MODEL), BF16)],
        compiler_params=_params(("parallel", "arbitrary")),
        name="mix_proj",
    )(h, norm_w, w_in)


def _swa_body(sinks_ref, q_ref, kvc_ref, kvp_ref, o_ref):
    blk = pl.program_id(1)
    q = q_ref[...]
    kvc = kvc_ref[...]
    kvp = kvp_ref[...]
    qi = lax.broadcasted_iota(jnp.int32, (WINDOW, 2 * WINDOW), 0)
    kj = lax.broadcasted_iota(jnp.int32, (WINDOW, 2 * WINDOW), 1)
    first_key = jnp.where(blk > 0, 0, WINDOW)
    valid = (kj > qi) & (kj <= qi + WINDOW) & (kj >= first_key)
    scale = HEAD_DIM ** -0.5
    for g in range(N_KV_HEADS):
        ks = slice(g * HEAD_DIM, (g + 1) * HEAD_DIM)
        vs = slice(KV_WIDTH + g * HEAD_DIM, KV_WIDTH + (g + 1) * HEAD_DIM)
        k = jnp.concatenate([kvp[:, ks], kvc[:, ks]], axis=0)
        v = jnp.concatenate([kvp[:, vs], kvc[:, vs]], axis=0)
        for r in range(GQA_REP):
            h = g * GQA_REP + r
            qh = q[:, h * HEAD_DIM:(h + 1) * HEAD_DIM]
            s = lax.dot_general(qh, k, (((1,), (1,)), ((), ())), preferred_element_type=F32) * scale
            s = jnp.where(valid, s, NEG_INF)
            sink = sinks_ref[h]
            m = jnp.maximum(jnp.max(s, axis=-1, keepdims=True), sink)
            p = jnp.exp(s - m)
            denom = jnp.sum(p, axis=-1, keepdims=True) + jnp.exp(sink - m)
            o = jnp.dot(p.astype(BF16), v, preferred_element_type=F32) / denom
            o_ref[:, h * HEAD_DIM:(h + 1) * HEAD_DIM] = o.astype(BF16)


def _swa(qkv, sinks, batch, seq):
    nb = seq // WINDOW
    kv_col = Q_WIDTH // (2 * KV_WIDTH)
    return pl.pallas_call(
        _swa_body,
        grid=(batch, nb),
        in_specs=[
            pl.BlockSpec(memory_space=pltpu.SMEM),
            pl.BlockSpec((WINDOW, Q_WIDTH), lambda b, n: (b * nb + n, 0)),
            pl.BlockSpec((WINDOW, 2 * KV_WIDTH), lambda b, n: (b * nb + n, kv_col)),
            pl.BlockSpec((WINDOW, 2 * KV_WIDTH), lambda b, n: (b * nb + jnp.maximum(n - 1, 0), kv_col)),
        ],
        out_specs=pl.BlockSpec((WINDOW, Q_WIDTH), lambda b, n: (b * nb + n, 0)),
        out_shape=jax.ShapeDtypeStruct((batch * seq, Q_WIDTH), BF16),
        compiler_params=_params(("parallel", "arbitrary")),
        name="swa",
    )(sinks, qkv, qkv, qkv)


def _mem_attn_body(q_ref, kv_ref, o_ref):
    scale = MEM_HEAD_DIM ** -0.5
    for h in range(MEM_HEADS):
        cs = slice(h * MEM_HEAD_DIM, (h + 1) * MEM_HEAD_DIM)
        vs = slice(MEM_WIDTH + h * MEM_HEAD_DIM, MEM_WIDTH + (h + 1) * MEM_HEAD_DIM)
        s = lax.dot_general(q_ref[:, cs], kv_ref[:, cs], (((1,), (1,)), ((), ())),
                            preferred_element_type=F32) * scale
        m = jnp.max(s, axis=-1, keepdims=True)
        p = jnp.exp(s - m)
        denom = jnp.sum(p, axis=-1, keepdims=True)
        o = jnp.dot(p.astype(BF16), kv_ref[:, vs], preferred_element_type=F32) / denom
        o_ref[:, cs] = o.astype(BF16)


def _mem_attn(mq, mem_kv, batch, seq, *, tq=512):
    nq = seq // tq
    return pl.pallas_call(
        _mem_attn_body,
        grid=(batch, nq),
        in_specs=[
            pl.BlockSpec((tq, MEM_WIDTH), lambda b, i: (b * nq + i, 0)),
            pl.BlockSpec((N_MEM, 2 * MEM_WIDTH), lambda b, i: (b, 0)),
        ],
        out_specs=pl.BlockSpec((tq, MEM_WIDTH), lambda b, i: (b * nq + i, 0)),
        out_shape=jax.ShapeDtypeStruct((batch * seq, MEM_WIDTH), BF16),
        compiler_params=_params(("parallel", "arbitrary")),
        name="mem_attn",
    )(mq, mem_kv)


def _ssm_operators(lam_re, lam_im, log_dt, b_re, b_im, c_re, c_im, d_skip):
    hp = lax.Precision.HIGHEST
    t_len, g_n, p_n, ch = SSM_CHUNK, SSM_GROUPS, SSM_STATE, SSM_GROUP
    lr = jnp.minimum(lam_re, -1e-4)
    li = lam_im
    dt = jnp.exp(log_dt)[:, None]
    mag = jnp.exp(lr * dt)
    ar = mag * jnp.cos(li * dt)
    ai = mag * jnp.sin(li * dt)
    nr, ni = ar - 1.0, ai
    den = lr * lr + li * li
    kr = (nr * lr + ni * li) / den
    ki = (ni * lr - nr * li) / den
    bbr = kr[..., None] * b_re - ki[..., None] * b_im
    bbi = kr[..., None] * b_im + ki[..., None] * b_re
    steps = jnp.arange(t_len + 1, dtype=F32)[None, :, None]
    pmag = jnp.exp(steps * (lr * dt)[:, None, :])
    ang = steps * (li * dt)[:, None, :]
    pr = pmag * jnp.cos(ang)
    pi = pmag * jnp.sin(ang)
    wr = pr[:, :t_len, :, None] * bbr[:, None] - pi[:, :t_len, :, None] * bbi[:, None]
    wi = pr[:, :t_len, :, None] * bbi[:, None] + pi[:, :t_len, :, None] * bbr[:, None]
    lagk = (jnp.einsum('gcp,gkpd->gkcd', c_re, wr, precision=hp)
            - jnp.einsum('gcp,gkpd->gkcd', c_im, wi, precision=hp))
    nt, gt = SSM_LANE_TILES, SSM_TILE_GROUPS
    lag_c = lagk.reshape(nt, gt, t_len, ch, ch).transpose(0, 2, 1, 4, 3).reshape(nt, t_len * LANES, ch)

    def pair(x, y):
        return jnp.concatenate([x, y], axis=-1)

    def per_step(w):
        return w.reshape(nt, gt, t_len, 2 * p_n).transpose(0, 2, 1, 3)[:, :, :, None, :]

    def per_channel(w):
        return w.reshape(nt, gt, ch, 2 * p_n)[:, None]

    back = jnp.arange(t_len - 1, -1, -1, dtype=F32)[None, :, None]
    bmag = jnp.exp(back * (lr * dt)[:, None, :])
    bang = back * (li * dt)[:, None, :]
    qr, qi = bmag * jnp.cos(bang), bmag * jnp.sin(bang)
    bbr_t, bbi_t = bbr.transpose(0, 2, 1), bbi.transpose(0, 2, 1)
    inp = (per_step(pair(qr, qr)) * per_channel(pair(bbr_t, bbi_t))
           + per_step(pair(-qi, qi)) * per_channel(pair(bbi_t, bbr_t))
           ).reshape(nt, t_len * LANES, 2 * p_n)
    pr1, pi1 = pr[:, 1:], pi[:, 1:]
    outp = (per_step(pair(pr1, pi1)) * per_channel(pair(c_re, -c_re))
            - per_step(pair(pi1, pr1)) * per_channel(pair(c_im, c_im))
            ).reshape(nt, t_len * LANES, 2 * p_n)
    a1 = jnp.concatenate([pr[:, t_len], pr[:, t_len]], axis=-1).reshape(nt, gt, 2 * p_n)
    a2 = jnp.concatenate([-pi[:, t_len], pi[:, t_len]], axis=-1).reshape(nt, gt, 2 * p_n)
    return lag_c, inp, outp, a1, a2, d_skip.reshape(1, SSM_WIDTH)


def _ssm_body(s_ref, lag_ref, inp_ref, outp_ref, a1_ref, a2_ref, d_ref, y_ref,
              u_ref, panel_ref, inpx_ref, outpx_ref, z_ref, zs_ref, sp_ref, *, batch):
    t_len, gt = SSM_CHUNK, SSM_TILE_GROUPS
    n_chunks = u_ref.shape[0]
    per_seq = n_chunks // batch
    flat = t_len * LANES

    for t in range(t_len):
        u_ref[:, t * LANES:(t + 1) * LANES] = s_ref[pl.ds(t, n_chunks, stride=t_len), :].astype(BF16)

    row_group = (lax.broadcasted_iota(jnp.int32, (flat, 1), 0) >> 4) & (gt - 1)
    col_group = lax.broadcasted_iota(jnp.int32, (1, LANES), 1) >> 4
    spread = (lax.broadcasted_iota(jnp.int32, (SSM_GROUP, LANES), 1) & (SSM_GROUP - 1)
              == lax.broadcasted_iota(jnp.int32, (SSM_GROUP, LANES), 0)).astype(BF16)
    lag = jnp.dot(lag_ref[...].astype(BF16), spread, preferred_element_type=F32)
    lag = jnp.where(row_group == col_group, lag, 0.0).astype(BF16)

    for r in range(t_len):
        k_left, k_right = t_len - 2 - r, t_len - 1 - r
        left = lag[k_left * LANES:(k_left + 1) * LANES] if k_left >= 0 else jnp.zeros((LANES, LANES), BF16)
        panel_ref[r * LANES:(r + 1) * LANES, :LANES] = left
        panel_ref[r * LANES:(r + 1) * LANES, LANES:] = lag[k_right * LANES:(k_right + 1) * LANES]

    inp = inp_ref[...]
    outp = outp_ref[...]
    for g in range(gt):
        inpx_ref[:, g * LANES:(g + 1) * LANES] = jnp.where(row_group == g, inp, 0.0).astype(BF16)
        outpx_ref[:, g * LANES:(g + 1) * LANES] = jnp.where(row_group == g, outp, 0.0).astype(BF16)

    z = jnp.dot(u_ref[...], inpx_ref[...], preferred_element_type=F32)
    for g in range(gt):
        zg = z[:, g * LANES:(g + 1) * LANES]
        z_ref[pl.ds(g, n_chunks, stride=gt), :] = zg
        zs_ref[pl.ds(g, n_chunks, stride=gt), :] = pltpu.roll(zg, SSM_STATE, axis=1)

    a1 = a1_ref[...]
    a2 = a2_ref[...]

    def step(c, carry):
        new = []
        for b in range(batch):
            v0, v1 = carry[b]
            row = pl.multiple_of((b * per_seq + c) * gt, gt)
            sp_ref[pl.ds(row, gt), :] = v0
            z0 = z_ref[pl.ds(row, gt), :]
            z1 = zs_ref[pl.ds(row, gt), :]
            new.append((a1 * v0 + a2 * v1 + z0, a1 * v1 - a2 * v0 + z1))
        return tuple(new)

    zero = jnp.zeros((gt, LANES), F32)
    lax.fori_loop(0, per_seq, step, tuple((zero, zero) for _ in range(batch)), unroll=4)

    sp = jnp.concatenate([sp_ref[pl.ds(g, n_chunks, stride=gt), :] for g in range(gt)], axis=1).astype(BF16)
    d2 = jnp.concatenate([d_ref[...], d_ref[...]], axis=1)
    for q in range(t_len // 2):
        cols = slice(2 * q * LANES, (2 * q + 2) * LANES)
        k_len = (2 * q + 2) * LANES
        y = jnp.dot(u_ref[:, :k_len], panel_ref[flat - k_len:, :], preferred_element_type=F32)
        y = y + lax.dot_general(sp, outpx_ref[cols, :], (((1,), (1,)), ((), ())), preferred_element_type=F32)
        y = y + d2 * u_ref[:, cols].astype(F32)
        y_ref[pl.ds(2 * q, n_chunks, stride=t_len), :] = y[:, :LANES]
        y_ref[pl.ds(2 * q + 1, n_chunks, stride=t_len), :] = y[:, LANES:]


def _ssm(s_in, ops, layer, batch):
    lag_c, inp, outp, a1, a2, d_row = ops
    n = s_in.shape[0]
    n_chunks = n // SSM_CHUNK
    flat = SSM_CHUNK * LANES
    return pl.pallas_call(
        functools.partial(_ssm_body, batch=batch),
        grid=(SSM_LANE_TILES,),
        in_specs=[
            pl.BlockSpec((n, LANES), lambda j: (0, j)),
            pl.BlockSpec((None, None, flat, SSM_GROUP), lambda j: (layer, j, 0, 0)),
            pl.BlockSpec((None, None, flat, 2 * SSM_STATE), lambda j: (layer, j, 0, 0)),
            pl.BlockSpec((None, None, flat, 2 * SSM_STATE), lambda j: (layer, j, 0, 0)),
            pl.BlockSpec((None, None, SSM_TILE_GROUPS, 2 * SSM_STATE), lambda j: (layer, j, 0, 0)),
            pl.BlockSpec((None, None, SSM_TILE_GROUPS, 2 * SSM_STATE), lambda j: (layer, j, 0, 0)),
            pl.BlockSpec((None, 1, LANES), lambda j: (layer, 0, j)),
        ],
        out_specs=pl.BlockSpec((n, LANES), lambda j: (0, j)),
        out_shape=jax.ShapeDtypeStruct((n, SSM_WIDTH), F32),
        scratch_shapes=[
            pltpu.VMEM((n_chunks, flat), BF16),
            pltpu.VMEM((flat, 2 * LANES), BF16),
            pltpu.VMEM((flat, SSM_TILE_GROUPS * 2 * SSM_STATE), BF16),
            pltpu.VMEM((flat, SSM_TILE_GROUPS * 2 * SSM_STATE), BF16),
            pltpu.VMEM((n_chunks * SSM_TILE_GROUPS, 2 * SSM_STATE), F32),
            pltpu.VMEM((n_chunks * SSM_TILE_GROUPS, 2 * SSM_STATE), F32),
            pltpu.VMEM((n_chunks * SSM_TILE_GROUPS, 2 * SSM_STATE), F32),
        ],
        compiler_params=_params(("parallel",)),
        name="ssm",
    )(s_in, lag_c, inp, outp, a1, a2, d_row)


MERGE_ROW_CHUNK = 256


def _merge_body(h_hbm, nw_ref, swa_ref, ssm_ref, mem_ref, wg0_ref, wg1_ref, wg2_ref, wswa_ref,
                wga_ref, wgb_ref, wmem_ref, wo_ref, out_hbm, xn_ref, hbuf, o_ref, sems):
    j = pl.program_id(1)
    io = _RowTileIO(h_hbm, out_hbm, hbuf, o_ref, sems)

    @pl.when(j == 0)
    def _():
        io.begin(xn_ref, nw_ref)

    tn = wo_ref.shape[0]
    half_k = D_MODEL // 2
    wg01 = jnp.concatenate([wg0_ref[...].astype(BF16), wg1_ref[...].astype(BF16)], axis=1)
    wg2 = wg2_ref[...].astype(BF16)
    wglu = jnp.concatenate([wga_ref[...], wgb_ref[...]], axis=1)
    for r in range(o_ref.shape[0] // MERGE_ROW_CHUNK):
        rows = pl.ds(r * MERGE_ROW_CHUNK, MERGE_ROW_CHUNK)
        xn = xn_ref[rows, :]
        ys = ssm_ref[rows, :].astype(BF16)
        y_swa = jnp.dot(swa_ref[rows, :], wswa_ref[...], preferred_element_type=F32)
        y_mem = jnp.dot(mem_ref[rows, :], wmem_ref[...], preferred_element_type=F32)
        glu = jnp.dot(ys, wglu, preferred_element_type=F32)
        y_ssm = glu[:, :tn] * _sigmoid(glu[:, tn:])
        g01 = jnp.dot(xn, wg01, preferred_element_type=F32)
        g2 = (jnp.dot(xn[:, :half_k], wg2[:half_k], preferred_element_type=F32)
              + jnp.dot(xn[:, half_k:], wg2[half_k:], preferred_element_type=F32))
        merged = (_sigmoid(g01[:, :tn]) * y_swa + _sigmoid(g01[:, tn:]) * y_ssm + _sigmoid(g2) * y_mem).astype(BF16)
        for c in range(2):
            cols = pl.ds(c * half_k, half_k)
            o_ref[rows, cols] += jnp.dot(merged, wo_ref[:, cols], preferred_element_type=F32)

    @pl.when(j == pl.num_programs(1) - 1)
    def _():
        io.end()


def _merge(h, norm_w, o_swa, y_s, o_mem, w_in, layer, w_swa_up, w_ssm_glu, w_mem_up, w_out, *, tm=1024, tn=256):
    n = h.shape[0]
    nj = D_MODEL // tn
    g0 = GATE_OFFSET // tn
    row = lambda i, j: (i, 0)
    once = pl.Buffered(1)
    return pl.pallas_call(
        _merge_body,
        grid=(n // tm, nj),
        in_specs=[
            pl.BlockSpec(memory_space=pl.ANY),
            pl.BlockSpec((1, D_MODEL), lambda i, j: (0, 0)),
            pl.BlockSpec((tm, Q_WIDTH), row, pipeline_mode=once),
            pl.BlockSpec((tm, SSM_WIDTH), row, pipeline_mode=once),
            pl.BlockSpec((tm, MEM_WIDTH), row, pipeline_mode=once),
            pl.BlockSpec((None, D_MODEL, tn), lambda i, j: (layer, 0, g0 + j)),
            pl.BlockSpec((None, D_MODEL, tn), lambda i, j: (layer, 0, g0 + j + nj)),
            pl.BlockSpec((None, D_MODEL, tn), lambda i, j: (layer, 0, g0 + j + 2 * nj)),
            pl.BlockSpec((Q_WIDTH, tn), lambda i, j: (0, j)),
            pl.BlockSpec((SSM_WIDTH, tn), lambda i, j: (0, j)),
            pl.BlockSpec((SSM_WIDTH, tn), lambda i, j: (0, j + nj)),
            pl.BlockSpec((MEM_WIDTH, tn), lambda i, j: (0, j)),
            pl.BlockSpec((tn, D_MODEL), lambda i, j: (j, 0)),
        ],
        out_specs=pl.BlockSpec(memory_space=pl.ANY),
        out_shape=jax.ShapeDtypeStruct((n, D_MODEL), F32),
        scratch_shapes=[pltpu.VMEM((tm, D_MODEL), BF16), pltpu.VMEM((tm, D_MODEL), F32),
                        pltpu.VMEM((tm, D_MODEL), F32), pltpu.SemaphoreType.DMA((2,))],
        compiler_params=_params(("arbitrary", "arbitrary")),
        name="merge",
    )(h, norm_w, o_swa, y_s, o_mem, w_in, w_in, w_in, w_swa_up, w_ssm_glu, w_ssm_glu, w_mem_up, w_out)


def kernel(x, mem, ffn1_norm, ffn1_w_in, ffn1_w_out, mix_norm, mem_norm, w_in, sinks, w_mem_kv, lam_re, lam_im, log_dt, b_re, b_im, c_re, c_im, d_skip, w_ssm_glu, w_swa_up, w_mem_up, w_out, ffn2_norm, ffn2_w_in, ffn2_w_out, final_norm):
    batch, seq = x.shape[0], x.shape[1]
    n = batch * seq
    h = x.reshape(n, D_MODEL)
    mem2 = mem.reshape(batch * N_MEM, D_MODEL)
    final_w = final_norm.reshape(1, D_MODEL)
    ssm_ops = jax.vmap(_ssm_operators)(lam_re, lam_im, log_dt, b_re, b_im, c_re, c_im, d_skip)
    for l in range(DEPTH):
        mix_w = mix_norm[l].reshape(1, D_MODEL)

        h = _ffn(h, ffn1_norm[l].reshape(1, D_MODEL), ffn1_w_in, ffn1_w_out, final_w, l, apply_final_norm=False)
        qkv, s_in, mq = _mix_proj(h, mix_w, w_in, l)
        mem_kv = _norm_proj(mem2, mem_norm[l].reshape(1, D_MODEL), w_mem_kv, l, 0, 2 * MEM_WIDTH, BF16,
                            tm=batch * N_MEM)
        o_swa = _swa(qkv, sinks[l], batch, seq)
        o_mem = _mem_attn(mq, mem_kv, batch, seq)
        y_s = _ssm(s_in, ssm_ops, l, batch)
        h = _merge(h, mix_w, o_swa, y_s, o_mem, w_in, l, w_swa_up[l].astype(BF16),
                   w_ssm_glu[l].astype(BF16), w_mem_up[l].astype(BF16), w_out[l].astype(BF16))
        h = _ffn(h, ffn2_norm[l].reshape(1, D_MODEL), ffn2_w_in, ffn2_w_out, final_w, l,
                 apply_final_norm=(l == DEPTH - 1))
    return h.reshape(batch, seq, D_MODEL)
```

```python
import functools
import math

import jax
import jax.numpy as jnp
from jax import lax
from jax.experimental import pallas as pl
from jax.experimental.pallas import tpu as pltpu

D_MODEL = 2048
DEPTH = 4
N_MEM = 256
D_FF = 5632
RMS_EPS = 1e-5

WINDOW = 128
HEAD_DIM = 64
N_Q_HEADS = 16
N_KV_HEADS = 4
GQA_REP = N_Q_HEADS // N_KV_HEADS
Q_WIDTH = N_Q_HEADS * HEAD_DIM
KV_WIDTH = N_KV_HEADS * HEAD_DIM

SSM_WIDTH = 1024
SSM_GROUP = 16
SSM_GROUPS = SSM_WIDTH // SSM_GROUP
SSM_STATE = 64
SSM_CHUNK = 16
LANES = 128
SSM_LANE_TILES = SSM_WIDTH // LANES
SSM_TILE_GROUPS = LANES // SSM_GROUP

MEM_HEADS = 4
MEM_HEAD_DIM = 256
MEM_WIDTH = MEM_HEADS * MEM_HEAD_DIM

N_BRANCHES = 3
NEG_INF = -1e30

QKV_WIDTH = Q_WIDTH + 2 * KV_WIDTH
SSM_OFFSET = QKV_WIDTH
MEMQ_OFFSET = SSM_OFFSET + SSM_WIDTH
GATE_OFFSET = MEMQ_OFFSET + MEM_WIDTH

VMEM_LIMIT_BYTES = 56 * 1024 * 1024

BF16 = jnp.bfloat16
F32 = jnp.float32


def _params(semantics):
    return pltpu.CompilerParams(dimension_semantics=semantics, vmem_limit_bytes=VMEM_LIMIT_BYTES)


def _rms_normalize(x, w):
    ms = jnp.mean(x * x, axis=-1, keepdims=True)
    return (x * lax.rsqrt(ms + RMS_EPS)) * w


def _sigmoid(x):
    return 0.5 * jnp.tanh(0.5 * x) + 0.5


FFN_ROW_CHUNK = 512


class _RowTileIO:
    def __init__(self, h_hbm, out_hbm, hbuf, acc_ref, sems):
        self.h_hbm, self.out_hbm, self.hbuf, self.acc_ref, self.sems = h_hbm, out_hbm, hbuf, acc_ref, sems
        self.tm = acc_ref.shape[0]

    def _fetch(self, tile):
        return pltpu.make_async_copy(self.h_hbm.at[pl.ds(tile * self.tm, self.tm), :], self.hbuf, self.sems.at[0])

    def _write_back(self, tile):
        return pltpu.make_async_copy(self.acc_ref, self.out_hbm.at[pl.ds(tile * self.tm, self.tm), :],
                                     self.sems.at[1])

    def begin(self, xn_ref, nw_ref):
        i = pl.program_id(0)

        @pl.when(i == 0)
        def _():
            self._fetch(0).start()

        self._fetch(i).wait()
        xn_ref[...] = _rms_normalize(self.hbuf[...], nw_ref[...]).astype(BF16)

        @pl.when(i > 0)
        def _():
            self._write_back(i - 1).wait()

        self.acc_ref[...] = self.hbuf[...]

        @pl.when(i + 1 < pl.num_programs(0))
        def _():
            self._fetch(i + 1).start()

    def end(self):
        i = pl.program_id(0)
        self._write_back(i).start()

        @pl.when(i == pl.num_programs(0) - 1)
        def _():
            self._write_back(i).wait()


def _ffn_body(h_hbm, nw_ref, wg_ref, wu_ref, wo_ref, fw_ref, out_hbm, xn_ref, hbuf, acc_ref, sems, *,
              apply_final_norm):
    j = pl.program_id(1)
    io = _RowTileIO(h_hbm, out_hbm, hbuf, acc_ref, sems)

    @pl.when(j == 0)
    def _():
        io.begin(xn_ref, nw_ref)

    wg = wg_ref[...].astype(BF16)
    wu = wu_ref[...].astype(BF16)
    wo = wo_ref[...].astype(BF16)
    for r in range(acc_ref.shape[0] // FFN_ROW_CHUNK):
        rows = pl.ds(r * FFN_ROW_CHUNK, FFN_ROW_CHUNK)
        xn = xn_ref[rows, :]
        g = jnp.dot(xn, wg, preferred_element_type=F32)
        u = jnp.dot(xn, wu, preferred_element_type=F32)
        a = ((0.5 * g) * _sigmoid(g)) * u
        acc_ref[rows, :] += jnp.dot(a.astype(BF16), wo, preferred_element_type=F32)

    @pl.when(j == pl.num_programs(1) - 1)
    def _():
        if apply_final_norm:
            acc_ref[...] = _rms_normalize(acc_ref[...], fw_ref[...])
        io.end()


def _ffn(h, norm_w, w_in, w_out, final_w, layer, *, apply_final_norm, tm=1024, tf=512):
    n = h.shape[0]
    nf = D_FF // tf
    return pl.pallas_call(
        functools.partial(_ffn_body, apply_final_norm=apply_final_norm),
        grid=(n // tm, nf),
        in_specs=[
            pl.BlockSpec(memory_space=pl.ANY),
            pl.BlockSpec((1, D_MODEL), lambda i, j: (0, 0)),
            pl.BlockSpec((None, D_MODEL, tf), lambda i, j: (layer, 0, j)),
            pl.BlockSpec((None, D_MODEL, tf), lambda i, j: (layer, 0, j + nf)),
            pl.BlockSpec((None, tf, D_MODEL), lambda i, j: (layer, j, 0)),
            pl.BlockSpec((1, D_MODEL), lambda i, j: (0, 0)),
        ],
        out_specs=pl.BlockSpec(memory_space=pl.ANY),
        out_shape=jax.ShapeDtypeStruct((n, D_MODEL), F32),
        scratch_shapes=[pltpu.VMEM((tm, D_MODEL), BF16), pltpu.VMEM((tm, D_MODEL), F32),
                        pltpu.VMEM((tm, D_MODEL), F32), pltpu.SemaphoreType.DMA((2,))],
        compiler_params=_params(("arbitrary", "arbitrary")),
        name="ffn",
    )(h, norm_w, w_in, w_in, w_out, final_w)


def _proj_body(h_ref, nw_ref, w_ref, o_ref, xn_ref):
    @pl.when(pl.program_id(1) == 0)
    def _():
        xn_ref[...] = _rms_normalize(h_ref[...], nw_ref[...]).astype(BF16)

    o_ref[...] = jnp.dot(xn_ref[...], w_ref[...].astype(BF16), preferred_element_type=F32).astype(o_ref.dtype)


def _norm_proj(h, norm_w, w, layer, col0, width, out_dtype, *, tm, tn=512):
    n = h.shape[0]
    c0 = col0 // tn
    return pl.pallas_call(
        _proj_body,
        grid=(n // tm, width // tn),
        in_specs=[
            pl.BlockSpec((tm, D_MODEL), lambda i, j: (i, 0)),
            pl.BlockSpec((1, D_MODEL), lambda i, j: (0, 0)),
            pl.BlockSpec((None, D_MODEL, tn), lambda i, j: (layer, 0, c0 + j)),
        ],
        out_specs=pl.BlockSpec((tm, tn), lambda i, j: (i, j)),
        out_shape=jax.ShapeDtypeStruct((n, width), out_dtype),
        scratch_shapes=[pltpu.VMEM((tm, D_MODEL), BF16)],
        compiler_params=_params(("parallel", "arbitrary")),
        name="norm_proj",
    )(h, norm_w, w)


def _mix_proj_body(h_ref, nw_ref, w_ref, qkv_ref, s_ref, mq_ref, xn_ref, *, n_qkv, n_ssm):
    j = pl.program_id(1)

    @pl.when(j == 0)
    def _():
        xn_ref[...] = _rms_normalize(h_ref[...], nw_ref[...]).astype(BF16)

    def project(out_ref):
        out_ref[...] = jnp.dot(xn_ref[...], w_ref[...].astype(BF16), preferred_element_type=F32).astype(out_ref.dtype)

    @pl.when(j < n_qkv)
    def _():
        project(qkv_ref)

    @pl.when((j >= n_qkv) & (j < n_qkv + n_ssm))
    def _():
        project(s_ref)

    @pl.when(j >= n_qkv + n_ssm)
    def _():
        project(mq_ref)


def _mix_proj(h, norm_w, w_in, layer, *, tm=2048, tn=512):
    n = h.shape[0]
    n_qkv, n_ssm, n_mq = QKV_WIDTH // tn, SSM_WIDTH // tn, MEM_WIDTH // tn
    return pl.pallas_call(
        functools.partial(_mix_proj_body, n_qkv=n_qkv, n_ssm=n_ssm),
        grid=(n // tm, n_qkv + n_ssm + n_mq),
        in_specs=[
            pl.BlockSpec((tm, D_MODEL), lambda i, j: (i, 0), pipeline_mode=pl.Buffered(1)),
            pl.BlockSpec((1, D_MODEL), lambda i, j: (0, 0)),
            pl.BlockSpec((None, D_MODEL, tn), lambda i, j: (layer, 0, j)),
        ],
        out_specs=[
            pl.BlockSpec((tm, tn), lambda i, j: (i, jnp.minimum(j, n_qkv - 1))),
            pl.BlockSpec((tm, tn), lambda i, j: (i, jnp.clip(j - n_qkv, 0, n_ssm - 1))),
            pl.BlockSpec((tm, tn), lambda i, j: (i, jnp.clip(j - n_qkv - n_ssm, 0, n_mq - 1))),
        ],
        out_shape=[
            jax.ShapeDtypeStruct((n, QKV_WIDTH), BF16),
            jax.ShapeDtypeStruct((n, SSM_WIDTH), F32),
            jax.ShapeDtypeStruct((n, MEM_WIDTH), BF16),
        ],
        scratch_shapes=[pltpu.VMEM((tm, D_MODEL), BF16)],
        compiler_params=_params(("parallel", "arbitrary")),
        name="mix_proj",
    )(h, norm_w, w_in)


def _swa_body(sinks_ref, q_ref, kvc_ref, kvp_ref, o_ref, s_ref, p_ref, den_ref):
    blk = pl.program_id(1)
    q = q_ref[...]
    kvc = kvc_ref[...]
    kvp = kvp_ref[...]
    qi = lax.broadcasted_iota(jnp.int32, (WINDOW, 2 * WINDOW), 0)
    kj = lax.broadcasted_iota(jnp.int32, (WINDOW, 2 * WINDOW), 1)
    first_key = jnp.where(blk > 0, 0, WINDOW)
    valid = (kj > qi) & (kj <= qi + WINDOW) & (kj >= first_key)
    scale = HEAD_DIM ** -0.5
    vals = []
    for g in range(N_KV_HEADS):
        ks = slice(g * HEAD_DIM, (g + 1) * HEAD_DIM)
        vs = slice(KV_WIDTH + g * HEAD_DIM, KV_WIDTH + (g + 1) * HEAD_DIM)
        k = jnp.concatenate([kvp[:, ks], kvc[:, ks]], axis=0)
        vals.append(jnp.concatenate([kvp[:, vs], kvc[:, vs]], axis=0))
        for r in range(GQA_REP):
            h = g * GQA_REP + r
            qh = q[:, h * HEAD_DIM:(h + 1) * HEAD_DIM]
            s_ref[h] = lax.dot_general(qh, k, (((1,), (1,)), ((), ())), preferred_element_type=F32)
    for h in range(N_Q_HEADS):
        s = jnp.where(valid, s_ref[h] * scale, NEG_INF)
        sink = sinks_ref[h]
        m = jnp.maximum(jnp.max(s, axis=-1, keepdims=True), sink)
        p = jnp.exp(s - m)
        den_ref[h] = jnp.broadcast_to(jnp.sum(p, axis=-1, keepdims=True) + jnp.exp(sink - m), (WINDOW, HEAD_DIM))
        p_ref[h] = p.astype(BF16)
    for h in range(N_Q_HEADS):
        o = jnp.dot(p_ref[h], vals[h // GQA_REP], preferred_element_type=F32) / den_ref[h]
        o_ref[:, h * HEAD_DIM:(h + 1) * HEAD_DIM] = o.astype(BF16)


def _swa(qkv, sinks, batch, seq):
    nb = seq // WINDOW
    kv_col = Q_WIDTH // (2 * KV_WIDTH)
    return pl.pallas_call(
        _swa_body,
        grid=(batch, nb),
        in_specs=[
            pl.BlockSpec(memory_space=pltpu.SMEM),
            pl.BlockSpec((WINDOW, Q_WIDTH), lambda b, n: (b * nb + n, 0)),
            pl.BlockSpec((WINDOW, 2 * KV_WIDTH), lambda b, n: (b * nb + n, kv_col)),
            pl.BlockSpec((WINDOW, 2 * KV_WIDTH), lambda b, n: (b * nb + jnp.maximum(n - 1, 0), kv_col)),
        ],
        out_specs=pl.BlockSpec((WINDOW, Q_WIDTH), lambda b, n: (b * nb + n, 0)),
        out_shape=jax.ShapeDtypeStruct((batch * seq, Q_WIDTH), BF16),
        scratch_shapes=[pltpu.VMEM((N_Q_HEADS, WINDOW, 2 * WINDOW), F32),
                        pltpu.VMEM((N_Q_HEADS, WINDOW, 2 * WINDOW), BF16),
                        pltpu.VMEM((N_Q_HEADS, WINDOW, HEAD_DIM), F32)],
        compiler_params=_params(("parallel", "arbitrary")),
        name="swa",
    )(sinks, qkv, qkv, qkv)


def _mem_attn_body(q_ref, kv_ref, o_ref):
    scale = MEM_HEAD_DIM ** -0.5
    for h in range(MEM_HEADS):
        cs = slice(h * MEM_HEAD_DIM, (h + 1) * MEM_HEAD_DIM)
        vs = slice(MEM_WIDTH + h * MEM_HEAD_DIM, MEM_WIDTH + (h + 1) * MEM_HEAD_DIM)
        s = lax.dot_general(q_ref[:, cs], kv_ref[:, cs], (((1,), (1,)), ((), ())),
                            preferred_element_type=F32) * scale
        m = jnp.max(s, axis=-1, keepdims=True)
        p = jnp.exp(s - m)
        denom = jnp.sum(p, axis=-1, keepdims=True)
        o = jnp.dot(p.astype(BF16), kv_ref[:, vs], preferred_element_type=F32) / denom
        o_ref[:, cs] = o.astype(BF16)


def _mem_attn(mq, mem_kv, batch, seq, *, tq=512):
    nq = seq // tq
    return pl.pallas_call(
        _mem_attn_body,
        grid=(batch, nq),
        in_specs=[
            pl.BlockSpec((tq, MEM_WIDTH), lambda b, i: (b * nq + i, 0)),
            pl.BlockSpec((N_MEM, 2 * MEM_WIDTH), lambda b, i: (b, 0)),
        ],
        out_specs=pl.BlockSpec((tq, MEM_WIDTH), lambda b, i: (b * nq + i, 0)),
        out_shape=jax.ShapeDtypeStruct((batch * seq, MEM_WIDTH), BF16),
        compiler_params=_params(("parallel", "arbitrary")),
        name="mem_attn",
    )(mq, mem_kv)


def _ssm_operators(lam_re, lam_im, log_dt, b_re, b_im, c_re, c_im, d_skip):
    hp = lax.Precision.HIGHEST
    t_len, g_n, p_n, ch = SSM_CHUNK, SSM_GROUPS, SSM_STATE, SSM_GROUP
    lr = jnp.minimum(lam_re, -1e-4)
    li = lam_im
    dt = jnp.exp(log_dt)[:, None]
    mag = jnp.exp(lr * dt)
    ar = mag * jnp.cos(li * dt)
    ai = mag * jnp.sin(li * dt)
    nr, ni = ar - 1.0, ai
    den = lr * lr + li * li
    kr = (nr * lr + ni * li) / den
    ki = (ni * lr - nr * li) / den
    bbr = kr[..., None] * b_re - ki[..., None] * b_im
    bbi = kr[..., None] * b_im + ki[..., None] * b_re
    steps = jnp.arange(t_len + 1, dtype=F32)[None, :, None]
    pmag = jnp.exp(steps * (lr * dt)[:, None, :])
    ang = steps * (li * dt)[:, None, :]
    pr = pmag * jnp.cos(ang)
    pi = pmag * jnp.sin(ang)
    wr = pr[:, :t_len, :, None] * bbr[:, None] - pi[:, :t_len, :, None] * bbi[:, None]
    wi = pr[:, :t_len, :, None] * bbi[:, None] + pi[:, :t_len, :, None] * bbr[:, None]
    lagk = (jnp.einsum('gcp,gkpd->gkcd', c_re, wr, precision=hp)
            - jnp.einsum('gcp,gkpd->gkcd', c_im, wi, precision=hp))
    nt, gt = SSM_LANE_TILES, SSM_TILE_GROUPS
    lag_c = lagk.reshape(nt, gt, t_len, ch, ch).transpose(0, 2, 1, 4, 3).reshape(nt, t_len * LANES, ch)

    def pair(x, y):
        return jnp.concatenate([x, y], axis=-1)

    def per_step(w):
        return w.reshape(nt, gt, t_len, 2 * p_n).transpose(0, 2, 1, 3)[:, :, :, None, :]

    def per_channel(w):
        return w.reshape(nt, gt, ch, 2 * p_n)[:, None]

    back = jnp.arange(t_len - 1, -1, -1, dtype=F32)[None, :, None]
    bmag = jnp.exp(back * (lr * dt)[:, None, :])
    bang = back * (li * dt)[:, None, :]
    qr, qi = bmag * jnp.cos(bang), bmag * jnp.sin(bang)
    bbr_t, bbi_t = bbr.transpose(0, 2, 1), bbi.transpose(0, 2, 1)
    inp = (per_step(pair(qr, qr)) * per_channel(pair(bbr_t, bbi_t))
           + per_step(pair(-qi, qi)) * per_channel(pair(bbi_t, bbr_t))
           ).reshape(nt, t_len * LANES, 2 * p_n)
    pr1, pi1 = pr[:, 1:], pi[:, 1:]
    outp = (per_step(pair(pr1, pi1)) * per_channel(pair(c_re, -c_re))
            - per_step(pair(pi1, pr1)) * per_channel(pair(c_im, c_im))
            ).reshape(nt, t_len * LANES, 2 * p_n)
    a1 = jnp.concatenate([pr[:, t_len], pr[:, t_len]], axis=-1).reshape(nt, gt, 2 * p_n)
    a2 = jnp.concatenate([-pi[:, t_len], pi[:, t_len]], axis=-1).reshape(nt, gt, 2 * p_n)
    return lag_c, inp, outp, a1, a2, d_skip.reshape(1, SSM_WIDTH)


def _ssm_body(s_ref, lag_ref, inp_ref, outp_ref, a1_ref, a2_ref, d_ref, y_ref,
              u_ref, panel_ref, inpx_ref, outpx_ref, z_ref, zs_ref, sp_ref, *, batch):
    t_len, gt = SSM_CHUNK, SSM_TILE_GROUPS
    n_chunks = u_ref.shape[0]
    per_seq = n_chunks // batch
    flat = t_len * LANES

    for t in range(t_len):
        u_ref[:, t * LANES:(t + 1) * LANES] = s_ref[pl.ds(t, n_chunks, stride=t_len), :].astype(BF16)

    row_group = (lax.broadcasted_iota(jnp.int32, (flat, 1), 0) >> 4) & (gt - 1)
    col_group = lax.broadcasted_iota(jnp.int32, (1, LANES), 1) >> 4
    spread = (lax.broadcasted_iota(jnp.int32, (SSM_GROUP, LANES), 1) & (SSM_GROUP - 1)
              == lax.broadcasted_iota(jnp.int32, (SSM_GROUP, LANES), 0)).astype(BF16)
    lag = jnp.dot(lag_ref[...].astype(BF16), spread, preferred_element_type=F32)
    lag = jnp.where(row_group == col_group, lag, 0.0).astype(BF16)

    for r in range(t_len):
        k_left, k_right = t_len - 2 - r, t_len - 1 - r
        left = lag[k_left * LANES:(k_left + 1) * LANES] if k_left >= 0 else jnp.zeros((LANES, LANES), BF16)
        panel_ref[r * LANES:(r + 1) * LANES, :LANES] = left
        panel_ref[r * LANES:(r + 1) * LANES, LANES:] = lag[k_right * LANES:(k_right + 1) * LANES]

    inp = inp_ref[...]
    outp = outp_ref[...]
    for g in range(gt):
        inpx_ref[:, g * LANES:(g + 1) * LANES] = jnp.where(row_group == g, inp, 0.0).astype(BF16)
        outpx_ref[:, g * LANES:(g + 1) * LANES] = jnp.where(row_group == g, outp, 0.0).astype(BF16)

    z = jnp.dot(u_ref[...], inpx_ref[...], preferred_element_type=F32)
    for g in range(gt):
        zg = z[:, g * LANES:(g + 1) * LANES]
        z_ref[pl.ds(g, n_chunks, stride=gt), :] = zg
        zs_ref[pl.ds(g, n_chunks, stride=gt), :] = pltpu.roll(zg, SSM_STATE, axis=1)

    a1 = a1_ref[...]
    a2 = a2_ref[...]

    def step(c, carry):
        new = []
        for b in range(batch):
            v0, v1 = carry[b]
            row = pl.multiple_of((b * per_seq + c) * gt, gt)
            sp_ref[pl.ds(row, gt), :] = v0
            z0 = z_ref[pl.ds(row, gt), :]
            z1 = zs_ref[pl.ds(row, gt), :]
            new.append((a1 * v0 + a2 * v1 + z0, a1 * v1 - a2 * v0 + z1))
        return tuple(new)

    zero = jnp.zeros((gt, LANES), F32)
    lax.fori_loop(0, per_seq, step, tuple((zero, zero) for _ in range(batch)), unroll=4)

    sp = jnp.concatenate([sp_ref[pl.ds(g, n_chunks, stride=gt), :] for g in range(gt)], axis=1).astype(BF16)
    d2 = jnp.concatenate([d_ref[...], d_ref[...]], axis=1)
    for q in range(t_len // 2):
        cols = slice(2 * q * LANES, (2 * q + 2) * LANES)
        k_len = (2 * q + 2) * LANES
        y = jnp.dot(u_ref[:, :k_len], panel_ref[flat - k_len:, :], preferred_element_type=F32)
        y = y + lax.dot_general(sp, outpx_ref[cols, :], (((1,), (1,)), ((), ())), preferred_element_type=F32)
        y = y + d2 * u_ref[:, cols].astype(F32)
        y_ref[pl.ds(2 * q, n_chunks, stride=t_len), :] = y[:, :LANES]
        y_ref[pl.ds(2 * q + 1, n_chunks, stride=t_len), :] = y[:, LANES:]


def _ssm(s_in, ops, layer, batch):
    lag_c, inp, outp, a1, a2, d_row = ops
    n = s_in.shape[0]
    n_chunks = n // SSM_CHUNK
    flat = SSM_CHUNK * LANES
    return pl.pallas_call(
        functools.partial(_ssm_body, batch=batch),
        grid=(SSM_LANE_TILES,),
        in_specs=[
            pl.BlockSpec((n, LANES), lambda j: (0, j)),
            pl.BlockSpec((None, None, flat, SSM_GROUP), lambda j: (layer, j, 0, 0)),
            pl.BlockSpec((None, None, flat, 2 * SSM_STATE), lambda j: (layer, j, 0, 0)),
            pl.BlockSpec((None, None, flat, 2 * SSM_STATE), lambda j: (layer, j, 0, 0)),
            pl.BlockSpec((None, None, SSM_TILE_GROUPS, 2 * SSM_STATE), lambda j: (layer, j, 0, 0)),
            pl.BlockSpec((None, None, SSM_TILE_GROUPS, 2 * SSM_STATE), lambda j: (layer, j, 0, 0)),
            pl.BlockSpec((None, 1, LANES), lambda j: (layer, 0, j)),
        ],
        out_specs=pl.BlockSpec((n, LANES), lambda j: (0, j)),
        out_shape=jax.ShapeDtypeStruct((n, SSM_WIDTH), F32),
        scratch_shapes=[
            pltpu.VMEM((n_chunks, flat), BF16),
            pltpu.VMEM((flat, 2 * LANES), BF16),
            pltpu.VMEM((flat, SSM_TILE_GROUPS * 2 * SSM_STATE), BF16),
            pltpu.VMEM((flat, SSM_TILE_GROUPS * 2 * SSM_STATE), BF16),
            pltpu.VMEM((n_chunks * SSM_TILE_GROUPS, 2 * SSM_STATE), F32),
            pltpu.VMEM((n_chunks * SSM_TILE_GROUPS, 2 * SSM_STATE), F32),
            pltpu.VMEM((n_chunks * SSM_TILE_GROUPS, 2 * SSM_STATE), F32),
        ],
        compiler_params=_params(("parallel",)),
        name="ssm",
    )(s_in, lag_c, inp, outp, a1, a2, d_row)


MERGE_ROW_CHUNK = 256


def _merge_body(h_hbm, nw_ref, swa_ref, ssm_ref, mem_ref, wg0_ref, wg1_ref, wg2_ref, wswa_ref,
                wga_ref, wgb_ref, wmem_ref, wo_ref, out_hbm, xn_ref, hbuf, o_ref, sems):
    j = pl.program_id(1)
    io = _RowTileIO(h_hbm, out_hbm, hbuf, o_ref, sems)

    @pl.when(j == 0)
    def _():
        io.begin(xn_ref, nw_ref)

    tn = wo_ref.shape[0]
    half_k = D_MODEL // 2
    wg01 = jnp.concatenate([wg0_ref[...].astype(BF16), wg1_ref[...].astype(BF16)], axis=1)
    wg2 = wg2_ref[...].astype(BF16)
    wglu = jnp.concatenate([wga_ref[...], wgb_ref[...]], axis=1)
    for r in range(o_ref.shape[0] // MERGE_ROW_CHUNK):
        rows = pl.ds(r * MERGE_ROW_CHUNK, MERGE_ROW_CHUNK)
        xn = xn_ref[rows, :]
        ys = ssm_ref[rows, :].astype(BF16)
        y_swa = jnp.dot(swa_ref[rows, :], wswa_ref[...], preferred_element_type=F32)
        y_mem = jnp.dot(mem_ref[rows, :], wmem_ref[...], preferred_element_type=F32)
        glu = jnp.dot(ys, wglu, preferred_element_type=F32)
        y_ssm = glu[:, :tn] * _sigmoid(glu[:, tn:])
        g01 = jnp.dot(xn, wg01, preferred_element_type=F32)
        g2 = (jnp.dot(xn[:, :half_k], wg2[:half_k], preferred_element_type=F32)
              + jnp.dot(xn[:, half_k:], wg2[half_k:], preferred_element_type=F32))
        merged = (_sigmoid(g01[:, :tn]) * y_swa + _sigmoid(g01[:, tn:]) * y_ssm + _sigmoid(g2) * y_mem).astype(BF16)
        for c in range(2):
            cols = pl.ds(c * half_k, half_k)
            o_ref[rows, cols] += jnp.dot(merged, wo_ref[:, cols], preferred_element_type=F32)

    @pl.when(j == pl.num_programs(1) - 1)
    def _():
        io.end()


def _merge(h, norm_w, o_swa, y_s, o_mem, w_in, layer, w_swa_up, w_ssm_glu, w_mem_up, w_out, *, tm=1024, tn=256):
    n = h.shape[0]
    nj = D_MODEL // tn
    g0 = GATE_OFFSET // tn
    row = lambda i, j: (i, 0)
    once = pl.Buffered(1)
    return pl.pallas_call(
        _merge_body,
        grid=(n // tm, nj),
        in_specs=[
            pl.BlockSpec(memory_space=pl.ANY),
            pl.BlockSpec((1, D_MODEL), lambda i, j: (0, 0)),
            pl.BlockSpec((tm, Q_WIDTH), row, pipeline_mode=once),
            pl.BlockSpec((tm, SSM_WIDTH), row, pipeline_mode=once),
            pl.BlockSpec((tm, MEM_WIDTH), row, pipeline_mode=once),
            pl.BlockSpec((None, D_MODEL, tn), lambda i, j: (layer, 0, g0 + j)),
            pl.BlockSpec((None, D_MODEL, tn), lambda i, j: (layer, 0, g0 + j + nj)),
            pl.BlockSpec((None, D_MODEL, tn), lambda i, j: (layer, 0, g0 + j + 2 * nj)),
            pl.BlockSpec((Q_WIDTH, tn), lambda i, j: (0, j)),
            pl.BlockSpec((SSM_WIDTH, tn), lambda i, j: (0, j)),
            pl.BlockSpec((SSM_WIDTH, tn), lambda i, j: (0, j + nj)),
            pl.BlockSpec((MEM_WIDTH, tn), lambda i, j: (0, j)),
            pl.BlockSpec((tn, D_MODEL), lambda i, j: (j, 0)),
        ],
        out_specs=pl.BlockSpec(memory_space=pl.ANY),
        out_shape=jax.ShapeDtypeStruct((n, D_MODEL), F32),
        scratch_shapes=[pltpu.VMEM((tm, D_MODEL), BF16), pltpu.VMEM((tm, D_MODEL), F32),
                        pltpu.VMEM((tm, D_MODEL), F32), pltpu.SemaphoreType.DMA((2,))],
        compiler_params=_params(("arbitrary", "arbitrary")),
        name="merge",
    )(h, norm_w, o_swa, y_s, o_mem, w_in, w_in, w_in, w_swa_up, w_ssm_glu, w_ssm_glu, w_mem_up, w_out)


def kernel(x, mem, ffn1_norm, ffn1_w_in, ffn1_w_out, mix_norm, mem_norm, w_in, sinks, w_mem_kv, lam_re, lam_im, log_dt, b_re, b_im, c_re, c_im, d_skip, w_ssm_glu, w_swa_up, w_mem_up, w_out, ffn2_norm, ffn2_w_in, ffn2_w_out, final_norm):
    batch, seq = x.shape[0], x.shape[1]
    n = batch * seq
    h = x.reshape(n, D_MODEL)
    mem2 = mem.reshape(batch * N_MEM, D_MODEL)
    final_w = final_norm.reshape(1, D_MODEL)
    ssm_ops = jax.vmap(_ssm_operators)(lam_re, lam_im, log_dt, b_re, b_im, c_re, c_im, d_skip)
    for l in range(DEPTH):
        mix_w = mix_norm[l].reshape(1, D_MODEL)

        h = _ffn(h, ffn1_norm[l].reshape(1, D_MODEL), ffn1_w_in, ffn1_w_out, final_w, l, apply_final_norm=False)
        qkv, s_in, mq = _mix_proj(h, mix_w, w_in, l)
        mem_kv = _norm_proj(mem2, mem_norm[l].reshape(1, D_MODEL), w_mem_kv, l, 0, 2 * MEM_WIDTH, BF16,
                            tm=batch * N_MEM)
        o_swa = _swa(qkv, sinks[l], batch, seq)
        o_mem = _mem_attn(mq, mem_kv, batch, seq)
        y_s = _ssm(s_in, ssm_ops, l, batch)
        h = _merge(h, mix_w, o_swa, y_s, o_mem, w_in, l, w_swa_up[l].astype(BF16),
                   w_ssm_glu[l].astype(BF16), w_mem_up[l].astype(BF16), w_out[l].astype(BF16))
        h = _ffn(h, ffn2_norm[l].reshape(1, D_MODEL), ffn2_w_in, ffn2_w_out, final_w, l,
                 apply_final_norm=(l == DEPTH - 1))
    return h.reshape(batch, seq, D_MODEL)
```

```python
import functools
import math

import jax
import jax.numpy as jnp
from jax import lax
from jax.experimental import pallas as pl
from jax.experimental.pallas import tpu as pltpu

D_MODEL = 2048
DEPTH = 4
N_MEM = 256
D_FF = 5632
RMS_EPS = 1e-5

WINDOW = 128
HEAD_DIM = 64
N_Q_HEADS = 16
N_KV_HEADS = 4
GQA_REP = N_Q_HEADS // N_KV_HEADS
Q_WIDTH = N_Q_HEADS * HEAD_DIM
KV_WIDTH = N_KV_HEADS * HEAD_DIM

SSM_WIDTH = 1024
SSM_GROUP = 16
SSM_GROUPS = SSM_WIDTH // SSM_GROUP
SSM_STATE = 64
SSM_CHUNK = 16
LANES = 128
SSM_LANE_TILES = SSM_WIDTH // LANES
SSM_TILE_GROUPS = LANES // SSM_GROUP

MEM_HEADS = 4
MEM_HEAD_DIM = 256
MEM_WIDTH = MEM_HEADS * MEM_HEAD_DIM

N_BRANCHES = 3
NEG_INF = -1e30

QKV_WIDTH = Q_WIDTH + 2 * KV_WIDTH
SSM_OFFSET = QKV_WIDTH
MEMQ_OFFSET = SSM_OFFSET + SSM_WIDTH
GATE_OFFSET = MEMQ_OFFSET + MEM_WIDTH

VMEM_LIMIT_BYTES = 56 * 1024 * 1024

BF16 = jnp.bfloat16
F32 = jnp.float32


def _params(semantics):
    return pltpu.CompilerParams(dimension_semantics=semantics, vmem_limit_bytes=VMEM_LIMIT_BYTES)


def _rms_normalize(x, w):
    ms = jnp.mean(x * x, axis=-1, keepdims=True)
    return (x * lax.rsqrt(ms + RMS_EPS)) * w


def _sigmoid(x):
    return 0.5 * jnp.tanh(0.5 * x) + 0.5


FFN_ROW_CHUNK = 512


class _RowTileIO:
    def __init__(self, h_hbm, out_hbm, hbuf, acc_ref, sems):
        self.h_hbm, self.out_hbm, self.hbuf, self.acc_ref, self.sems = h_hbm, out_hbm, hbuf, acc_ref, sems
        self.tm = acc_ref.shape[0]

    def _fetch(self, tile):
        return pltpu.make_async_copy(self.h_hbm.at[pl.ds(tile * self.tm, self.tm), :], self.hbuf, self.sems.at[0])

    def _write_back(self, tile):
        return pltpu.make_async_copy(self.acc_ref, self.out_hbm.at[pl.ds(tile * self.tm, self.tm), :],
                                     self.sems.at[1])

    def begin(self, xn_ref, nw_ref):
        i = pl.program_id(0)

        @pl.when(i == 0)
        def _():
            self._fetch(0).start()

        self._fetch(i).wait()
        xn_ref[...] = _rms_normalize(self.hbuf[...], nw_ref[...]).astype(BF16)

        @pl.when(i > 0)
        def _():
            self._write_back(i - 1).wait()

        self.acc_ref[...] = self.hbuf[...]

        @pl.when(i + 1 < pl.num_programs(0))
        def _():
            self._fetch(i + 1).start()

    def end(self):
        i = pl.program_id(0)
        self._write_back(i).start()

        @pl.when(i == pl.num_programs(0) - 1)
        def _():
            self._write_back(i).wait()


def _ffn_body(h_hbm, nw_ref, wg_ref, wu_ref, wo_ref, fw_ref, out_hbm, xn_ref, hbuf, acc_ref, sems, *,
              apply_final_norm):
    j = pl.program_id(1)
    io = _RowTileIO(h_hbm, out_hbm, hbuf, acc_ref, sems)

    @pl.when(j == 0)
    def _():
        io.begin(xn_ref, nw_ref)

    wg = wg_ref[...].astype(BF16)
    wu = wu_ref[...].astype(BF16)
    wo = wo_ref[...].astype(BF16)
    for r in range(acc_ref.shape[0] // FFN_ROW_CHUNK):
        rows = pl.ds(r * FFN_ROW_CHUNK, FFN_ROW_CHUNK)
        xn = xn_ref[rows, :]
        g = jnp.dot(xn, wg, preferred_element_type=F32)
        u = jnp.dot(xn, wu, preferred_element_type=F32)
        a = ((0.5 * g) * _sigmoid(g)) * u
        acc_ref[rows, :] += jnp.dot(a.astype(BF16), wo, preferred_element_type=F32)

    @pl.when(j == pl.num_programs(1) - 1)
    def _():
        if apply_final_norm:
            acc_ref[...] = _rms_normalize(acc_ref[...], fw_ref[...])
        io.end()


def _ffn(h, norm_w, w_in, w_out, final_w, layer, *, apply_final_norm, tm=1024, tf=512):
    n = h.shape[0]
    nf = D_FF // tf
    return pl.pallas_call(
        functools.partial(_ffn_body, apply_final_norm=apply_final_norm),
        grid=(n // tm, nf),
        in_specs=[
            pl.BlockSpec(memory_space=pl.ANY),
            pl.BlockSpec((1, D_MODEL), lambda i, j: (0, 0)),
            pl.BlockSpec((None, D_MODEL, tf), lambda i, j: (layer, 0, j)),
            pl.BlockSpec((None, D_MODEL, tf), lambda i, j: (layer, 0, j + nf)),
            pl.BlockSpec((None, tf, D_MODEL), lambda i, j: (layer, j, 0)),
            pl.BlockSpec((1, D_MODEL), lambda i, j: (0, 0)),
        ],
        out_specs=pl.BlockSpec(memory_space=pl.ANY),
        out_shape=jax.ShapeDtypeStruct((n, D_MODEL), F32),
        scratch_shapes=[pltpu.VMEM((tm, D_MODEL), BF16), pltpu.VMEM((tm, D_MODEL), F32),
                        pltpu.VMEM((tm, D_MODEL), F32), pltpu.SemaphoreType.DMA((2,))],
        compiler_params=_params(("arbitrary", "arbitrary")),
        name="ffn",
    )(h, norm_w, w_in, w_in, w_out, final_w)


def _proj_body(h_ref, nw_ref, w_ref, o_ref, xn_ref):
    @pl.when(pl.program_id(1) == 0)
    def _():
        xn_ref[...] = _rms_normalize(h_ref[...], nw_ref[...]).astype(BF16)

    o_ref[...] = jnp.dot(xn_ref[...], w_ref[...].astype(BF16), preferred_element_type=F32).astype(o_ref.dtype)


def _norm_proj(h, norm_w, w, layer, col0, width, out_dtype, *, tm, tn=512):
    n = h.shape[0]
    c0 = col0 // tn
    return pl.pallas_call(
        _proj_body,
        grid=(n // tm, width // tn),
        in_specs=[
            pl.BlockSpec((tm, D_MODEL), lambda i, j: (i, 0)),
            pl.BlockSpec((1, D_MODEL), lambda i, j: (0, 0)),
            pl.BlockSpec((None, D_MODEL, tn), lambda i, j: (layer, 0, c0 + j)),
        ],
        out_specs=pl.BlockSpec((tm, tn), lambda i, j: (i, j)),
        out_shape=jax.ShapeDtypeStruct((n, width), out_dtype),
        scratch_shapes=[pltpu.VMEM((tm, D_MODEL), BF16)],
        compiler_params=_params(("parallel", "arbitrary")),
        name="norm_proj",
    )(h, norm_w, w)


def _mix_proj_body(h_ref, nw_ref, w_ref, qkv_ref, s_ref, mq_ref, xn_ref, *, n_qkv, n_ssm):
    j = pl.program_id(1)

    @pl.when(j == 0)
    def _():
        xn_ref[...] = _rms_normalize(h_ref[...], nw_ref[...]).astype(BF16)

    def project(out_ref):
        out_ref[...] = jnp.dot(xn_ref[...], w_ref[...].astype(BF16), preferred_element_type=F32).astype(out_ref.dtype)

    @pl.when(j < n_qkv)
    def _():
        project(qkv_ref)

    @pl.when((j >= n_qkv) & (j < n_qkv + n_ssm))
    def _():
        project(s_ref)

    @pl.when(j >= n_qkv + n_ssm)
    def _():
        project(mq_ref)


def _mix_proj(h, norm_w, w_in, layer, *, tm=2048, tn=512):
    n = h.shape[0]
    n_qkv, n_ssm, n_mq = QKV_WIDTH // tn, SSM_WIDTH // tn, MEM_WIDTH // tn
    return pl.pallas_call(
        functools.partial(_mix_proj_body, n_qkv=n_qkv, n_ssm=n_ssm),
        grid=(n // tm, n_qkv + n_ssm + n_mq),
        in_specs=[
            pl.BlockSpec((tm, D_MODEL), lambda i, j: (i, 0), pipeline_mode=pl.Buffered(1)),
            pl.BlockSpec((1, D_MODEL), lambda i, j: (0, 0)),
            pl.BlockSpec((None, D_MODEL, tn), lambda i, j: (layer, 0, j)),
        ],
        out_specs=[
            pl.BlockSpec((tm, tn), lambda i, j: (i, jnp.minimum(j, n_qkv - 1))),
            pl.BlockSpec((tm, tn), lambda i, j: (i, jnp.clip(j - n_qkv, 0, n_ssm - 1))),
            pl.BlockSpec((tm, tn), lambda i, j: (i, jnp.clip(j - n_qkv - n_ssm, 0, n_mq - 1))),
            pl.BlockSpec((tm, D_MODEL), lambda i, j: (i, 0), pipeline_mode=pl.Buffered(1)),
        ],
        out_shape=[
            jax.ShapeDtypeStruct((n, QKV_WIDTH), BF16),
            jax.ShapeDtypeStruct((n, SSM_WIDTH), F32),
            jax.ShapeDtypeStruct((n, MEM_WIDTH), BF16),
            jax.ShapeDtypeStruct((n, D_MODEL), BF16),
        ],
        compiler_params=_params(("parallel", "arbitrary")),
        name="mix_proj",
    )(h, norm_w, w_in)


def _swa_body(sinks_ref, q_ref, kvc_ref, kvp_ref, o_ref, s_ref, p_ref, den_ref):
    blk = pl.program_id(1)
    q = q_ref[...]
    kvc = kvc_ref[...]
    kvp = kvp_ref[...]
    qi = lax.broadcasted_iota(jnp.int32, (WINDOW, 2 * WINDOW), 0)
    kj = lax.broadcasted_iota(jnp.int32, (WINDOW, 2 * WINDOW), 1)
    first_key = jnp.where(blk > 0, 0, WINDOW)
    valid = (kj > qi) & (kj <= qi + WINDOW) & (kj >= first_key)
    scale = HEAD_DIM ** -0.5
    vals = []
    for g in range(N_KV_HEADS):
        ks = slice(g * HEAD_DIM, (g + 1) * HEAD_DIM)
        vs = slice(KV_WIDTH + g * HEAD_DIM, KV_WIDTH + (g + 1) * HEAD_DIM)
        k = jnp.concatenate([kvp[:, ks], kvc[:, ks]], axis=0)
        vals.append(jnp.concatenate([kvp[:, vs], kvc[:, vs]], axis=0))
        for r in range(GQA_REP):
            h = g * GQA_REP + r
            qh = q[:, h * HEAD_DIM:(h + 1) * HEAD_DIM]
            s_ref[h] = lax.dot_general(qh, k, (((1,), (1,)), ((), ())), preferred_element_type=F32)
    for h in range(N_Q_HEADS):
        s = jnp.where(valid, s_ref[h] * scale, NEG_INF)
        sink = sinks_ref[h]
        m = jnp.maximum(jnp.max(s, axis=-1, keepdims=True), sink)
        p = jnp.exp(s - m)
        den_ref[h] = jnp.broadcast_to(jnp.sum(p, axis=-1, keepdims=True) + jnp.exp(sink - m), (WINDOW, HEAD_DIM))
        p_ref[h] = p.astype(BF16)
    for h in range(N_Q_HEADS):
        o = jnp.dot(p_ref[h], vals[h // GQA_REP], preferred_element_type=F32) / den_ref[h]
        o_ref[:, h * HEAD_DIM:(h + 1) * HEAD_DIM] = o.astype(BF16)


def _swa(qkv, sinks, batch, seq):
    nb = seq // WINDOW
    kv_col = Q_WIDTH // (2 * KV_WIDTH)
    return pl.pallas_call(
        _swa_body,
        grid=(batch, nb),
        in_specs=[
            pl.BlockSpec(memory_space=pltpu.SMEM),
            pl.BlockSpec((WINDOW, Q_WIDTH), lambda b, n: (b * nb + n, 0)),
            pl.BlockSpec((WINDOW, 2 * KV_WIDTH), lambda b, n: (b * nb + n, kv_col)),
            pl.BlockSpec((WINDOW, 2 * KV_WIDTH), lambda b, n: (b * nb + jnp.maximum(n - 1, 0), kv_col)),
        ],
        out_specs=pl.BlockSpec((WINDOW, Q_WIDTH), lambda b, n: (b * nb + n, 0)),
        out_shape=jax.ShapeDtypeStruct((batch * seq, Q_WIDTH), BF16),
        scratch_shapes=[pltpu.VMEM((N_Q_HEADS, WINDOW, 2 * WINDOW), F32),
                        pltpu.VMEM((N_Q_HEADS, WINDOW, 2 * WINDOW), BF16),
                        pltpu.VMEM((N_Q_HEADS, WINDOW, HEAD_DIM), F32)],
        compiler_params=_params(("parallel", "arbitrary")),
        name="swa",
    )(sinks, qkv, qkv, qkv)


def _mem_attn_body(q_ref, kv_ref, o_ref):
    scale = MEM_HEAD_DIM ** -0.5
    for h in range(MEM_HEADS):
        cs = slice(h * MEM_HEAD_DIM, (h + 1) * MEM_HEAD_DIM)
        vs = slice(MEM_WIDTH + h * MEM_HEAD_DIM, MEM_WIDTH + (h + 1) * MEM_HEAD_DIM)
        s = lax.dot_general(q_ref[:, cs], kv_ref[:, cs], (((1,), (1,)), ((), ())),
                            preferred_element_type=F32) * scale
        m = jnp.max(s, axis=-1, keepdims=True)
        p = jnp.exp(s - m)
        denom = jnp.sum(p, axis=-1, keepdims=True)
        o = jnp.dot(p.astype(BF16), kv_ref[:, vs], preferred_element_type=F32) / denom
        o_ref[:, cs] = o.astype(BF16)


def _mem_attn(mq, mem_kv, batch, seq, *, tq=512):
    nq = seq // tq
    return pl.pallas_call(
        _mem_attn_body,
        grid=(batch, nq),
        in_specs=[
            pl.BlockSpec((tq, MEM_WIDTH), lambda b, i: (b * nq + i, 0)),
            pl.BlockSpec((N_MEM, 2 * MEM_WIDTH), lambda b, i: (b, 0)),
        ],
        out_specs=pl.BlockSpec((tq, MEM_WIDTH), lambda b, i: (b * nq + i, 0)),
        out_shape=jax.ShapeDtypeStruct((batch * seq, MEM_WIDTH), BF16),
        compiler_params=_params(("parallel", "arbitrary")),
        name="mem_attn",
    )(mq, mem_kv)


def _ssm_operators(lam_re, lam_im, log_dt, b_re, b_im, c_re, c_im, d_skip):
    hp = lax.Precision.HIGHEST
    t_len, g_n, p_n, ch = SSM_CHUNK, SSM_GROUPS, SSM_STATE, SSM_GROUP
    lr = jnp.minimum(lam_re, -1e-4)
    li = lam_im
    dt = jnp.exp(log_dt)[:, None]
    mag = jnp.exp(lr * dt)
    ar = mag * jnp.cos(li * dt)
    ai = mag * jnp.sin(li * dt)
    nr, ni = ar - 1.0, ai
    den = lr * lr + li * li
    kr = (nr * lr + ni * li) / den
    ki = (ni * lr - nr * li) / den
    bbr = kr[..., None] * b_re - ki[..., None] * b_im
    bbi = kr[..., None] * b_im + ki[..., None] * b_re
    steps = jnp.arange(t_len + 1, dtype=F32)[None, :, None]
    pmag = jnp.exp(steps * (lr * dt)[:, None, :])
    ang = steps * (li * dt)[:, None, :]
    pr = pmag * jnp.cos(ang)
    pi = pmag * jnp.sin(ang)
    wr = pr[:, :t_len, :, None] * bbr[:, None] - pi[:, :t_len, :, None] * bbi[:, None]
    wi = pr[:, :t_len, :, None] * bbi[:, None] + pi[:, :t_len, :, None] * bbr[:, None]
    lagk = (jnp.einsum('gcp,gkpd->gkcd', c_re, wr, precision=hp)
            - jnp.einsum('gcp,gkpd->gkcd', c_im, wi, precision=hp))
    nt, gt = SSM_LANE_TILES, SSM_TILE_GROUPS
    lag_c = lagk.reshape(nt, gt, t_len, ch, ch).transpose(0, 2, 1, 4, 3).reshape(nt, t_len * LANES, ch)

    def pair(x, y):
        return jnp.concatenate([x, y], axis=-1)

    def per_step(w):
        return w.reshape(nt, gt, t_len, 2 * p_n).transpose(0, 2, 1, 3)[:, :, :, None, :]

    def per_channel(w):
        return w.reshape(nt, gt, ch, 2 * p_n)[:, None]

    back = jnp.arange(t_len - 1, -1, -1, dtype=F32)[None, :, None]
    bmag = jnp.exp(back * (lr * dt)[:, None, :])
    bang = back * (li * dt)[:, None, :]
    qr, qi = bmag * jnp.cos(bang), bmag * jnp.sin(bang)
    bbr_t, bbi_t = bbr.transpose(0, 2, 1), bbi.transpose(0, 2, 1)
    inp = (per_step(pair(qr, qr)) * per_channel(pair(bbr_t, bbi_t))
           + per_step(pair(-qi, qi)) * per_channel(pair(bbi_t, bbr_t))
           ).reshape(nt, t_len * LANES, 2 * p_n)
    pr1, pi1 = pr[:, 1:], pi[:, 1:]
    outp = (per_step(pair(pr1, pi1)) * per_channel(pair(c_re, -c_re))
            - per_step(pair(pi1, pr1)) * per_channel(pair(c_im, c_im))
            ).reshape(nt, t_len * LANES, 2 * p_n)
    a1 = jnp.concatenate([pr[:, t_len], pr[:, t_len]], axis=-1).reshape(nt, gt, 2 * p_n)
    a2 = jnp.concatenate([-pi[:, t_len], pi[:, t_len]], axis=-1).reshape(nt, gt, 2 * p_n)
    return lag_c, inp, outp, a1, a2, d_skip.reshape(1, SSM_WIDTH)


def _ssm_body(s_ref, lag_ref, inp_ref, outp_ref, a1_ref, a2_ref, d_ref, y_ref,
              u_ref, panel_ref, inpx_ref, outpx_ref, z_ref, zs_ref, sp_ref, *, batch):
    t_len, gt = SSM_CHUNK, SSM_TILE_GROUPS
    n_chunks = u_ref.shape[0]
    per_seq = n_chunks // batch
    flat = t_len * LANES

    for t in range(t_len):
        u_ref[:, t * LANES:(t + 1) * LANES] = s_ref[pl.ds(t, n_chunks, stride=t_len), :].astype(BF16)

    row_group = (lax.broadcasted_iota(jnp.int32, (flat, 1), 0) >> 4) & (gt - 1)
    col_group = lax.broadcasted_iota(jnp.int32, (1, LANES), 1) >> 4
    spread = (lax.broadcasted_iota(jnp.int32, (SSM_GROUP, LANES), 1) & (SSM_GROUP - 1)
              == lax.broadcasted_iota(jnp.int32, (SSM_GROUP, LANES), 0)).astype(BF16)
    lag = jnp.dot(lag_ref[...].astype(BF16), spread, preferred_element_type=F32)
    lag = jnp.where(row_group == col_group, lag, 0.0).astype(BF16)

    for r in range(t_len):
        k_left, k_right = t_len - 2 - r, t_len - 1 - r
        left = lag[k_left * LANES:(k_left + 1) * LANES] if k_left >= 0 else jnp.zeros((LANES, LANES), BF16)
        panel_ref[r * LANES:(r + 1) * LANES, :LANES] = left
        panel_ref[r * LANES:(r + 1) * LANES, LANES:] = lag[k_right * LANES:(k_right + 1) * LANES]

    inp = inp_ref[...]
    outp = outp_ref[...]
    for g in range(gt):
        inpx_ref[:, g * LANES:(g + 1) * LANES] = jnp.where(row_group == g, inp, 0.0).astype(BF16)
        outpx_ref[:, g * LANES:(g + 1) * LANES] = jnp.where(row_group == g, outp, 0.0).astype(BF16)

    z = jnp.dot(u_ref[...], inpx_ref[...], preferred_element_type=F32)
    for g in range(gt):
        zg = z[:, g * LANES:(g + 1) * LANES]
        z_ref[pl.ds(g, n_chunks, stride=gt), :] = zg
        zs_ref[pl.ds(g, n_chunks, stride=gt), :] = pltpu.roll(zg, SSM_STATE, axis=1)

    a1 = a1_ref[...]
    a2 = a2_ref[...]

    def step(c, carry):
        new = []
        for b in range(batch):
            v0, v1 = carry[b]
            row = pl.multiple_of((b * per_seq + c) * gt, gt)
            sp_ref[pl.ds(row, gt), :] = v0
            z0 = z_ref[pl.ds(row, gt), :]
            z1 = zs_ref[pl.ds(row, gt), :]
            new.append((a1 * v0 + a2 * v1 + z0, a1 * v1 - a2 * v0 + z1))
        return tuple(new)

    zero = jnp.zeros((gt, LANES), F32)
    lax.fori_loop(0, per_seq, step, tuple((zero, zero) for _ in range(batch)), unroll=4)

    sp = jnp.concatenate([sp_ref[pl.ds(g, n_chunks, stride=gt), :] for g in range(gt)], axis=1).astype(BF16)
    d2 = jnp.concatenate([d_ref[...], d_ref[...]], axis=1)
    for q in range(t_len // 2):
        cols = slice(2 * q * LANES, (2 * q + 2) * LANES)
        k_len = (2 * q + 2) * LANES
        y = jnp.dot(u_ref[:, :k_len], panel_ref[flat - k_len:, :], preferred_element_type=F32)
        y = y + lax.dot_general(sp, outpx_ref[cols, :], (((1,), (1,)), ((), ())), preferred_element_type=F32)
        y = y + d2 * u_ref[:, cols].astype(F32)
        y_ref[pl.ds(2 * q, n_chunks, stride=t_len), :] = y[:, :LANES]
        y_ref[pl.ds(2 * q + 1, n_chunks, stride=t_len), :] = y[:, LANES:]


def _ssm(s_in, ops, layer, batch):
    lag_c, inp, outp, a1, a2, d_row = ops
    n = s_in.shape[0]
    n_chunks = n // SSM_CHUNK
    flat = SSM_CHUNK * LANES
    return pl.pallas_call(
        functools.partial(_ssm_body, batch=batch),
        grid=(SSM_LANE_TILES,),
        in_specs=[
            pl.BlockSpec((n, LANES), lambda j: (0, j)),
            pl.BlockSpec((None, None, flat, SSM_GROUP), lambda j: (layer, j, 0, 0)),
            pl.BlockSpec((None, None, flat, 2 * SSM_STATE), lambda j: (layer, j, 0, 0)),
            pl.BlockSpec((None, None, flat, 2 * SSM_STATE), lambda j: (layer, j, 0, 0)),
            pl.BlockSpec((None, None, SSM_TILE_GROUPS, 2 * SSM_STATE), lambda j: (layer, j, 0, 0)),
            pl.BlockSpec((None, None, SSM_TILE_GROUPS, 2 * SSM_STATE), lambda j: (layer, j, 0, 0)),
            pl.BlockSpec((None, 1, LANES), lambda j: (layer, 0, j)),
        ],
        out_specs=pl.BlockSpec((n, LANES), lambda j: (0, j)),
        out_shape=jax.ShapeDtypeStruct((n, SSM_WIDTH), F32),
        scratch_shapes=[
            pltpu.VMEM((n_chunks, flat), BF16),
            pltpu.VMEM((flat, 2 * LANES), BF16),
            pltpu.VMEM((flat, SSM_TILE_GROUPS * 2 * SSM_STATE), BF16),
            pltpu.VMEM((flat, SSM_TILE_GROUPS * 2 * SSM_STATE), BF16),
            pltpu.VMEM((n_chunks * SSM_TILE_GROUPS, 2 * SSM_STATE), F32),
            pltpu.VMEM((n_chunks * SSM_TILE_GROUPS, 2 * SSM_STATE), F32),
            pltpu.VMEM((n_chunks * SSM_TILE_GROUPS, 2 * SSM_STATE), F32),
        ],
        compiler_params=_params(("parallel",)),
        name="ssm",
    )(s_in, lag_c, inp, outp, a1, a2, d_row)


MERGE_ROW_CHUNK = 256


def _gate_merge_body(xn_ref, swa_ref, ssm_ref, mem_ref, wg0_ref, wg1_ref, wg2_ref, wswa_ref,
                     wga_ref, wgb_ref, wmem_ref, o_ref):
    tn = o_ref.shape[1]
    half_k = D_MODEL // 2
    wg01 = jnp.concatenate([wg0_ref[...].astype(BF16), wg1_ref[...].astype(BF16)], axis=1)
    wg2 = wg2_ref[...].astype(BF16)
    wglu = jnp.concatenate([wga_ref[...], wgb_ref[...]], axis=1)
    for r in range(o_ref.shape[0] // MERGE_ROW_CHUNK):
        rows = pl.ds(r * MERGE_ROW_CHUNK, MERGE_ROW_CHUNK)
        xn = xn_ref[rows, :]
        ys = ssm_ref[rows, :].astype(BF16)
        y_swa = jnp.dot(swa_ref[rows, :], wswa_ref[...], preferred_element_type=F32)
        y_mem = jnp.dot(mem_ref[rows, :], wmem_ref[...], preferred_element_type=F32)
        glu = jnp.dot(ys, wglu, preferred_element_type=F32)
        y_ssm = glu[:, :tn] * _sigmoid(glu[:, tn:])
        g01 = jnp.dot(xn, wg01, preferred_element_type=F32)
        g2 = (jnp.dot(xn[:, :half_k], wg2[:half_k], preferred_element_type=F32)
              + jnp.dot(xn[:, half_k:], wg2[half_k:], preferred_element_type=F32))
        merged = _sigmoid(g01[:, :tn]) * y_swa + _sigmoid(g01[:, tn:]) * y_ssm + _sigmoid(g2) * y_mem
        o_ref[rows, :] = merged.astype(BF16)


def _gate_merge(xn, o_swa, y_s, o_mem, w_in, layer, w_swa_up, w_ssm_glu, w_mem_up, *, tm=1024, tn=256):
    n = xn.shape[0]
    nj = D_MODEL // tn
    g0 = GATE_OFFSET // tn
    row = lambda i, j: (i, 0)
    return pl.pallas_call(
        _gate_merge_body,
        grid=(n // tm, nj),
        in_specs=[
            pl.BlockSpec((tm, D_MODEL), row),
            pl.BlockSpec((tm, Q_WIDTH), row),
            pl.BlockSpec((tm, SSM_WIDTH), row),
            pl.BlockSpec((tm, MEM_WIDTH), row),
            pl.BlockSpec((None, D_MODEL, tn), lambda i, j: (layer, 0, g0 + j)),
            pl.BlockSpec((None, D_MODEL, tn), lambda i, j: (layer, 0, g0 + j + nj)),
            pl.BlockSpec((None, D_MODEL, tn), lambda i, j: (layer, 0, g0 + j + 2 * nj)),
            pl.BlockSpec((Q_WIDTH, tn), lambda i, j: (0, j)),
            pl.BlockSpec((SSM_WIDTH, tn), lambda i, j: (0, j)),
            pl.BlockSpec((SSM_WIDTH, tn), lambda i, j: (0, j + nj)),
            pl.BlockSpec((MEM_WIDTH, tn), lambda i, j: (0, j)),
        ],
        out_specs=pl.BlockSpec((tm, tn), lambda i, j: (i, j)),
        out_shape=jax.ShapeDtypeStruct((n, D_MODEL), BF16),
        compiler_params=_params(("parallel", "arbitrary")),
        name="gate_merge",
    )(xn, o_swa, y_s, o_mem, w_in, w_in, w_in, w_swa_up, w_ssm_glu, w_ssm_glu, w_mem_up)


def _out_proj_body(h_ref, m_ref, w_ref, o_ref):
    o_ref[...] = h_ref[...] + jnp.dot(m_ref[...], w_ref[...].astype(BF16), preferred_element_type=F32)


def _out_proj(h, merged, w_out, layer, *, tm=1024, tn=512):
    n = h.shape[0]
    return pl.pallas_call(
        _out_proj_body,
        grid=(n // tm, D_MODEL // tn),
        in_specs=[
            pl.BlockSpec((tm, tn), lambda i, j: (i, j)),
            pl.BlockSpec((tm, D_MODEL), lambda i, j: (i, 0)),
            pl.BlockSpec((None, D_MODEL, tn), lambda i, j: (layer, 0, j)),
        ],
        out_specs=pl.BlockSpec((tm, tn), lambda i, j: (i, j)),
        out_shape=jax.ShapeDtypeStruct((n, D_MODEL), F32),
        compiler_params=_params(("parallel", "arbitrary")),
        name="out_proj",
    )(h, merged, w_out)


def kernel(x, mem, ffn1_norm, ffn1_w_in, ffn1_w_out, mix_norm, mem_norm, w_in, sinks, w_mem_kv, lam_re, lam_im, log_dt, b_re, b_im, c_re, c_im, d_skip, w_ssm_glu, w_swa_up, w_mem_up, w_out, ffn2_norm, ffn2_w_in, ffn2_w_out, final_norm):
    batch, seq = x.shape[0], x.shape[1]
    n = batch * seq
    h = x.reshape(n, D_MODEL)
    mem2 = mem.reshape(batch * N_MEM, D_MODEL)
    final_w = final_norm.reshape(1, D_MODEL)
    ssm_ops = jax.vmap(_ssm_operators)(lam_re, lam_im, log_dt, b_re, b_im, c_re, c_im, d_skip)
    for l in range(DEPTH):
        mix_w = mix_norm[l].reshape(1, D_MODEL)

        h = _ffn(h, ffn1_norm[l].reshape(1, D_MODEL), ffn1_w_in, ffn1_w_out, final_w, l, apply_final_norm=False)
        qkv, s_in, mq, xn = _mix_proj(h, mix_w, w_in, l)
        mem_kv = _norm_proj(mem2, mem_norm[l].reshape(1, D_MODEL), w_mem_kv, l, 0, 2 * MEM_WIDTH, BF16,
                            tm=batch * N_MEM)
        o_swa = _swa(qkv, sinks[l], batch, seq)
        o_mem = _mem_attn(mq, mem_kv, batch, seq)
        y_s = _ssm(s_in, ssm_ops, l, batch)
        merged = _gate_merge(xn, o_swa, y_s, o_mem, w_in, l, w_swa_up[l].astype(BF16),
                             w_ssm_glu[l].astype(BF16), w_mem_up[l].astype(BF16))
        h = _out_proj(h, merged, w_out, l)
        h = _ffn(h, ffn2_norm[l].reshape(1, D_MODEL), ffn2_w_in, ffn2_w_out, final_w, l,
                 apply_final_norm=(l == DEPTH - 1))
    return h.reshape(batch, seq, D_MODEL)
```

```python
import functools
import math

import jax
import jax.numpy as jnp
from jax import lax
from jax.experimental import pallas as pl
from jax.experimental.pallas import tpu as pltpu

D_MODEL = 2048
DEPTH = 4
N_MEM = 256
D_FF = 5632
RMS_EPS = 1e-5

WINDOW = 128
HEAD_DIM = 64
N_Q_HEADS = 16
N_KV_HEADS = 4
GQA_REP = N_Q_HEADS // N_KV_HEADS
Q_WIDTH = N_Q_HEADS * HEAD_DIM
KV_WIDTH = N_KV_HEADS * HEAD_DIM

SSM_WIDTH = 1024
SSM_GROUP = 16
SSM_GROUPS = SSM_WIDTH // SSM_GROUP
SSM_STATE = 64
SSM_CHUNK = 16
LANES = 128
SSM_LANE_TILES = SSM_WIDTH // LANES
SSM_TILE_GROUPS = LANES // SSM_GROUP

MEM_HEADS = 4
MEM_HEAD_DIM = 256
MEM_WIDTH = MEM_HEADS * MEM_HEAD_DIM

N_BRANCHES = 3
NEG_INF = -1e30

QKV_WIDTH = Q_WIDTH + 2 * KV_WIDTH
SSM_OFFSET = QKV_WIDTH
MEMQ_OFFSET = SSM_OFFSET + SSM_WIDTH
GATE_OFFSET = MEMQ_OFFSET + MEM_WIDTH

VMEM_LIMIT_BYTES = 56 * 1024 * 1024

BF16 = jnp.bfloat16
F32 = jnp.float32


def _params(semantics):
    return pltpu.CompilerParams(dimension_semantics=semantics, vmem_limit_bytes=VMEM_LIMIT_BYTES)


def _rms_normalize(x, w):
    ms = jnp.mean(x * x, axis=-1, keepdims=True)
    return (x * lax.rsqrt(ms + RMS_EPS)) * w


def _sigmoid(x):
    return 0.5 * jnp.tanh(0.5 * x) + 0.5


FFN_ROW_CHUNK = 512


class _RowTileIO:
    def __init__(self, h_hbm, out_hbm, hbuf, acc_ref, sems):
        self.h_hbm, self.out_hbm, self.hbuf, self.acc_ref, self.sems = h_hbm, out_hbm, hbuf, acc_ref, sems
        self.tm = acc_ref.shape[0]

    def _fetch(self, tile):
        return pltpu.make_async_copy(self.h_hbm.at[pl.ds(tile * self.tm, self.tm), :], self.hbuf, self.sems.at[0])

    def _write_back(self, tile):
        return pltpu.make_async_copy(self.acc_ref, self.out_hbm.at[pl.ds(tile * self.tm, self.tm), :],
                                     self.sems.at[1])

    def begin(self, xn_ref, nw_ref):
        i = pl.program_id(0)

        @pl.when(i == 0)
        def _():
            self._fetch(0).start()

        self._fetch(i).wait()
        xn_ref[...] = _rms_normalize(self.hbuf[...], nw_ref[...]).astype(BF16)

        @pl.when(i > 0)
        def _():
            self._write_back(i - 1).wait()

        self.acc_ref[...] = self.hbuf[...]

        @pl.when(i + 1 < pl.num_programs(0))
        def _():
            self._fetch(i + 1).start()

    def end(self):
        i = pl.program_id(0)
        self._write_back(i).start()

        @pl.when(i == pl.num_programs(0) - 1)
        def _():
            self._write_back(i).wait()


def _ffn_body(h_hbm, nw_ref, wg_ref, wu_ref, wo_ref, fw_ref, out_hbm, xn_ref, hbuf, acc_ref, sems, *,
              apply_final_norm):
    j = pl.program_id(1)
    io = _RowTileIO(h_hbm, out_hbm, hbuf, acc_ref, sems)

    @pl.when(j == 0)
    def _():
        io.begin(xn_ref, nw_ref)

    wg = wg_ref[...].astype(BF16)
    wu = wu_ref[...].astype(BF16)
    wo = wo_ref[...].astype(BF16)
    for r in range(acc_ref.shape[0] // FFN_ROW_CHUNK):
        rows = pl.ds(r * FFN_ROW_CHUNK, FFN_ROW_CHUNK)
        xn = xn_ref[rows, :]
        g = jnp.dot(xn, wg, preferred_element_type=F32)
        u = jnp.dot(xn, wu, preferred_element_type=F32)
        a = ((0.5 * g) * _sigmoid(g)) * u
        acc_ref[rows, :] += jnp.dot(a.astype(BF16), wo, preferred_element_type=F32)

    @pl.when(j == pl.num_programs(1) - 1)
    def _():
        if apply_final_norm:
            acc_ref[...] = _rms_normalize(acc_ref[...], fw_ref[...])
        io.end()


def _ffn(h, norm_w, w_in, w_out, final_w, layer, *, apply_final_norm, tm=1024, tf=512):
    n = h.shape[0]
    nf = D_FF // tf
    return pl.pallas_call(
        functools.partial(_ffn_body, apply_final_norm=apply_final_norm),
        grid=(n // tm, nf),
        in_specs=[
            pl.BlockSpec(memory_space=pl.ANY),
            pl.BlockSpec((1, D_MODEL), lambda i, j: (0, 0)),
            pl.BlockSpec((None, D_MODEL, tf), lambda i, j: (layer, 0, j)),
            pl.BlockSpec((None, D_MODEL, tf), lambda i, j: (layer, 0, j + nf)),
            pl.BlockSpec((None, tf, D_MODEL), lambda i, j: (layer, j, 0)),
            pl.BlockSpec((1, D_MODEL), lambda i, j: (0, 0)),
        ],
        out_specs=pl.BlockSpec(memory_space=pl.ANY),
        out_shape=jax.ShapeDtypeStruct((n, D_MODEL), F32),
        scratch_shapes=[pltpu.VMEM((tm, D_MODEL), BF16), pltpu.VMEM((tm, D_MODEL), F32),
                        pltpu.VMEM((tm, D_MODEL), F32), pltpu.SemaphoreType.DMA((2,))],
        compiler_params=_params(("arbitrary", "arbitrary")),
        name="ffn",
    )(h, norm_w, w_in, w_in, w_out, final_w)


def _mem_kv_body(m_ref, nw_ref, w_ref, o_ref, xn_ref):
    @pl.when(pl.program_id(1) == 0)
    def _():
        xn_ref[...] = _rms_normalize(m_ref[...], nw_ref[...]).astype(BF16)

    o_ref[...] = jnp.dot(xn_ref[...], w_ref[...].astype(BF16), preferred_element_type=F32).astype(BF16)


def _mem_kv(mem2, mem_norm, w_mem_kv, *, tn=512):
    rows = mem2.shape[0]
    width = 2 * MEM_WIDTH
    return pl.pallas_call(
        _mem_kv_body,
        grid=(DEPTH, width // tn),
        in_specs=[
            pl.BlockSpec((rows, D_MODEL), lambda l, j: (0, 0)),
            pl.BlockSpec((None, 1, D_MODEL), lambda l, j: (l, 0, 0)),
            pl.BlockSpec((None, D_MODEL, tn), lambda l, j: (l, 0, j)),
        ],
        out_specs=pl.BlockSpec((None, rows, tn), lambda l, j: (l, 0, j)),
        out_shape=jax.ShapeDtypeStruct((DEPTH, rows, width), BF16),
        scratch_shapes=[pltpu.VMEM((rows, D_MODEL), BF16)],
        compiler_params=_params(("arbitrary", "arbitrary")),
        name="mem_kv",
    )(mem2, mem_norm.reshape(DEPTH, 1, D_MODEL), w_mem_kv)


def _mix_proj_body(h_ref, nw_ref, w_ref, qkv_ref, s_ref, mq_ref, xn_ref, *, n_qkv, n_ssm):
    j = pl.program_id(1)

    @pl.when(j == 0)
    def _():
        xn_ref[...] = _rms_normalize(h_ref[...], nw_ref[...]).astype(BF16)

    def project(out_ref):
        out_ref[...] = jnp.dot(xn_ref[...], w_ref[...].astype(BF16), preferred_element_type=F32).astype(out_ref.dtype)

    @pl.when(j < n_qkv)
    def _():
        project(qkv_ref)

    @pl.when((j >= n_qkv) & (j < n_qkv + n_ssm))
    def _():
        project(s_ref)

    @pl.when(j >= n_qkv + n_ssm)
    def _():
        project(mq_ref)


def _mix_proj(h, norm_w, w_in, layer, *, tm=2048, tn=512):
    n = h.shape[0]
    n_qkv, n_ssm, n_mq = QKV_WIDTH // tn, SSM_WIDTH // tn, MEM_WIDTH // tn
    return pl.pallas_call(
        functools.partial(_mix_proj_body, n_qkv=n_qkv, n_ssm=n_ssm),
        grid=(n // tm, n_qkv + n_ssm + n_mq),
        in_specs=[
            pl.BlockSpec((tm, D_MODEL), lambda i, j: (i, 0), pipeline_mode=pl.Buffered(1)),
            pl.BlockSpec((1, D_MODEL), lambda i, j: (0, 0)),
            pl.BlockSpec((None, D_MODEL, tn), lambda i, j: (layer, 0, j)),
        ],
        out_specs=[
            pl.BlockSpec((tm, tn), lambda i, j: (i, jnp.minimum(j, n_qkv - 1))),
            pl.BlockSpec((tm, tn), lambda i, j: (i, jnp.clip(j - n_qkv, 0, n_ssm - 1))),
            pl.BlockSpec((tm, tn), lambda i, j: (i, jnp.clip(j - n_qkv - n_ssm, 0, n_mq - 1))),
            pl.BlockSpec((tm, D_MODEL), lambda i, j: (i, 0), pipeline_mode=pl.Buffered(1)),
        ],
        out_shape=[
            jax.ShapeDtypeStruct((n, QKV_WIDTH), BF16),
            jax.ShapeDtypeStruct((n, SSM_WIDTH), F32),
            jax.ShapeDtypeStruct((n, MEM_WIDTH), BF16),
            jax.ShapeDtypeStruct((n, D_MODEL), BF16),
        ],
        compiler_params=_params(("parallel", "arbitrary")),
        name="mix_proj",
    )(h, norm_w, w_in)


def _swa_body(sinks_ref, q_ref, kvc_ref, kvp_ref, o_ref, s_ref, p_ref, den_ref):
    blk = pl.program_id(1)
    q = q_ref[...]
    kvc = kvc_ref[...]
    kvp = kvp_ref[...]
    qi = lax.broadcasted_iota(jnp.int32, (WINDOW, 2 * WINDOW), 0)
    kj = lax.broadcasted_iota(jnp.int32, (WINDOW, 2 * WINDOW), 1)
    first_key = jnp.where(blk > 0, 0, WINDOW)
    valid = (kj > qi) & (kj <= qi + WINDOW) & (kj >= first_key)
    scale = HEAD_DIM ** -0.5
    vals = []
    for g in range(N_KV_HEADS):
        ks = slice(g * HEAD_DIM, (g + 1) * HEAD_DIM)
        vs = slice(KV_WIDTH + g * HEAD_DIM, KV_WIDTH + (g + 1) * HEAD_DIM)
        k = jnp.concatenate([kvp[:, ks], kvc[:, ks]], axis=0)
        vals.append(jnp.concatenate([kvp[:, vs], kvc[:, vs]], axis=0))
        for r in range(GQA_REP):
            h = g * GQA_REP + r
            qh = q[:, h * HEAD_DIM:(h + 1) * HEAD_DIM]
            s_ref[h] = lax.dot_general(qh, k, (((1,), (1,)), ((), ())), preferred_element_type=F32)
    for h in range(N_Q_HEADS):
        s = jnp.where(valid, s_ref[h] * scale, NEG_INF)
        sink = sinks_ref[h]
        m = jnp.maximum(jnp.max(s, axis=-1, keepdims=True), sink)
        p = jnp.exp(s - m)
        den_ref[h] = jnp.broadcast_to(jnp.sum(p, axis=-1, keepdims=True) + jnp.exp(sink - m), (WINDOW, HEAD_DIM))
        p_ref[h] = p.astype(BF16)
    for h in range(N_Q_HEADS):
        o = jnp.dot(p_ref[h], vals[h // GQA_REP], preferred_element_type=F32) / den_ref[h]
        o_ref[:, h * HEAD_DIM:(h + 1) * HEAD_DIM] = o.astype(BF16)


def _swa(qkv, sinks, batch, seq):
    nb = seq // WINDOW
    kv_col = Q_WIDTH // (2 * KV_WIDTH)
    return pl.pallas_call(
        _swa_body,
        grid=(batch, nb),
        in_specs=[
            pl.BlockSpec(memory_space=pltpu.SMEM),
            pl.BlockSpec((WINDOW, Q_WIDTH), lambda b, n: (b * nb + n, 0)),
            pl.BlockSpec((WINDOW, 2 * KV_WIDTH), lambda b, n: (b * nb + n, kv_col)),
            pl.BlockSpec((WINDOW, 2 * KV_WIDTH), lambda b, n: (b * nb + jnp.maximum(n - 1, 0), kv_col)),
        ],
        out_specs=pl.BlockSpec((WINDOW, Q_WIDTH), lambda b, n: (b * nb + n, 0)),
        out_shape=jax.ShapeDtypeStruct((batch * seq, Q_WIDTH), BF16),
        scratch_shapes=[pltpu.VMEM((N_Q_HEADS, WINDOW, 2 * WINDOW), F32),
                        pltpu.VMEM((N_Q_HEADS, WINDOW, 2 * WINDOW), BF16),
                        pltpu.VMEM((N_Q_HEADS, WINDOW, HEAD_DIM), F32)],
        compiler_params=_params(("parallel", "arbitrary")),
        name="swa",
    )(sinks, qkv, qkv, qkv)


def _mem_attn_body(q_ref, kv_ref, o_ref):
    scale = MEM_HEAD_DIM ** -0.5
    for h in range(MEM_HEADS):
        cs = slice(h * MEM_HEAD_DIM, (h + 1) * MEM_HEAD_DIM)
        vs = slice(MEM_WIDTH + h * MEM_HEAD_DIM, MEM_WIDTH + (h + 1) * MEM_HEAD_DIM)
        s = lax.dot_general(q_ref[:, cs], kv_ref[:, cs], (((1,), (1,)), ((), ())),
                            preferred_element_type=F32) * scale
        m = jnp.max(s, axis=-1, keepdims=True)
        p = jnp.exp(s - m)
        denom = jnp.sum(p, axis=-1, keepdims=True)
        o = jnp.dot(p.astype(BF16), kv_ref[:, vs], preferred_element_type=F32) / denom
        o_ref[:, cs] = o.astype(BF16)


def _mem_attn(mq, mem_kv, layer, batch, seq, *, tq=512):
    nq = seq // tq
    return pl.pallas_call(
        _mem_attn_body,
        grid=(batch, nq),
        in_specs=[
            pl.BlockSpec((tq, MEM_WIDTH), lambda b, i: (b * nq + i, 0)),
            pl.BlockSpec((None, N_MEM, 2 * MEM_WIDTH), lambda b, i: (layer, b, 0)),
        ],
        out_specs=pl.BlockSpec((tq, MEM_WIDTH), lambda b, i: (b * nq + i, 0)),
        out_shape=jax.ShapeDtypeStruct((batch * seq, MEM_WIDTH), BF16),
        compiler_params=_params(("parallel", "arbitrary")),
        name="mem_attn",
    )(mq, mem_kv)


def _ssm_operators(lam_re, lam_im, log_dt, b_re, b_im, c_re, c_im, d_skip):
    hp = lax.Precision.HIGHEST
    t_len, g_n, p_n, ch = SSM_CHUNK, SSM_GROUPS, SSM_STATE, SSM_GROUP
    lr = jnp.minimum(lam_re, -1e-4)
    li = lam_im
    dt = jnp.exp(log_dt)[:, None]
    mag = jnp.exp(lr * dt)
    ar = mag * jnp.cos(li * dt)
    ai = mag * jnp.sin(li * dt)
    nr, ni = ar - 1.0, ai
    den = lr * lr + li * li
    kr = (nr * lr + ni * li) / den
    ki = (ni * lr - nr * li) / den
    bbr = kr[..., None] * b_re - ki[..., None] * b_im
    bbi = kr[..., None] * b_im + ki[..., None] * b_re
    steps = jnp.arange(t_len + 1, dtype=F32)[None, :, None]
    pmag = jnp.exp(steps * (lr * dt)[:, None, :])
    ang = steps * (li * dt)[:, None, :]
    pr = pmag * jnp.cos(ang)
    pi = pmag * jnp.sin(ang)
    wr = pr[:, :t_len, :, None] * bbr[:, None] - pi[:, :t_len, :, None] * bbi[:, None]
    wi = pr[:, :t_len, :, None] * bbi[:, None] + pi[:, :t_len, :, None] * bbr[:, None]
    lagk = (jnp.einsum('gcp,gkpd->gkcd', c_re, wr, precision=hp)
            - jnp.einsum('gcp,gkpd->gkcd', c_im, wi, precision=hp))
    nt, gt = SSM_LANE_TILES, SSM_TILE_GROUPS
    lag_c = lagk.reshape(nt, gt, t_len, ch, ch).transpose(0, 2, 1, 4, 3).reshape(nt, t_len * LANES, ch)

    def pair(x, y):
        return jnp.concatenate([x, y], axis=-1)

    def per_step(w):
        return w.reshape(nt, gt, t_len, 2 * p_n).transpose(0, 2, 1, 3)[:, :, :, None, :]

    def per_channel(w):
        return w.reshape(nt, gt, ch, 2 * p_n)[:, None]

    back = jnp.arange(t_len - 1, -1, -1, dtype=F32)[None, :, None]
    bmag = jnp.exp(back * (lr * dt)[:, None, :])
    bang = back * (li * dt)[:, None, :]
    qr, qi = bmag * jnp.cos(bang), bmag * jnp.sin(bang)
    bbr_t, bbi_t = bbr.transpose(0, 2, 1), bbi.transpose(0, 2, 1)
    inp = (per_step(pair(qr, qr)) * per_channel(pair(bbr_t, bbi_t))
           + per_step(pair(-qi, qi)) * per_channel(pair(bbi_t, bbr_t))
           ).reshape(nt, t_len * LANES, 2 * p_n)
    pr1, pi1 = pr[:, 1:], pi[:, 1:]
    outp = (per_step(pair(pr1, pi1)) * per_channel(pair(c_re, -c_re))
            - per_step(pair(pi1, pr1)) * per_channel(pair(c_im, c_im))
            ).reshape(nt, t_len * LANES, 2 * p_n)
    a1 = jnp.concatenate([pr[:, t_len], pr[:, t_len]], axis=-1).reshape(nt, gt, 2 * p_n)
    a2 = jnp.concatenate([-pi[:, t_len], pi[:, t_len]], axis=-1).reshape(nt, gt, 2 * p_n)
    return lag_c, inp, outp, a1, a2, d_skip.reshape(1, SSM_WIDTH)


def _ssm_body(s_ref, lag_ref, inp_ref, outp_ref, a1_ref, a2_ref, d_ref, y_ref,
              u_ref, panel_ref, inpx_ref, outpx_ref, z_ref, zs_ref, sp_ref, *, batch):
    t_len, gt = SSM_CHUNK, SSM_TILE_GROUPS
    n_chunks = u_ref.shape[0]
    per_seq = n_chunks // batch
    flat = t_len * LANES

    for t in range(t_len):
        u_ref[:, t * LANES:(t + 1) * LANES] = s_ref[pl.ds(t, n_chunks, stride=t_len), :].astype(BF16)

    row_group = (lax.broadcasted_iota(jnp.int32, (flat, 1), 0) >> 4) & (gt - 1)
    col_group = lax.broadcasted_iota(jnp.int32, (1, LANES), 1) >> 4
    spread = (lax.broadcasted_iota(jnp.int32, (SSM_GROUP, LANES), 1) & (SSM_GROUP - 1)
              == lax.broadcasted_iota(jnp.int32, (SSM_GROUP, LANES), 0)).astype(BF16)
    lag = jnp.dot(lag_ref[...].astype(BF16), spread, preferred_element_type=F32)
    lag = jnp.where(row_group == col_group, lag, 0.0).astype(BF16)

    for r in range(t_len):
        k_left, k_right = t_len - 2 - r, t_len - 1 - r
        left = lag[k_left * LANES:(k_left + 1) * LANES] if k_left >= 0 else jnp.zeros((LANES, LANES), BF16)
        panel_ref[r * LANES:(r + 1) * LANES, :LANES] = left
        panel_ref[r * LANES:(r + 1) * LANES, LANES:] = lag[k_right * LANES:(k_right + 1) * LANES]

    inp = inp_ref[...]
    outp = outp_ref[...]
    for g in range(gt):
        inpx_ref[:, g * LANES:(g + 1) * LANES] = jnp.where(row_group == g, inp, 0.0).astype(BF16)
        outpx_ref[:, g * LANES:(g + 1) * LANES] = jnp.where(row_group == g, outp, 0.0).astype(BF16)

    z = jnp.dot(u_ref[...], inpx_ref[...], preferred_element_type=F32)
    for g in range(gt):
        zg = z[:, g * LANES:(g + 1) * LANES]
        z_ref[pl.ds(g, n_chunks, stride=gt), :] = zg
        zs_ref[pl.ds(g, n_chunks, stride=gt), :] = pltpu.roll(zg, SSM_STATE, axis=1)

    a1 = a1_ref[...]
    a2 = a2_ref[...]

    def step(c, carry):
        new = []
        for b in range(batch):
            v0, v1 = carry[b]
            row = pl.multiple_of((b * per_seq + c) * gt, gt)
            sp_ref[pl.ds(row, gt), :] = v0
            z0 = z_ref[pl.ds(row, gt), :]
            z1 = zs_ref[pl.ds(row, gt), :]
            new.append((a1 * v0 + a2 * v1 + z0, a1 * v1 - a2 * v0 + z1))
        return tuple(new)

    zero = jnp.zeros((gt, LANES), F32)
    lax.fori_loop(0, per_seq, step, tuple((zero, zero) for _ in range(batch)), unroll=4)

    sp = jnp.concatenate([sp_ref[pl.ds(g, n_chunks, stride=gt), :] for g in range(gt)], axis=1).astype(BF16)
    d2 = jnp.concatenate([d_ref[...], d_ref[...]], axis=1)
    for q in range(t_len // 2):
        cols = slice(2 * q * LANES, (2 * q + 2) * LANES)
        k_len = (2 * q + 2) * LANES
        y = jnp.dot(u_ref[:, :k_len], panel_ref[flat - k_len:, :], preferred_element_type=F32)
        y = y + lax.dot_general(sp, outpx_ref[cols, :], (((1,), (1,)), ((), ())), preferred_element_type=F32)
        y = y + d2 * u_ref[:, cols].astype(F32)
        y_ref[pl.ds(2 * q, n_chunks, stride=t_len), :] = y[:, :LANES]
        y_ref[pl.ds(2 * q + 1, n_chunks, stride=t_len), :] = y[:, LANES:]


def _ssm(s_in, ops, layer, batch):
    lag_c, inp, outp, a1, a2, d_row = ops
    n = s_in.shape[0]
    n_chunks = n // SSM_CHUNK
    flat = SSM_CHUNK * LANES
    return pl.pallas_call(
        functools.partial(_ssm_body, batch=batch),
        grid=(SSM_LANE_TILES,),
        in_specs=[
            pl.BlockSpec((n, LANES), lambda j: (0, j)),
            pl.BlockSpec((None, None, flat, SSM_GROUP), lambda j: (layer, j, 0, 0)),
            pl.BlockSpec((None, None, flat, 2 * SSM_STATE), lambda j: (layer, j, 0, 0)),
            pl.BlockSpec((None, None, flat, 2 * SSM_STATE), lambda j: (layer, j, 0, 0)),
            pl.BlockSpec((None, None, SSM_TILE_GROUPS, 2 * SSM_STATE), lambda j: (layer, j, 0, 0)),
            pl.BlockSpec((None, None, SSM_TILE_GROUPS, 2 * SSM_STATE), lambda j: (layer, j, 0, 0)),
            pl.BlockSpec((None, 1, LANES), lambda j: (layer, 0, j)),
        ],
        out_specs=pl.BlockSpec((n, LANES), lambda j: (0, j)),
        out_shape=jax.ShapeDtypeStruct((n, SSM_WIDTH), F32),
        scratch_shapes=[
            pltpu.VMEM((n_chunks, flat), BF16),
            pltpu.VMEM((flat, 2 * LANES), BF16),
            pltpu.VMEM((flat, SSM_TILE_GROUPS * 2 * SSM_STATE), BF16),
            pltpu.VMEM((flat, SSM_TILE_GROUPS * 2 * SSM_STATE), BF16),
            pltpu.VMEM((n_chunks * SSM_TILE_GROUPS, 2 * SSM_STATE), F32),
            pltpu.VMEM((n_chunks * SSM_TILE_GROUPS, 2 * SSM_STATE), F32),
            pltpu.VMEM((n_chunks * SSM_TILE_GROUPS, 2 * SSM_STATE), F32),
        ],
        compiler_params=_params(("parallel",)),
        name="ssm",
    )(s_in, lag_c, inp, outp, a1, a2, d_row)


MERGE_ROW_CHUNK = 256


def _gate_merge_body(xn_ref, swa_ref, ssm_ref, mem_ref, wg0_ref, wg1_ref, wg2_ref, wswa_ref,
                     wga_ref, wgb_ref, wmem_ref, o_ref):
    tn = o_ref.shape[1]
    half_k = D_MODEL // 2
    wg01 = jnp.concatenate([wg0_ref[...].astype(BF16), wg1_ref[...].astype(BF16)], axis=1)
    wg2 = wg2_ref[...].astype(BF16)
    wglu = jnp.concatenate([wga_ref[...].astype(BF16), wgb_ref[...].astype(BF16)], axis=1)
    wswa = wswa_ref[...].astype(BF16)
    wmem = wmem_ref[...].astype(BF16)
    for r in range(o_ref.shape[0] // MERGE_ROW_CHUNK):
        rows = pl.ds(r * MERGE_ROW_CHUNK, MERGE_ROW_CHUNK)
        xn = xn_ref[rows, :]
        ys = ssm_ref[rows, :].astype(BF16)
        y_swa = jnp.dot(swa_ref[rows, :], wswa, preferred_element_type=F32)
        y_mem = jnp.dot(mem_ref[rows, :], wmem, preferred_element_type=F32)
        glu = jnp.dot(ys, wglu, preferred_element_type=F32)
        y_ssm = glu[:, :tn] * _sigmoid(glu[:, tn:])
        g01 = jnp.dot(xn, wg01, preferred_element_type=F32)
        g2 = (jnp.dot(xn[:, :half_k], wg2[:half_k], preferred_element_type=F32)
              + jnp.dot(xn[:, half_k:], wg2[half_k:], preferred_element_type=F32))
        merged = _sigmoid(g01[:, :tn]) * y_swa + _sigmoid(g01[:, tn:]) * y_ssm + _sigmoid(g2) * y_mem
        o_ref[rows, :] = merged.astype(BF16)


def _gate_merge(xn, o_swa, y_s, o_mem, w_in, layer, w_swa_up, w_ssm_glu, w_mem_up, *, tm=1024, tn=256):
    n = xn.shape[0]
    nj = D_MODEL // tn
    g0 = GATE_OFFSET // tn
    row = lambda i, j: (i, 0)
    return pl.pallas_call(
        _gate_merge_body,
        grid=(n // tm, nj),
        in_specs=[
            pl.BlockSpec((tm, D_MODEL), row),
            pl.BlockSpec((tm, Q_WIDTH), row),
            pl.BlockSpec((tm, SSM_WIDTH), row),
            pl.BlockSpec((tm, MEM_WIDTH), row),
            pl.BlockSpec((None, D_MODEL, tn), lambda i, j: (layer, 0, g0 + j)),
            pl.BlockSpec((None, D_MODEL, tn), lambda i, j: (layer, 0, g0 + j + nj)),
            pl.BlockSpec((None, D_MODEL, tn), lambda i, j: (layer, 0, g0 + j + 2 * nj)),
            pl.BlockSpec((None, Q_WIDTH, tn), lambda i, j: (layer, 0, j)),
            pl.BlockSpec((None, SSM_WIDTH, tn), lambda i, j: (layer, 0, j)),
            pl.BlockSpec((None, SSM_WIDTH, tn), lambda i, j: (layer, 0, j + nj)),
            pl.BlockSpec((None, MEM_WIDTH, tn), lambda i, j: (layer, 0, j)),
        ],
        out_specs=pl.BlockSpec((tm, tn), lambda i, j: (i, j)),
        out_shape=jax.ShapeDtypeStruct((n, D_MODEL), BF16),
        compiler_params=_params(("parallel", "arbitrary")),
        name="gate_merge",
    )(xn, o_swa, y_s, o_mem, w_in, w_in, w_in, w_swa_up, w_ssm_glu, w_ssm_glu, w_mem_up)


def _out_proj_body(h_ref, m_ref, w_ref, o_ref, wb_ref):
    @pl.when(pl.program_id(0) == 0)
    def _():
        wb_ref[...] = w_ref[...].astype(BF16)

    half = D_MODEL // 2
    for c in range(2):
        cols = pl.ds(c * half, half)
        o_ref[:, cols] = h_ref[:, cols] + jnp.dot(m_ref[...], wb_ref[:, cols], preferred_element_type=F32)


def _out_proj(h, merged, w_out, layer, *, tm=512):
    n = h.shape[0]
    return pl.pallas_call(
        _out_proj_body,
        grid=(n // tm,),
        in_specs=[
            pl.BlockSpec((tm, D_MODEL), lambda i: (i, 0)),
            pl.BlockSpec((tm, D_MODEL), lambda i: (i, 0)),
            pl.BlockSpec((None, D_MODEL, D_MODEL), lambda i: (layer, 0, 0), pipeline_mode=pl.Buffered(1)),
        ],
        out_specs=pl.BlockSpec((tm, D_MODEL), lambda i: (i, 0)),
        out_shape=jax.ShapeDtypeStruct((n, D_MODEL), F32),
        scratch_shapes=[pltpu.VMEM((D_MODEL, D_MODEL), BF16)],
        compiler_params=_params(("arbitrary",)),
        name="out_proj",
    )(h, merged, w_out)


def kernel(x, mem, ffn1_norm, ffn1_w_in, ffn1_w_out, mix_norm, mem_norm, w_in, sinks, w_mem_kv, lam_re, lam_im, log_dt, b_re, b_im, c_re, c_im, d_skip, w_ssm_glu, w_swa_up, w_mem_up, w_out, ffn2_norm, ffn2_w_in, ffn2_w_out, final_norm):
    batch, seq = x.shape[0], x.shape[1]
    n = batch * seq
    h = x.reshape(n, D_MODEL)
    mem2 = mem.reshape(batch * N_MEM, D_MODEL)
    final_w = final_norm.reshape(1, D_MODEL)
    ssm_ops = jax.vmap(_ssm_operators)(lam_re, lam_im, log_dt, b_re, b_im, c_re, c_im, d_skip)
    mem_kv = _mem_kv(mem2, mem_norm, w_mem_kv)
    for l in range(DEPTH):
        mix_w = mix_norm[l].reshape(1, D_MODEL)

        h = _ffn(h, ffn1_norm[l].reshape(1, D_MODEL), ffn1_w_in, ffn1_w_out, final_w, l, apply_final_norm=False)
        qkv, s_in, mq, xn = _mix_proj(h, mix_w, w_in, l)
        o_swa = _swa(qkv, sinks[l], batch, seq)
        o_mem = _mem_attn(mq, mem_kv, l, batch, seq)
        y_s = _ssm(s_in, ssm_ops, l, batch)
        merged = _gate_merge(xn, o_swa, y_s, o_mem, w_in, l, w_swa_up, w_ssm_glu, w_mem_up)
        h = _out_proj(h, merged, w_out, l)
        h = _ffn(h, ffn2_norm[l].reshape(1, D_MODEL), ffn2_w_in, ffn2_w_out, final_w, l,
                 apply_final_norm=(l == DEPTH - 1))
    return h.reshape(batch, seq, D_MODEL)
```

```python
import functools
import math

import jax
import jax.numpy as jnp
from jax import lax
from jax.experimental import pallas as pl
from jax.experimental.pallas import tpu as pltpu

D_MODEL = 2048
DEPTH = 4
N_MEM = 256
D_FF = 5632
RMS_EPS = 1e-5

WINDOW = 128
HEAD_DIM = 64
N_Q_HEADS = 16
N_KV_HEADS = 4
GQA_REP = N_Q_HEADS // N_KV_HEADS
Q_WIDTH = N_Q_HEADS * HEAD_DIM
KV_WIDTH = N_KV_HEADS * HEAD_DIM

SSM_WIDTH = 1024
SSM_GROUP = 16
SSM_GROUPS = SSM_WIDTH // SSM_GROUP
SSM_STATE = 64
SSM_CHUNK = 16
LANES = 128
SSM_LANE_TILES = SSM_WIDTH // LANES
SSM_TILE_GROUPS = LANES // SSM_GROUP

MEM_HEADS = 4
MEM_HEAD_DIM = 256
MEM_WIDTH = MEM_HEADS * MEM_HEAD_DIM

N_BRANCHES = 3
NEG_INF = -1e30

QKV_WIDTH = Q_WIDTH + 2 * KV_WIDTH
SSM_OFFSET = QKV_WIDTH
MEMQ_OFFSET = SSM_OFFSET + SSM_WIDTH
GATE_OFFSET = MEMQ_OFFSET + MEM_WIDTH

VMEM_LIMIT_BYTES = 56 * 1024 * 1024

BF16 = jnp.bfloat16
F32 = jnp.float32


def _params(semantics):
    return pltpu.CompilerParams(dimension_semantics=semantics, vmem_limit_bytes=VMEM_LIMIT_BYTES)


def _rms_normalize(x, w):
    ms = jnp.mean(x * x, axis=-1, keepdims=True)
    return (x * lax.rsqrt(ms + RMS_EPS)) * w


def _sigmoid(x):
    return 0.5 * jnp.tanh(0.5 * x) + 0.5


FFN_ROW_CHUNK = 512


class _RowTileIO:
    def __init__(self, h_hbm, out_hbm, hbuf, acc_ref, sems):
        self.h_hbm, self.out_hbm, self.hbuf, self.acc_ref, self.sems = h_hbm, out_hbm, hbuf, acc_ref, sems
        self.tm = acc_ref.shape[0]

    def _fetch(self, tile):
        return pltpu.make_async_copy(self.h_hbm.at[pl.ds(tile * self.tm, self.tm), :], self.hbuf, self.sems.at[0])

    def _write_back(self, tile):
        return pltpu.make_async_copy(self.acc_ref, self.out_hbm.at[pl.ds(tile * self.tm, self.tm), :],
                                     self.sems.at[1])

    def begin(self, xn_ref, nw_ref):
        i = pl.program_id(0)

        @pl.when(i == 0)
        def _():
            self._fetch(0).start()

        self._fetch(i).wait()
        xn_ref[...] = _rms_normalize(self.hbuf[...], nw_ref[...]).astype(BF16)

        @pl.when(i > 0)
        def _():
            self._write_back(i - 1).wait()

        self.acc_ref[...] = self.hbuf[...]

        @pl.when(i + 1 < pl.num_programs(0))
        def _():
            self._fetch(i + 1).start()

    def end(self):
        i = pl.program_id(0)
        self._write_back(i).start()

        @pl.when(i == pl.num_programs(0) - 1)
        def _():
            self._write_back(i).wait()


def _ffn_body(h_hbm, nw_ref, wg_ref, wu_ref, wo_ref, fw_ref, out_hbm, xn_ref, hbuf, acc_ref, sems, *,
              apply_final_norm):
    j = pl.program_id(1)
    io = _RowTileIO(h_hbm, out_hbm, hbuf, acc_ref, sems)

    @pl.when(j == 0)
    def _():
        io.begin(xn_ref, nw_ref)

    wg = wg_ref[...].astype(BF16)
    wu = wu_ref[...].astype(BF16)
    wo = wo_ref[...].astype(BF16)
    for r in range(acc_ref.shape[0] // FFN_ROW_CHUNK):
        rows = pl.ds(r * FFN_ROW_CHUNK, FFN_ROW_CHUNK)
        xn = xn_ref[rows, :]
        g = jnp.dot(xn, wg, preferred_element_type=F32)
        u = jnp.dot(xn, wu, preferred_element_type=F32)
        a = ((0.5 * g) * _sigmoid(g)) * u
        acc_ref[rows, :] += jnp.dot(a.astype(BF16), wo, preferred_element_type=F32)

    @pl.when(j == pl.num_programs(1) - 1)
    def _():
        if apply_final_norm:
            acc_ref[...] = _rms_normalize(acc_ref[...], fw_ref[...])
        io.end()


def _ffn(h, norm_w, w_in, w_out, final_w, layer, *, apply_final_norm, tm=1024, tf=512):
    n = h.shape[0]
    nf = D_FF // tf
    return pl.pallas_call(
        functools.partial(_ffn_body, apply_final_norm=apply_final_norm),
        grid=(n // tm, nf),
        in_specs=[
            pl.BlockSpec(memory_space=pl.ANY),
            pl.BlockSpec((1, D_MODEL), lambda i, j: (0, 0)),
            pl.BlockSpec((None, D_MODEL, tf), lambda i, j: (layer, 0, j)),
            pl.BlockSpec((None, D_MODEL, tf), lambda i, j: (layer, 0, j + nf)),
            pl.BlockSpec((None, tf, D_MODEL), lambda i, j: (layer, j, 0)),
            pl.BlockSpec((1, D_MODEL), lambda i, j: (0, 0)),
        ],
        out_specs=pl.BlockSpec(memory_space=pl.ANY),
        out_shape=jax.ShapeDtypeStruct((n, D_MODEL), F32),
        scratch_shapes=[pltpu.VMEM((tm, D_MODEL), BF16), pltpu.VMEM((tm, D_MODEL), F32),
                        pltpu.VMEM((tm, D_MODEL), F32), pltpu.SemaphoreType.DMA((2,))],
        compiler_params=_params(("arbitrary", "arbitrary")),
        name="ffn",
    )(h, norm_w, w_in, w_in, w_out, final_w)


def _mem_kv_body(m_ref, nw_ref, w_ref, o_ref, xn_ref):
    @pl.when(pl.program_id(1) == 0)
    def _():
        xn_ref[...] = _rms_normalize(m_ref[...], nw_ref[...]).astype(BF16)

    o_ref[...] = jnp.dot(xn_ref[...], w_ref[...].astype(BF16), preferred_element_type=F32).astype(BF16)


def _mem_kv(mem2, mem_norm, w_mem_kv, *, tn=512):
    rows = mem2.shape[0]
    width = 2 * MEM_WIDTH
    return pl.pallas_call(
        _mem_kv_body,
        grid=(DEPTH, width // tn),
        in_specs=[
            pl.BlockSpec((rows, D_MODEL), lambda l, j: (0, 0)),
            pl.BlockSpec((None, 1, D_MODEL), lambda l, j: (l, 0, 0)),
            pl.BlockSpec((None, D_MODEL, tn), lambda l, j: (l, 0, j)),
        ],
        out_specs=pl.BlockSpec((None, rows, tn), lambda l, j: (l, 0, j)),
        out_shape=jax.ShapeDtypeStruct((DEPTH, rows, width), BF16),
        scratch_shapes=[pltpu.VMEM((rows, D_MODEL), BF16)],
        compiler_params=_params(("arbitrary", "arbitrary")),
        name="mem_kv",
    )(mem2, mem_norm.reshape(DEPTH, 1, D_MODEL), w_mem_kv)


def _mix_proj_body(h_hbm, nw_ref, w_ref, qkv_ref, s_ref, mq_ref, xn_ref, hbuf, sem, *, n_qkv, n_ssm):
    i, j = pl.program_id(0), pl.program_id(1)
    tm = hbuf.shape[0]

    def fetch(tile):
        return pltpu.make_async_copy(h_hbm.at[pl.ds(tile * tm, tm), :], hbuf, sem.at[0])

    @pl.when(j == 0)
    def _():
        @pl.when(i == 0)
        def _():
            fetch(0).start()

        fetch(i).wait()
        xn_ref[...] = _rms_normalize(hbuf[...], nw_ref[...]).astype(BF16)

        @pl.when(i + 1 < pl.num_programs(0))
        def _():
            fetch(i + 1).start()

    def project(out_ref):
        out_ref[...] = jnp.dot(xn_ref[...], w_ref[...].astype(BF16), preferred_element_type=F32).astype(out_ref.dtype)

    @pl.when(j < n_qkv)
    def _():
        project(qkv_ref)

    @pl.when((j >= n_qkv) & (j < n_qkv + n_ssm))
    def _():
        project(s_ref)

    @pl.when(j >= n_qkv + n_ssm)
    def _():
        project(mq_ref)


def _mix_proj(h, norm_w, w_in, layer, *, tm=2048, tn=512):
    n = h.shape[0]
    n_qkv, n_ssm, n_mq = QKV_WIDTH // tn, SSM_WIDTH // tn, MEM_WIDTH // tn
    return pl.pallas_call(
        functools.partial(_mix_proj_body, n_qkv=n_qkv, n_ssm=n_ssm),
        grid=(n // tm, n_qkv + n_ssm + n_mq),
        in_specs=[
            pl.BlockSpec(memory_space=pl.ANY),
            pl.BlockSpec((1, D_MODEL), lambda i, j: (0, 0)),
            pl.BlockSpec((None, D_MODEL, tn), lambda i, j: (layer, 0, j)),
        ],
        out_specs=[
            pl.BlockSpec((tm, tn), lambda i, j: (i, jnp.minimum(j, n_qkv - 1))),
            pl.BlockSpec((tm, tn), lambda i, j: (i, jnp.clip(j - n_qkv, 0, n_ssm - 1))),
            pl.BlockSpec((tm, tn), lambda i, j: (i, jnp.clip(j - n_qkv - n_ssm, 0, n_mq - 1))),
            pl.BlockSpec((tm, D_MODEL), lambda i, j: (i, 0), pipeline_mode=pl.Buffered(1)),
        ],
        out_shape=[
            jax.ShapeDtypeStruct((n, QKV_WIDTH), BF16),
            jax.ShapeDtypeStruct((n, SSM_WIDTH), F32),
            jax.ShapeDtypeStruct((n, MEM_WIDTH), BF16),
            jax.ShapeDtypeStruct((n, D_MODEL), BF16),
        ],
        scratch_shapes=[pltpu.VMEM((tm, D_MODEL), F32), pltpu.SemaphoreType.DMA((1,))],
        compiler_params=_params(("arbitrary", "arbitrary")),
        name="mix_proj",
    )(h, norm_w, w_in)


SWA_STEP_BLOCKS = 2


def _swa_body(sinks_ref, q_ref, kvc_ref, kvp_ref, o_ref, s_ref, p_ref, den_ref):
    step = pl.program_id(1)
    for t in range(SWA_STEP_BLOCKS):
        rows = slice(t * WINDOW, (t + 1) * WINDOW)
        if t == 0:
            kvp, first_key = kvp_ref[...], jnp.where(step > 0, 0, WINDOW)
        else:
            kvp, first_key = kvc_ref[(t - 1) * WINDOW:t * WINDOW, :], 0
        _swa_block(sinks_ref, q_ref[rows, :], kvc_ref[rows, :], kvp, first_key, o_ref, rows, s_ref, p_ref, den_ref)


def _swa_block(sinks_ref, q, kvc, kvp, first_key, o_ref, rows, s_ref, p_ref, den_ref):
    qi = lax.broadcasted_iota(jnp.int32, (WINDOW, 2 * WINDOW), 0)
    kj = lax.broadcasted_iota(jnp.int32, (WINDOW, 2 * WINDOW), 1)
    valid = (kj > qi) & (kj <= qi + WINDOW) & (kj >= first_key)
    scale = HEAD_DIM ** -0.5
    vals = []
    for g in range(N_KV_HEADS):
        ks = slice(g * HEAD_DIM, (g + 1) * HEAD_DIM)
        vs = slice(KV_WIDTH + g * HEAD_DIM, KV_WIDTH + (g + 1) * HEAD_DIM)
        k = jnp.concatenate([kvp[:, ks], kvc[:, ks]], axis=0)
        vals.append(jnp.concatenate([kvp[:, vs], kvc[:, vs]], axis=0))
        for r in range(GQA_REP):
            h = g * GQA_REP + r
            qh = q[:, h * HEAD_DIM:(h + 1) * HEAD_DIM]
            s_ref[h] = lax.dot_general(qh, k, (((1,), (1,)), ((), ())), preferred_element_type=F32)
    for h in range(N_Q_HEADS):
        s = jnp.where(valid, s_ref[h] * scale, NEG_INF)
        sink = sinks_ref[h]
        m = jnp.maximum(jnp.max(s, axis=-1, keepdims=True), sink)
        den_ref[h] = jnp.broadcast_to(jnp.exp(sink - m), (WINDOW, LANES))
        p_ref[h] = jnp.exp(s - m).astype(BF16)
    ones = jnp.ones((2 * WINDOW, LANES), BF16)
    for h in range(N_Q_HEADS):
        p = p_ref[h]
        den = jnp.dot(p, ones, preferred_element_type=F32) + den_ref[h]
        o = jnp.dot(p, vals[h // GQA_REP], preferred_element_type=F32) / den[:, :HEAD_DIM]
        o_ref[rows, h * HEAD_DIM:(h + 1) * HEAD_DIM] = o.astype(BF16)


def _swa(qkv, sinks, batch, seq):
    nb = seq // WINDOW
    assert seq % (SWA_STEP_BLOCKS * WINDOW) == 0, seq
    ns = nb // SWA_STEP_BLOCKS
    span = SWA_STEP_BLOCKS * WINDOW
    kv_col = Q_WIDTH // (2 * KV_WIDTH)
    return pl.pallas_call(
        _swa_body,
        grid=(batch, ns),
        in_specs=[
            pl.BlockSpec(memory_space=pltpu.SMEM),
            pl.BlockSpec((span, Q_WIDTH), lambda b, n: (b * ns + n, 0)),
            pl.BlockSpec((span, 2 * KV_WIDTH), lambda b, n: (b * ns + n, kv_col)),
            pl.BlockSpec((WINDOW, 2 * KV_WIDTH),
                         lambda b, n: (b * nb + jnp.maximum(n * SWA_STEP_BLOCKS - 1, 0), kv_col)),
        ],
        out_specs=pl.BlockSpec((span, Q_WIDTH), lambda b, n: (b * ns + n, 0)),
        out_shape=jax.ShapeDtypeStruct((batch * seq, Q_WIDTH), BF16),
        scratch_shapes=[pltpu.VMEM((N_Q_HEADS, WINDOW, 2 * WINDOW), F32),
                        pltpu.VMEM((N_Q_HEADS, WINDOW, 2 * WINDOW), BF16),
                        pltpu.VMEM((N_Q_HEADS, WINDOW, LANES), F32)],
        compiler_params=_params(("parallel", "arbitrary")),
        name="swa",
    )(sinks, qkv, qkv, qkv)


def _mem_attn_body(q_ref, kv_ref, o_ref):
    scale = MEM_HEAD_DIM ** -0.5
    for h in range(MEM_HEADS):
        cs = slice(h * MEM_HEAD_DIM, (h + 1) * MEM_HEAD_DIM)
        vs = slice(MEM_WIDTH + h * MEM_HEAD_DIM, MEM_WIDTH + (h + 1) * MEM_HEAD_DIM)
        s = lax.dot_general(q_ref[:, cs], kv_ref[:, cs], (((1,), (1,)), ((), ())),
                            preferred_element_type=F32) * scale
        m = jnp.max(s, axis=-1, keepdims=True)
        p = jnp.exp(s - m)
        denom = jnp.sum(p, axis=-1, keepdims=True)
        o = jnp.dot(p.astype(BF16), kv_ref[:, vs], preferred_element_type=F32) / denom
        o_ref[:, cs] = o.astype(BF16)


def _mem_attn(mq, mem_kv, layer, batch, seq, *, tq=512):
    nq = seq // tq
    return pl.pallas_call(
        _mem_attn_body,
        grid=(batch, nq),
        in_specs=[
            pl.BlockSpec((tq, MEM_WIDTH), lambda b, i: (b * nq + i, 0)),
            pl.BlockSpec((None, N_MEM, 2 * MEM_WIDTH), lambda b, i: (layer, b, 0)),
        ],
        out_specs=pl.BlockSpec((tq, MEM_WIDTH), lambda b, i: (b * nq + i, 0)),
        out_shape=jax.ShapeDtypeStruct((batch * seq, MEM_WIDTH), BF16),
        compiler_params=_params(("parallel", "arbitrary")),
        name="mem_attn",
    )(mq, mem_kv)


def _ssm_operators(lam_re, lam_im, log_dt, b_re, b_im, c_re, c_im, d_skip):
    hp = lax.Precision.HIGHEST
    t_len, g_n, p_n, ch = SSM_CHUNK, SSM_GROUPS, SSM_STATE, SSM_GROUP
    lr = jnp.minimum(lam_re, -1e-4)
    li = lam_im
    dt = jnp.exp(log_dt)[:, None]
    mag = jnp.exp(lr * dt)
    ar = mag * jnp.cos(li * dt)
    ai = mag * jnp.sin(li * dt)
    nr, ni = ar - 1.0, ai
    den = lr * lr + li * li
    kr = (nr * lr + ni * li) / den
    ki = (ni * lr - nr * li) / den
    bbr = kr[..., None] * b_re - ki[..., None] * b_im
    bbi = kr[..., None] * b_im + ki[..., None] * b_re
    steps = jnp.arange(t_len + 1, dtype=F32)[None, :, None]
    pmag = jnp.exp(steps * (lr * dt)[:, None, :])
    ang = steps * (li * dt)[:, None, :]
    pr = pmag * jnp.cos(ang)
    pi = pmag * jnp.sin(ang)
    wr = pr[:, :t_len, :, None] * bbr[:, None] - pi[:, :t_len, :, None] * bbi[:, None]
    wi = pr[:, :t_len, :, None] * bbi[:, None] + pi[:, :t_len, :, None] * bbr[:, None]
    lagk = (jnp.einsum('gcp,gkpd->gkcd', c_re, wr, precision=hp)
            - jnp.einsum('gcp,gkpd->gkcd', c_im, wi, precision=hp))
    nt, gt = SSM_LANE_TILES, SSM_TILE_GROUPS
    lag_c = lagk.reshape(nt, gt, t_len, ch, ch).transpose(0, 2, 1, 4, 3).reshape(nt, t_len * LANES, ch)

    def pair(x, y):
        return jnp.concatenate([x, y], axis=-1)

    def per_step(w):
        return w.reshape(nt, gt, t_len, 2 * p_n).transpose(0, 2, 1, 3)[:, :, :, None, :]

    def per_channel(w):
        return w.reshape(nt, gt, ch, 2 * p_n)[:, None]

    back = jnp.arange(t_len - 1, -1, -1, dtype=F32)[None, :, None]
    bmag = jnp.exp(back * (lr * dt)[:, None, :])
    bang = back * (li * dt)[:, None, :]
    qr, qi = bmag * jnp.cos(bang), bmag * jnp.sin(bang)
    bbr_t, bbi_t = bbr.transpose(0, 2, 1), bbi.transpose(0, 2, 1)
    inp = (per_step(pair(qr, qr)) * per_channel(pair(bbr_t, bbi_t))
           + per_step(pair(-qi, qi)) * per_channel(pair(bbi_t, bbr_t))
           ).reshape(nt, t_len * LANES, 2 * p_n)
    pr1, pi1 = pr[:, 1:], pi[:, 1:]
    outp = (per_step(pair(pr1, pi1)) * per_channel(pair(c_re, -c_re))
            - per_step(pair(pi1, pr1)) * per_channel(pair(c_im, c_im))
            ).reshape(nt, t_len * LANES, 2 * p_n)
    a1 = jnp.concatenate([pr[:, t_len], pr[:, t_len]], axis=-1).reshape(nt, gt, 2 * p_n)
    a2 = jnp.concatenate([-pi[:, t_len], pi[:, t_len]], axis=-1).reshape(nt, gt, 2 * p_n)
    return lag_c, inp, outp, a1, a2, d_skip.reshape(1, SSM_WIDTH)


def _ssm_body(s_ref, lag_ref, inp_ref, outp_ref, a1_ref, a2_ref, d_ref, y_ref,
              u_ref, panel_ref, inpx_ref, outpx_ref, z_ref, zs_ref, sp_ref, *, batch):
    t_len, gt = SSM_CHUNK, SSM_TILE_GROUPS
    n_chunks = u_ref.shape[0]
    per_seq = n_chunks // batch
    flat = t_len * LANES

    for t in range(t_len):
        u_ref[:, t * LANES:(t + 1) * LANES] = s_ref[pl.ds(t, n_chunks, stride=t_len), :].astype(BF16)

    row_group = (lax.broadcasted_iota(jnp.int32, (flat, 1), 0) >> 4) & (gt - 1)
    col_group = lax.broadcasted_iota(jnp.int32, (1, LANES), 1) >> 4
    spread = (lax.broadcasted_iota(jnp.int32, (SSM_GROUP, LANES), 1) & (SSM_GROUP - 1)
              == lax.broadcasted_iota(jnp.int32, (SSM_GROUP, LANES), 0)).astype(BF16)
    lag = jnp.dot(lag_ref[...].astype(BF16), spread, preferred_element_type=F32)
    lag = jnp.where(row_group == col_group, lag, 0.0).astype(BF16)

    for r in range(t_len):
        k_left, k_right = t_len - 2 - r, t_len - 1 - r
        left = lag[k_left * LANES:(k_left + 1) * LANES] if k_left >= 0 else jnp.zeros((LANES, LANES), BF16)
        panel_ref[r * LANES:(r + 1) * LANES, :LANES] = left
        panel_ref[r * LANES:(r + 1) * LANES, LANES:] = lag[k_right * LANES:(k_right + 1) * LANES]

    inp = inp_ref[...]
    outp = outp_ref[...]
    for g in range(gt):
        inpx_ref[:, g * LANES:(g + 1) * LANES] = jnp.where(row_group == g, inp, 0.0).astype(BF16)
        outpx_ref[:, g * LANES:(g + 1) * LANES] = jnp.where(row_group == g, outp, 0.0).astype(BF16)

    z = jnp.dot(u_ref[...], inpx_ref[...], preferred_element_type=F32)
    for g in range(gt):
        zg = z[:, g * LANES:(g + 1) * LANES]
        z_ref[pl.ds(g, n_chunks, stride=gt), :] = zg
        zs_ref[pl.ds(g, n_chunks, stride=gt), :] = pltpu.roll(zg, SSM_STATE, axis=1)

    a1 = a1_ref[...]
    a2 = a2_ref[...]

    def step(c, carry):
        new = []
        for b in range(batch):
            v0, v1 = carry[b]
            row = pl.multiple_of((b * per_seq + c) * gt, gt)
            sp_ref[pl.ds(row, gt), :] = v0
            z0 = z_ref[pl.ds(row, gt), :]
            z1 = zs_ref[pl.ds(row, gt), :]
            new.append((a1 * v0 + a2 * v1 + z0, a1 * v1 - a2 * v0 + z1))
        return tuple(new)

    zero = jnp.zeros((gt, LANES), F32)
    lax.fori_loop(0, per_seq, step, tuple((zero, zero) for _ in range(batch)), unroll=4)

    sp = jnp.concatenate([sp_ref[pl.ds(g, n_chunks, stride=gt), :] for g in range(gt)], axis=1).astype(BF16)
    d2 = jnp.concatenate([d_ref[...], d_ref[...]], axis=1)
    for q in range(t_len // 2):
        cols = slice(2 * q * LANES, (2 * q + 2) * LANES)
        k_len = (2 * q + 2) * LANES
        y = jnp.dot(u_ref[:, :k_len], panel_ref[flat - k_len:, :], preferred_element_type=F32)
        y = y + lax.dot_general(sp, outpx_ref[cols, :], (((1,), (1,)), ((), ())), preferred_element_type=F32)
        y = y + d2 * u_ref[:, cols].astype(F32)
        y_ref[pl.ds(2 * q, n_chunks, stride=t_len), :] = y[:, :LANES]
        y_ref[pl.ds(2 * q + 1, n_chunks, stride=t_len), :] = y[:, LANES:]


def _ssm(s_in, ops, layer, batch):
    lag_c, inp, outp, a1, a2, d_row = ops
    n = s_in.shape[0]
    n_chunks = n // SSM_CHUNK
    flat = SSM_CHUNK * LANES
    return pl.pallas_call(
        functools.partial(_ssm_body, batch=batch),
        grid=(SSM_LANE_TILES,),
        in_specs=[
            pl.BlockSpec((n, LANES), lambda j: (0, j)),
            pl.BlockSpec((None, None, flat, SSM_GROUP), lambda j: (layer, j, 0, 0)),
            pl.BlockSpec((None, None, flat, 2 * SSM_STATE), lambda j: (layer, j, 0, 0)),
            pl.BlockSpec((None, None, flat, 2 * SSM_STATE), lambda j: (layer, j, 0, 0)),
            pl.BlockSpec((None, None, SSM_TILE_GROUPS, 2 * SSM_STATE), lambda j: (layer, j, 0, 0)),
            pl.BlockSpec((None, None, SSM_TILE_GROUPS, 2 * SSM_STATE), lambda j: (layer, j, 0, 0)),
            pl.BlockSpec((None, 1, LANES), lambda j: (layer, 0, j)),
        ],
        out_specs=pl.BlockSpec((n, LANES), lambda j: (0, j)),
        out_shape=jax.ShapeDtypeStruct((n, SSM_WIDTH), F32),
        scratch_shapes=[
            pltpu.VMEM((n_chunks, flat), BF16),
            pltpu.VMEM((flat, 2 * LANES), BF16),
            pltpu.VMEM((flat, SSM_TILE_GROUPS * 2 * SSM_STATE), BF16),
            pltpu.VMEM((flat, SSM_TILE_GROUPS * 2 * SSM_STATE), BF16),
            pltpu.VMEM((n_chunks * SSM_TILE_GROUPS, 2 * SSM_STATE), F32),
            pltpu.VMEM((n_chunks * SSM_TILE_GROUPS, 2 * SSM_STATE), F32),
            pltpu.VMEM((n_chunks * SSM_TILE_GROUPS, 2 * SSM_STATE), F32),
        ],
        compiler_params=_params(("parallel",)),
        name="ssm",
    )(s_in, lag_c, inp, outp, a1, a2, d_row)


MERGE_ROW_CHUNK = 256


def _gate_merge_body(xn_ref, swa_ref, ssm_ref, mem_ref, wg0_ref, wg1_ref, wg2_ref, wswa_ref,
                     wga_ref, wgb_ref, wmem_ref, o_ref):
    tn = o_ref.shape[1]
    half_k = D_MODEL // 2
    wg01 = jnp.concatenate([wg0_ref[...].astype(BF16), wg1_ref[...].astype(BF16)], axis=1)
    wg2 = wg2_ref[...].astype(BF16)
    wglu = jnp.concatenate([wga_ref[...].astype(BF16), wgb_ref[...].astype(BF16)], axis=1)
    wswa = wswa_ref[...].astype(BF16)
    wmem = wmem_ref[...].astype(BF16)
    for r in range(o_ref.shape[0] // MERGE_ROW_CHUNK):
        rows = pl.ds(r * MERGE_ROW_CHUNK, MERGE_ROW_CHUNK)
        xn = xn_ref[rows, :]
        ys = ssm_ref[rows, :].astype(BF16)
        y_swa = jnp.dot(swa_ref[rows, :], wswa, preferred_element_type=F32)
        y_mem = jnp.dot(mem_ref[rows, :], wmem, preferred_element_type=F32)
        glu = jnp.dot(ys, wglu, preferred_element_type=F32)
        y_ssm = glu[:, :tn] * _sigmoid(glu[:, tn:])
        g01 = jnp.dot(xn, wg01, preferred_element_type=F32)
        g2 = (jnp.dot(xn[:, :half_k], wg2[:half_k], preferred_element_type=F32)
              + jnp.dot(xn[:, half_k:], wg2[half_k:], preferred_element_type=F32))
        merged = _sigmoid(g01[:, :tn]) * y_swa + _sigmoid(g01[:, tn:]) * y_ssm + _sigmoid(g2) * y_mem
        o_ref[rows, :] = merged.astype(BF16)


def _gate_merge(xn, o_swa, y_s, o_mem, w_in, layer, w_swa_up, w_ssm_glu, w_mem_up, *, tm=1024, tn=256):
    n = xn.shape[0]
    nj = D_MODEL // tn
    g0 = GATE_OFFSET // tn
    row = lambda i, j: (i, 0)
    return pl.pallas_call(
        _gate_merge_body,
        grid=(n // tm, nj),
        in_specs=[
            pl.BlockSpec((tm, D_MODEL), row),
            pl.BlockSpec((tm, Q_WIDTH), row),
            pl.BlockSpec((tm, SSM_WIDTH), row),
            pl.BlockSpec((tm, MEM_WIDTH), row),
            pl.BlockSpec((None, D_MODEL, tn), lambda i, j: (layer, 0, g0 + j)),
            pl.BlockSpec((None, D_MODEL, tn), lambda i, j: (layer, 0, g0 + j + nj)),
            pl.BlockSpec((None, D_MODEL, tn), lambda i, j: (layer, 0, g0 + j + 2 * nj)),
            pl.BlockSpec((None, Q_WIDTH, tn), lambda i, j: (layer, 0, j)),
            pl.BlockSpec((None, SSM_WIDTH, tn), lambda i, j: (layer, 0, j)),
            pl.BlockSpec((None, SSM_WIDTH, tn), lambda i, j: (layer, 0, j + nj)),
            pl.BlockSpec((None, MEM_WIDTH, tn), lambda i, j: (layer, 0, j)),
        ],
        out_specs=pl.BlockSpec((tm, tn), lambda i, j: (i, j)),
        out_shape=jax.ShapeDtypeStruct((n, D_MODEL), BF16),
        compiler_params=_params(("parallel", "arbitrary")),
        name="gate_merge",
    )(xn, o_swa, y_s, o_mem, w_in, w_in, w_in, w_swa_up, w_ssm_glu, w_ssm_glu, w_mem_up)


def _out_proj_body(h_ref, m_ref, w_ref, o_ref, wb_ref):
    @pl.when(pl.program_id(0) == 0)
    def _():
        wb_ref[...] = w_ref[...].astype(BF16)

    half = D_MODEL // 2
    for c in range(2):
        cols = pl.ds(c * half, half)
        o_ref[:, cols] = h_ref[:, cols] + jnp.dot(m_ref[...], wb_ref[:, cols], preferred_element_type=F32)


def _out_proj(h, merged, w_out, layer, *, tm=512):
    n = h.shape[0]
    return pl.pallas_call(
        _out_proj_body,
        grid=(n // tm,),
        in_specs=[
            pl.BlockSpec((tm, D_MODEL), lambda i: (i, 0)),
            pl.BlockSpec((tm, D_MODEL), lambda i: (i, 0)),
            pl.BlockSpec((None, D_MODEL, D_MODEL), lambda i: (layer, 0, 0), pipeline_mode=pl.Buffered(1)),
        ],
        out_specs=pl.BlockSpec((tm, D_MODEL), lambda i: (i, 0)),
        out_shape=jax.ShapeDtypeStruct((n, D_MODEL), F32),
        scratch_shapes=[pltpu.VMEM((D_MODEL, D_MODEL), BF16)],
        compiler_params=_params(("arbitrary",)),
        name="out_proj",
    )(h, merged, w_out)


def kernel(x, mem, ffn1_norm, ffn1_w_in, ffn1_w_out, mix_norm, mem_norm, w_in, sinks, w_mem_kv, lam_re, lam_im, log_dt, b_re, b_im, c_re, c_im, d_skip, w_ssm_glu, w_swa_up, w_mem_up, w_out, ffn2_norm, ffn2_w_in, ffn2_w_out, final_norm):
    batch, seq = x.shape[0], x.shape[1]
    n = batch * seq
    h = x.reshape(n, D_MODEL)
    mem2 = mem.reshape(batch * N_MEM, D_MODEL)
    final_w = final_norm.reshape(1, D_MODEL)
    ssm_ops = jax.vmap(_ssm_operators)(lam_re, lam_im, log_dt, b_re, b_im, c_re, c_im, d_skip)
    mem_kv = _mem_kv(mem2, mem_norm, w_mem_kv)
    for l in range(DEPTH):
        mix_w = mix_norm[l].reshape(1, D_MODEL)

        h = _ffn(h, ffn1_norm[l].reshape(1, D_MODEL), ffn1_w_in, ffn1_w_out, final_w, l, apply_final_norm=False)
        qkv, s_in, mq, xn = _mix_proj(h, mix_w, w_in, l)
        o_swa = _swa(qkv, sinks[l], batch, seq)
        o_mem = _mem_attn(mq, mem_kv, l, batch, seq)
        y_s = _ssm(s_in, ssm_ops, l, batch)
        merged = _gate_merge(xn, o_swa, y_s, o_mem, w_in, l, w_swa_up, w_ssm_glu, w_mem_up)
        h = _out_proj(h, merged, w_out, l)
        h = _ffn(h, ffn2_norm[l].reshape(1, D_MODEL), ffn2_w_in, ffn2_w_out, final_w, l,
                 apply_final_norm=(l == DEPTH - 1))
    return h.reshape(batch, seq, D_MODEL)
```

```python
import functools
import math

import jax
import jax.numpy as jnp
from jax import lax
from jax.experimental import pallas as pl
from jax.experimental.pallas import tpu as pltpu

D_MODEL = 2048
DEPTH = 4
N_MEM = 256
D_FF = 5632
RMS_EPS = 1e-5

WINDOW = 128
HEAD_DIM = 64
N_Q_HEADS = 16
N_KV_HEADS = 4
GQA_REP = N_Q_HEADS // N_KV_HEADS
Q_WIDTH = N_Q_HEADS * HEAD_DIM
KV_WIDTH = N_KV_HEADS * HEAD_DIM

SSM_WIDTH = 1024
SSM_GROUP = 16
SSM_GROUPS = SSM_WIDTH // SSM_GROUP
SSM_STATE = 64
SSM_CHUNK = 16
LANES = 128
SSM_LANE_TILES = SSM_WIDTH // LANES
SSM_TILE_GROUPS = LANES // SSM_GROUP

MEM_HEADS = 4
MEM_HEAD_DIM = 256
MEM_WIDTH = MEM_HEADS * MEM_HEAD_DIM

N_BRANCHES = 3
NEG_INF = -1e30

QKV_WIDTH = Q_WIDTH + 2 * KV_WIDTH
SSM_OFFSET = QKV_WIDTH
MEMQ_OFFSET = SSM_OFFSET + SSM_WIDTH
GATE_OFFSET = MEMQ_OFFSET + MEM_WIDTH

VMEM_LIMIT_BYTES = 56 * 1024 * 1024

BF16 = jnp.bfloat16
F32 = jnp.float32


def _params(semantics):
    return pltpu.CompilerParams(dimension_semantics=semantics, vmem_limit_bytes=VMEM_LIMIT_BYTES)


def _rms_normalize(x, w):
    ms = jnp.mean(x * x, axis=-1, keepdims=True)
    return (x * lax.rsqrt(ms + RMS_EPS)) * w


def _sigmoid(x):
    return 0.5 * jnp.tanh(0.5 * x) + 0.5


FFN_ROW_CHUNK = 512


class _RowTileIO:
    def __init__(self, h_hbm, out_hbm, hbuf, acc_ref, sems):
        self.h_hbm, self.out_hbm, self.hbuf, self.acc_ref, self.sems = h_hbm, out_hbm, hbuf, acc_ref, sems
        self.tm = acc_ref.shape[0]

    def _fetch(self, tile):
        return pltpu.make_async_copy(self.h_hbm.at[pl.ds(tile * self.tm, self.tm), :], self.hbuf, self.sems.at[0])

    def _write_back(self, tile):
        return pltpu.make_async_copy(self.acc_ref, self.out_hbm.at[pl.ds(tile * self.tm, self.tm), :],
                                     self.sems.at[1])

    def begin(self, xn_ref, nw_ref):
        i = pl.program_id(0)

        @pl.when(i == 0)
        def _():
            self._fetch(0).start()

        self._fetch(i).wait()
        xn_ref[...] = _rms_normalize(self.hbuf[...], nw_ref[...]).astype(BF16)

        @pl.when(i > 0)
        def _():
            self._write_back(i - 1).wait()

        self.acc_ref[...] = self.hbuf[...]

        @pl.when(i + 1 < pl.num_programs(0))
        def _():
            self._fetch(i + 1).start()

    def end(self):
        i = pl.program_id(0)
        self._write_back(i).start()

        @pl.when(i == pl.num_programs(0) - 1)
        def _():
            self._write_back(i).wait()


def _ffn_body(h_hbm, nw_ref, wg_ref, wu_ref, wo_ref, fw_ref, out_hbm, xn_ref, hbuf, acc_ref, sems, *,
              apply_final_norm):
    j = pl.program_id(1)
    io = _RowTileIO(h_hbm, out_hbm, hbuf, acc_ref, sems)

    @pl.when(j == 0)
    def _():
        io.begin(xn_ref, nw_ref)

    wg = wg_ref[...].astype(BF16)
    wu = wu_ref[...].astype(BF16)
    wo = wo_ref[...].astype(BF16)
    for r in range(acc_ref.shape[0] // FFN_ROW_CHUNK):
        rows = pl.ds(r * FFN_ROW_CHUNK, FFN_ROW_CHUNK)
        xn = xn_ref[rows, :]
        g = jnp.dot(xn, wg, preferred_element_type=F32)
        u = jnp.dot(xn, wu, preferred_element_type=F32)
        a = ((0.5 * g) * _sigmoid(g)) * u
        acc_ref[rows, :] += jnp.dot(a.astype(BF16), wo, preferred_element_type=F32)

    @pl.when(j == pl.num_programs(1) - 1)
    def _():
        if apply_final_norm:
            acc_ref[...] = _rms_normalize(acc_ref[...], fw_ref[...])
        io.end()


def _ffn(h, norm_w, w_in, w_out, final_w, layer, *, apply_final_norm, tm=1024, tf=512):
    n = h.shape[0]
    nf = D_FF // tf
    return pl.pallas_call(
        functools.partial(_ffn_body, apply_final_norm=apply_final_norm),
        grid=(n // tm, nf),
        in_specs=[
            pl.BlockSpec(memory_space=pl.ANY),
            pl.BlockSpec((1, D_MODEL), lambda i, j: (0, 0)),
            pl.BlockSpec((None, D_MODEL, tf), lambda i, j: (layer, 0, j)),
            pl.BlockSpec((None, D_MODEL, tf), lambda i, j: (layer, 0, j + nf)),
            pl.BlockSpec((None, tf, D_MODEL), lambda i, j: (layer, j, 0)),
            pl.BlockSpec((1, D_MODEL), lambda i, j: (0, 0)),
        ],
        out_specs=pl.BlockSpec(memory_space=pl.ANY),
        out_shape=jax.ShapeDtypeStruct((n, D_MODEL), F32),
        scratch_shapes=[pltpu.VMEM((tm, D_MODEL), BF16), pltpu.VMEM((tm, D_MODEL), F32),
                        pltpu.VMEM((tm, D_MODEL), F32), pltpu.SemaphoreType.DMA((2,))],
        compiler_params=_params(("arbitrary", "arbitrary")),
        name="ffn",
    )(h, norm_w, w_in, w_in, w_out, final_w)


def _mem_kv_body(m_ref, nw_ref, w_ref, o_ref, xn_ref):
    @pl.when(pl.program_id(1) == 0)
    def _():
        xn_ref[...] = _rms_normalize(m_ref[...], nw_ref[...]).astype(BF16)

    o_ref[...] = jnp.dot(xn_ref[...], w_ref[...].astype(BF16), preferred_element_type=F32).astype(BF16)


def _mem_kv(mem2, mem_norm, w_mem_kv, *, tn=512):
    rows = mem2.shape[0]
    width = 2 * MEM_WIDTH
    return pl.pallas_call(
        _mem_kv_body,
        grid=(DEPTH, width // tn),
        in_specs=[
            pl.BlockSpec((rows, D_MODEL), lambda l, j: (0, 0)),
            pl.BlockSpec((None, 1, D_MODEL), lambda l, j: (l, 0, 0)),
            pl.BlockSpec((None, D_MODEL, tn), lambda l, j: (l, 0, j)),
        ],
        out_specs=pl.BlockSpec((None, rows, tn), lambda l, j: (l, 0, j)),
        out_shape=jax.ShapeDtypeStruct((DEPTH, rows, width), BF16),
        scratch_shapes=[pltpu.VMEM((rows, D_MODEL), BF16)],
        compiler_params=_params(("arbitrary", "arbitrary")),
        name="mem_kv",
    )(mem2, mem_norm.reshape(DEPTH, 1, D_MODEL), w_mem_kv)


MIX_ROW_CHUNK = 512


def _mix_proj_body(h_hbm, nw_ref, w_ref, qkv_ref, s_ref, mq_ref, xn_ref, hbuf, sem, *, n_qkv, n_ssm):
    i, j = pl.program_id(0), pl.program_id(1)
    tm = hbuf.shape[0]

    def fetch(tile):
        return pltpu.make_async_copy(h_hbm.at[pl.ds(tile * tm, tm), :], hbuf, sem.at[0])

    @pl.when(j == 0)
    def _():
        @pl.when(i == 0)
        def _():
            fetch(0).start()

        fetch(i).wait()
        w = w_ref[...].astype(BF16)
        for r in range(tm // MIX_ROW_CHUNK):
            rows = pl.ds(r * MIX_ROW_CHUNK, MIX_ROW_CHUNK)
            xn = _rms_normalize(hbuf[rows, :], nw_ref[...]).astype(BF16)
            xn_ref[rows, :] = xn
            qkv_ref[rows, :] = jnp.dot(xn, w, preferred_element_type=F32).astype(BF16)

        @pl.when(i + 1 < pl.num_programs(0))
        def _():
            fetch(i + 1).start()

    def project(out_ref):
        out_ref[...] = jnp.dot(xn_ref[...], w_ref[...].astype(BF16), preferred_element_type=F32).astype(out_ref.dtype)

    @pl.when((j > 0) & (j < n_qkv))
    def _():
        project(qkv_ref)

    @pl.when((j >= n_qkv) & (j < n_qkv + n_ssm))
    def _():
        project(s_ref)

    @pl.when(j >= n_qkv + n_ssm)
    def _():
        project(mq_ref)


def _mix_proj(h, norm_w, w_in, layer, *, tm=2048, tn=512):
    n = h.shape[0]
    n_qkv, n_ssm, n_mq = QKV_WIDTH // tn, SSM_WIDTH // tn, MEM_WIDTH // tn
    return pl.pallas_call(
        functools.partial(_mix_proj_body, n_qkv=n_qkv, n_ssm=n_ssm),
        grid=(n // tm, n_qkv + n_ssm + n_mq),
        in_specs=[
            pl.BlockSpec(memory_space=pl.ANY),
            pl.BlockSpec((1, D_MODEL), lambda i, j: (0, 0)),
            pl.BlockSpec((None, D_MODEL, tn), lambda i, j: (layer, 0, j)),
        ],
        out_specs=[
            pl.BlockSpec((tm, tn), lambda i, j: (i, jnp.minimum(j, n_qkv - 1))),
            pl.BlockSpec((tm, tn), lambda i, j: (i, jnp.clip(j - n_qkv, 0, n_ssm - 1))),
            pl.BlockSpec((tm, tn), lambda i, j: (i, jnp.clip(j - n_qkv - n_ssm, 0, n_mq - 1))),
            pl.BlockSpec((tm, D_MODEL), lambda i, j: (i, 0), pipeline_mode=pl.Buffered(1)),
        ],
        out_shape=[
            jax.ShapeDtypeStruct((n, QKV_WIDTH), BF16),
            jax.ShapeDtypeStruct((n, SSM_WIDTH), F32),
            jax.ShapeDtypeStruct((n, MEM_WIDTH), BF16),
            jax.ShapeDtypeStruct((n, D_MODEL), BF16),
        ],
        scratch_shapes=[pltpu.VMEM((tm, D_MODEL), F32), pltpu.SemaphoreType.DMA((1,))],
        compiler_params=_params(("arbitrary", "arbitrary")),
        name="mix_proj",
    )(h, norm_w, w_in)


SWA_STEP_BLOCKS = 2


def _swa_body(sinks_ref, q_ref, kvc_ref, kvp_ref, o_ref, s_ref, p_ref, den_ref):
    step = pl.program_id(1)
    for t in range(SWA_STEP_BLOCKS):
        rows = slice(t * WINDOW, (t + 1) * WINDOW)
        if t == 0:
            kvp, first_key = kvp_ref[...], jnp.where(step > 0, 0, WINDOW)
        else:
            kvp, first_key = kvc_ref[(t - 1) * WINDOW:t * WINDOW, :], 0
        _swa_block(sinks_ref, q_ref[rows, :], kvc_ref[rows, :], kvp, first_key, o_ref, rows, s_ref, p_ref, den_ref)


def _swa_block(sinks_ref, q, kvc, kvp, first_key, o_ref, rows, s_ref, p_ref, den_ref):
    qi = lax.broadcasted_iota(jnp.int32, (WINDOW, 2 * WINDOW), 0)
    kj = lax.broadcasted_iota(jnp.int32, (WINDOW, 2 * WINDOW), 1)
    valid = (kj > qi) & (kj <= qi + WINDOW) & (kj >= first_key)
    scale = HEAD_DIM ** -0.5
    vals = []
    for g in range(N_KV_HEADS):
        ks = slice(g * HEAD_DIM, (g + 1) * HEAD_DIM)
        vs = slice(KV_WIDTH + g * HEAD_DIM, KV_WIDTH + (g + 1) * HEAD_DIM)
        k = jnp.concatenate([kvp[:, ks], kvc[:, ks]], axis=0)
        vals.append(jnp.concatenate([kvp[:, vs], kvc[:, vs]], axis=0))
        for r in range(GQA_REP):
            h = g * GQA_REP + r
            qh = q[:, h * HEAD_DIM:(h + 1) * HEAD_DIM]
            s_ref[h] = lax.dot_general(qh, k, (((1,), (1,)), ((), ())), preferred_element_type=F32)
    for h in range(N_Q_HEADS):
        s = jnp.where(valid, s_ref[h] * scale, NEG_INF)
        sink = sinks_ref[h]
        m = jnp.maximum(jnp.max(s, axis=-1, keepdims=True), sink)
        den_ref[h] = jnp.broadcast_to(jnp.exp(sink - m), (WINDOW, LANES))
        p_ref[h] = jnp.exp(s - m).astype(BF16)
    ones = jnp.ones((2 * WINDOW, LANES), BF16)
    for h in range(N_Q_HEADS):
        p = p_ref[h]
        den = jnp.dot(p, ones, preferred_element_type=F32) + den_ref[h]
        o = jnp.dot(p, vals[h // GQA_REP], preferred_element_type=F32) / den[:, :HEAD_DIM]
        o_ref[rows, h * HEAD_DIM:(h + 1) * HEAD_DIM] = o.astype(BF16)


def _swa(qkv, sinks, batch, seq):
    nb = seq // WINDOW
    assert seq % (SWA_STEP_BLOCKS * WINDOW) == 0, seq
    ns = nb // SWA_STEP_BLOCKS
    span = SWA_STEP_BLOCKS * WINDOW
    kv_col = Q_WIDTH // (2 * KV_WIDTH)
    return pl.pallas_call(
        _swa_body,
        grid=(batch, ns),
        in_specs=[
            pl.BlockSpec(memory_space=pltpu.SMEM),
            pl.BlockSpec((span, Q_WIDTH), lambda b, n: (b * ns + n, 0)),
            pl.BlockSpec((span, 2 * KV_WIDTH), lambda b, n: (b * ns + n, kv_col)),
            pl.BlockSpec((WINDOW, 2 * KV_WIDTH),
                         lambda b, n: (b * nb + jnp.maximum(n * SWA_STEP_BLOCKS - 1, 0), kv_col)),
        ],
        out_specs=pl.BlockSpec((span, Q_WIDTH), lambda b, n: (b * ns + n, 0)),
        out_shape=jax.ShapeDtypeStruct((batch * seq, Q_WIDTH), BF16),
        scratch_shapes=[pltpu.VMEM((N_Q_HEADS, WINDOW, 2 * WINDOW), F32),
                        pltpu.VMEM((N_Q_HEADS, WINDOW, 2 * WINDOW), BF16),
                        pltpu.VMEM((N_Q_HEADS, WINDOW, LANES), F32)],
        compiler_params=_params(("parallel", "arbitrary")),
        name="swa",
    )(sinks, qkv, qkv, qkv)


def _mem_attn_body(q_ref, kv_ref, o_ref):
    scale = MEM_HEAD_DIM ** -0.5
    for h in range(MEM_HEADS):
        cs = slice(h * MEM_HEAD_DIM, (h + 1) * MEM_HEAD_DIM)
        vs = slice(MEM_WIDTH + h * MEM_HEAD_DIM, MEM_WIDTH + (h + 1) * MEM_HEAD_DIM)
        s = lax.dot_general(q_ref[:, cs], kv_ref[:, cs], (((1,), (1,)), ((), ())),
                            preferred_element_type=F32) * scale
        m = jnp.max(s, axis=-1, keepdims=True)
        p = jnp.exp(s - m)
        denom = jnp.sum(p, axis=-1, keepdims=True)
        o = jnp.dot(p.astype(BF16), kv_ref[:, vs], preferred_element_type=F32) / denom
        o_ref[:, cs] = o.astype(BF16)


def _mem_attn(mq, mem_kv, layer, batch, seq, *, tq=512):
    nq = seq // tq
    return pl.pallas_call(
        _mem_attn_body,
        grid=(batch, nq),
        in_specs=[
            pl.BlockSpec((tq, MEM_WIDTH), lambda b, i: (b * nq + i, 0)),
            pl.BlockSpec((None, N_MEM, 2 * MEM_WIDTH), lambda b, i: (layer, b, 0)),
        ],
        out_specs=pl.BlockSpec((tq, MEM_WIDTH), lambda b, i: (b * nq + i, 0)),
        out_shape=jax.ShapeDtypeStruct((batch * seq, MEM_WIDTH), BF16),
        compiler_params=_params(("parallel", "arbitrary")),
        name="mem_attn",
    )(mq, mem_kv)


def _ssm_operators(lam_re, lam_im, log_dt, b_re, b_im, c_re, c_im, d_skip):
    hp = lax.Precision.HIGHEST
    t_len, g_n, p_n, ch = SSM_CHUNK, SSM_GROUPS, SSM_STATE, SSM_GROUP
    lr = jnp.minimum(lam_re, -1e-4)
    li = lam_im
    dt = jnp.exp(log_dt)[:, None]
    mag = jnp.exp(lr * dt)
    ar = mag * jnp.cos(li * dt)
    ai = mag * jnp.sin(li * dt)
    nr, ni = ar - 1.0, ai
    den = lr * lr + li * li
    kr = (nr * lr + ni * li) / den
    ki = (ni * lr - nr * li) / den
    bbr = kr[..., None] * b_re - ki[..., None] * b_im
    bbi = kr[..., None] * b_im + ki[..., None] * b_re
    steps = jnp.arange(t_len + 1, dtype=F32)[None, :, None]
    pmag = jnp.exp(steps * (lr * dt)[:, None, :])
    ang = steps * (li * dt)[:, None, :]
    pr = pmag * jnp.cos(ang)
    pi = pmag * jnp.sin(ang)
    wr = pr[:, :t_len, :, None] * bbr[:, None] - pi[:, :t_len, :, None] * bbi[:, None]
    wi = pr[:, :t_len, :, None] * bbi[:, None] + pi[:, :t_len, :, None] * bbr[:, None]
    lagk = (jnp.einsum('gcp,gkpd->gkcd', c_re, wr, precision=hp)
            - jnp.einsum('gcp,gkpd->gkcd', c_im, wi, precision=hp))
    nt, gt = SSM_LANE_TILES, SSM_TILE_GROUPS
    lag_c = lagk.reshape(nt, gt, t_len, ch, ch).transpose(0, 2, 1, 4, 3).reshape(nt, t_len * LANES, ch)

    def pair(x, y):
        return jnp.concatenate([x, y], axis=-1)

    def per_step(w):
        return w.reshape(nt, gt, t_len, 2 * p_n).transpose(0, 2, 1, 3)[:, :, :, None, :]

    def per_channel(w):
        return w.reshape(nt, gt, ch, 2 * p_n)[:, None]

    back = jnp.arange(t_len - 1, -1, -1, dtype=F32)[None, :, None]
    bmag = jnp.exp(back * (lr * dt)[:, None, :])
    bang = back * (li * dt)[:, None, :]
    qr, qi = bmag * jnp.cos(bang), bmag * jnp.sin(bang)
    bbr_t, bbi_t = bbr.transpose(0, 2, 1), bbi.transpose(0, 2, 1)
    inp = (per_step(pair(qr, qr)) * per_channel(pair(bbr_t, bbi_t))
           + per_step(pair(-qi, qi)) * per_channel(pair(bbi_t, bbr_t))
           ).reshape(nt, t_len * LANES, 2 * p_n)
    pr1, pi1 = pr[:, 1:], pi[:, 1:]
    outp = (per_step(pair(pr1, pi1)) * per_channel(pair(c_re, -c_re))
            - per_step(pair(pi1, pr1)) * per_channel(pair(c_im, c_im))
            ).reshape(nt, t_len * LANES, 2 * p_n)
    a1 = jnp.concatenate([pr[:, t_len], pr[:, t_len]], axis=-1).reshape(nt, gt, 2 * p_n)
    a2 = jnp.concatenate([-pi[:, t_len], pi[:, t_len]], axis=-1).reshape(nt, gt, 2 * p_n)
    return lag_c, inp, outp, a1, a2, d_skip.reshape(1, SSM_WIDTH)


def _ssm_body(s_ref, lag_ref, inp_ref, outp_ref, a1_ref, a2_ref, d_ref, y_ref,
              u_ref, panel_ref, inpx_ref, outpx_ref, z_ref, zs_ref, sp_ref, *, batch):
    t_len, gt = SSM_CHUNK, SSM_TILE_GROUPS
    n_chunks = u_ref.shape[0]
    per_seq = n_chunks // batch
    flat = t_len * LANES

    for t in range(t_len):
        u_ref[:, t * LANES:(t + 1) * LANES] = s_ref[pl.ds(t, n_chunks, stride=t_len), :].astype(BF16)

    row_group = (lax.broadcasted_iota(jnp.int32, (flat, 1), 0) >> 4) & (gt - 1)
    col_group = lax.broadcasted_iota(jnp.int32, (1, LANES), 1) >> 4
    spread = (lax.broadcasted_iota(jnp.int32, (SSM_GROUP, LANES), 1) & (SSM_GROUP - 1)
              == lax.broadcasted_iota(jnp.int32, (SSM_GROUP, LANES), 0)).astype(BF16)
    lag = jnp.dot(lag_ref[...].astype(BF16), spread, preferred_element_type=F32)
    lag = jnp.where(row_group == col_group, lag, 0.0).astype(BF16)

    for r in range(t_len):
        k_left, k_right = t_len - 2 - r, t_len - 1 - r
        left = lag[k_left * LANES:(k_left + 1) * LANES] if k_left >= 0 else jnp.zeros((LANES, LANES), BF16)
        panel_ref[r * LANES:(r + 1) * LANES, :LANES] = left
        panel_ref[r * LANES:(r + 1) * LANES, LANES:] = lag[k_right * LANES:(k_right + 1) * LANES]

    inp = inp_ref[...]
    outp = outp_ref[...]
    for g in range(gt):
        inpx_ref[:, g * LANES:(g + 1) * LANES] = jnp.where(row_group == g, inp, 0.0).astype(BF16)
        outpx_ref[:, g * LANES:(g + 1) * LANES] = jnp.where(row_group == g, outp, 0.0).astype(BF16)

    z = jnp.dot(u_ref[...], inpx_ref[...], preferred_element_type=F32)
    for g in range(gt):
        zg = z[:, g * LANES:(g + 1) * LANES]
        z_ref[pl.ds(g, n_chunks, stride=gt), :] = zg
        zs_ref[pl.ds(g, n_chunks, stride=gt), :] = pltpu.roll(zg, SSM_STATE, axis=1)

    a1 = a1_ref[...]
    a2 = a2_ref[...]

    def step(c, carry):
        new = []
        for b in range(batch):
            v0, v1 = carry[b]
            row = pl.multiple_of((b * per_seq + c) * gt, gt)
            sp_ref[pl.ds(row, gt), :] = v0
            z0 = z_ref[pl.ds(row, gt), :]
            z1 = zs_ref[pl.ds(row, gt), :]
            new.append((a1 * v0 + a2 * v1 + z0, a1 * v1 - a2 * v0 + z1))
        return tuple(new)

    zero = jnp.zeros((gt, LANES), F32)
    lax.fori_loop(0, per_seq, step, tuple((zero, zero) for _ in range(batch)), unroll=4)

    sp = jnp.concatenate([sp_ref[pl.ds(g, n_chunks, stride=gt), :] for g in range(gt)], axis=1).astype(BF16)
    d2 = jnp.concatenate([d_ref[...], d_ref[...]], axis=1)
    for q in range(t_len // 2):
        cols = slice(2 * q * LANES, (2 * q + 2) * LANES)
        k_len = (2 * q + 2) * LANES
        y = jnp.dot(u_ref[:, :k_len], panel_ref[flat - k_len:, :], preferred_element_type=F32)
        y = y + lax.dot_general(sp, outpx_ref[cols, :], (((1,), (1,)), ((), ())), preferred_element_type=F32)
        y = y + d2 * u_ref[:, cols].astype(F32)
        y_ref[pl.ds(2 * q, n_chunks, stride=t_len), :] = y[:, :LANES]
        y_ref[pl.ds(2 * q + 1, n_chunks, stride=t_len), :] = y[:, LANES:]


def _ssm(s_in, ops, layer, batch):
    lag_c, inp, outp, a1, a2, d_row = ops
    n = s_in.shape[0]
    n_chunks = n // SSM_CHUNK
    flat = SSM_CHUNK * LANES
    return pl.pallas_call(
        functools.partial(_ssm_body, batch=batch),
        grid=(SSM_LANE_TILES,),
        in_specs=[
            pl.BlockSpec((n, LANES), lambda j: (0, j)),
            pl.BlockSpec((None, None, flat, SSM_GROUP), lambda j: (layer, j, 0, 0)),
            pl.BlockSpec((None, None, flat, 2 * SSM_STATE), lambda j: (layer, j, 0, 0)),
            pl.BlockSpec((None, None, flat, 2 * SSM_STATE), lambda j: (layer, j, 0, 0)),
            pl.BlockSpec((None, None, SSM_TILE_GROUPS, 2 * SSM_STATE), lambda j: (layer, j, 0, 0)),
            pl.BlockSpec((None, None, SSM_TILE_GROUPS, 2 * SSM_STATE), lambda j: (layer, j, 0, 0)),
            pl.BlockSpec((None, 1, LANES), lambda j: (layer, 0, j)),
        ],
        out_specs=pl.BlockSpec((n, LANES), lambda j: (0, j)),
        out_shape=jax.ShapeDtypeStruct((n, SSM_WIDTH), F32),
        scratch_shapes=[
            pltpu.VMEM((n_chunks, flat), BF16),
            pltpu.VMEM((flat, 2 * LANES), BF16),
            pltpu.VMEM((flat, SSM_TILE_GROUPS * 2 * SSM_STATE), BF16),
            pltpu.VMEM((flat, SSM_TILE_GROUPS * 2 * SSM_STATE), BF16),
            pltpu.VMEM((n_chunks * SSM_TILE_GROUPS, 2 * SSM_STATE), F32),
            pltpu.VMEM((n_chunks * SSM_TILE_GROUPS, 2 * SSM_STATE), F32),
            pltpu.VMEM((n_chunks * SSM_TILE_GROUPS, 2 * SSM_STATE), F32),
        ],
        compiler_params=_params(("parallel",)),
        name="ssm",
    )(s_in, lag_c, inp, outp, a1, a2, d_row)


MERGE_ROW_CHUNK = 256


def _gate_merge_body(xn_ref, swa_ref, ssm_ref, mem_ref, wg0_ref, wg1_ref, wg2_ref, wswa_ref,
                     wga_ref, wgb_ref, wmem_ref, o_ref):
    tn = o_ref.shape[1]
    half_k = D_MODEL // 2
    wg01 = jnp.concatenate([wg0_ref[...].astype(BF16), wg1_ref[...].astype(BF16)], axis=1)
    wg2 = wg2_ref[...].astype(BF16)
    wglu = jnp.concatenate([wga_ref[...].astype(BF16), wgb_ref[...].astype(BF16)], axis=1)
    wswa = wswa_ref[...].astype(BF16)
    wmem = wmem_ref[...].astype(BF16)
    for r in range(o_ref.shape[0] // MERGE_ROW_CHUNK):
        rows = pl.ds(r * MERGE_ROW_CHUNK, MERGE_ROW_CHUNK)
        xn = xn_ref[rows, :]
        ys = ssm_ref[rows, :].astype(BF16)
        y_swa = jnp.dot(swa_ref[rows, :], wswa, preferred_element_type=F32)
        y_mem = jnp.dot(mem_ref[rows, :], wmem, preferred_element_type=F32)
        glu = jnp.dot(ys, wglu, preferred_element_type=F32)
        y_ssm = glu[:, :tn] * _sigmoid(glu[:, tn:])
        g01 = jnp.dot(xn, wg01, preferred_element_type=F32)
        g2 = (jnp.dot(xn[:, :half_k], wg2[:half_k], preferred_element_type=F32)
              + jnp.dot(xn[:, half_k:], wg2[half_k:], preferred_element_type=F32))
        merged = _sigmoid(g01[:, :tn]) * y_swa + _sigmoid(g01[:, tn:]) * y_ssm + _sigmoid(g2) * y_mem
        o_ref[rows, :] = merged.astype(BF16)


def _gate_merge(xn, o_swa, y_s, o_mem, w_in, layer, w_swa_up, w_ssm_glu, w_mem_up, *, tm=1024, tn=256):
    n = xn.shape[0]
    nj = D_MODEL // tn
    g0 = GATE_OFFSET // tn
    row = lambda i, j: (i, 0)
    return pl.pallas_call(
        _gate_merge_body,
        grid=(n // tm, nj),
        in_specs=[
            pl.BlockSpec((tm, D_MODEL), row),
            pl.BlockSpec((tm, Q_WIDTH), row),
            pl.BlockSpec((tm, SSM_WIDTH), row),
            pl.BlockSpec((tm, MEM_WIDTH), row),
            pl.BlockSpec((None, D_MODEL, tn), lambda i, j: (layer, 0, g0 + j)),
            pl.BlockSpec((None, D_MODEL, tn), lambda i, j: (layer, 0, g0 + j + nj)),
            pl.BlockSpec((None, D_MODEL, tn), lambda i, j: (layer, 0, g0 + j + 2 * nj)),
            pl.BlockSpec((None, Q_WIDTH, tn), lambda i, j: (layer, 0, j)),
            pl.BlockSpec((None, SSM_WIDTH, tn), lambda i, j: (layer, 0, j)),
            pl.BlockSpec((None, SSM_WIDTH, tn), lambda i, j: (layer, 0, j + nj)),
            pl.BlockSpec((None, MEM_WIDTH, tn), lambda i, j: (layer, 0, j)),
        ],
        out_specs=pl.BlockSpec((tm, tn), lambda i, j: (i, j)),
        out_shape=jax.ShapeDtypeStruct((n, D_MODEL), BF16),
        compiler_params=_params(("parallel", "arbitrary")),
        name="gate_merge",
    )(xn, o_swa, y_s, o_mem, w_in, w_in, w_in, w_swa_up, w_ssm_glu, w_ssm_glu, w_mem_up)


def _out_proj_body(h_ref, m_ref, w_ref, o_ref, wb_ref):
    @pl.when(pl.program_id(0) == 0)
    def _():
        wb_ref[...] = w_ref[...].astype(BF16)

    half = D_MODEL // 2
    for c in range(2):
        cols = pl.ds(c * half, half)
        o_ref[:, cols] = h_ref[:, cols] + jnp.dot(m_ref[...], wb_ref[:, cols], preferred_element_type=F32)


def _out_proj(h, merged, w_out, layer, *, tm=512):
    n = h.shape[0]
    return pl.pallas_call(
        _out_proj_body,
        grid=(n // tm,),
        in_specs=[
            pl.BlockSpec((tm, D_MODEL), lambda i: (i, 0)),
            pl.BlockSpec((tm, D_MODEL), lambda i: (i, 0)),
            pl.BlockSpec((None, D_MODEL, D_MODEL), lambda i: (layer, 0, 0), pipeline_mode=pl.Buffered(1)),
        ],
        out_specs=pl.BlockSpec((tm, D_MODEL), lambda i: (i, 0)),
        out_shape=jax.ShapeDtypeStruct((n, D_MODEL), F32),
        scratch_shapes=[pltpu.VMEM((D_MODEL, D_MODEL), BF16)],
        compiler_params=_params(("arbitrary",)),
        name="out_proj",
    )(h, merged, w_out)


def kernel(x, mem, ffn1_norm, ffn1_w_in, ffn1_w_out, mix_norm, mem_norm, w_in, sinks, w_mem_kv, lam_re, lam_im, log_dt, b_re, b_im, c_re, c_im, d_skip, w_ssm_glu, w_swa_up, w_mem_up, w_out, ffn2_norm, ffn2_w_in, ffn2_w_out, final_norm):
    batch, seq = x.shape[0], x.shape[1]
    n = batch * seq
    h = x.reshape(n, D_MODEL)
    mem2 = mem.reshape(batch * N_MEM, D_MODEL)
    final_w = final_norm.reshape(1, D_MODEL)
    ssm_ops = jax.vmap(_ssm_operators)(lam_re, lam_im, log_dt, b_re, b_im, c_re, c_im, d_skip)
    mem_kv = _mem_kv(mem2, mem_norm, w_mem_kv)
    for l in range(DEPTH):
        mix_w = mix_norm[l].reshape(1, D_MODEL)

        h = _ffn(h, ffn1_norm[l].reshape(1, D_MODEL), ffn1_w_in, ffn1_w_out, final_w, l, apply_final_norm=False)
        qkv, s_in, mq, xn = _mix_proj(h, mix_w, w_in, l)
        o_swa = _swa(qkv, sinks[l], batch, seq)
        o_mem = _mem_attn(mq, mem_kv, l, batch, seq)
        y_s = _ssm(s_in, ssm_ops, l, batch)
        merged = _gate_merge(xn, o_swa, y_s, o_mem, w_in, l, w_swa_up, w_ssm_glu, w_mem_up)
        h = _out_proj(h, merged, w_out, l)
        h = _ffn(h, ffn2_norm[l].reshape(1, D_MODEL), ffn2_w_in, ffn2_w_out, final_w, l,
                 apply_final_norm=(l == DEPTH - 1))
    return h.reshape(batch, seq, D_MODEL)
```

```python
import functools
import math

import jax
import jax.numpy as jnp
from jax import lax
from jax.experimental import pallas as pl
from jax.experimental.pallas import tpu as pltpu

D_MODEL = 2048
DEPTH = 4
N_MEM = 256
D_FF = 5632
RMS_EPS = 1e-5

WINDOW = 128
HEAD_DIM = 64
N_Q_HEADS = 16
N_KV_HEADS = 4
GQA_REP = N_Q_HEADS // N_KV_HEADS
Q_WIDTH = N_Q_HEADS * HEAD_DIM
KV_WIDTH = N_KV_HEADS * HEAD_DIM

SSM_WIDTH = 1024
SSM_GROUP = 16
SSM_GROUPS = SSM_WIDTH // SSM_GROUP
SSM_STATE = 64
SSM_CHUNK = 16
LANES = 128
SSM_LANE_TILES = SSM_WIDTH // LANES
SSM_TILE_GROUPS = LANES // SSM_GROUP

MEM_HEADS = 4
MEM_HEAD_DIM = 256
MEM_WIDTH = MEM_HEADS * MEM_HEAD_DIM

N_BRANCHES = 3
NEG_INF = -1e30

QKV_WIDTH = Q_WIDTH + 2 * KV_WIDTH
SSM_OFFSET = QKV_WIDTH
MEMQ_OFFSET = SSM_OFFSET + SSM_WIDTH
GATE_OFFSET = MEMQ_OFFSET + MEM_WIDTH

VMEM_LIMIT_BYTES = 56 * 1024 * 1024

BF16 = jnp.bfloat16
F32 = jnp.float32


def _params(semantics):
    return pltpu.CompilerParams(dimension_semantics=semantics, vmem_limit_bytes=VMEM_LIMIT_BYTES)


def _rms_normalize(x, w):
    ms = jnp.mean(x * x, axis=-1, keepdims=True)
    return (x * lax.rsqrt(ms + RMS_EPS)) * w


def _sigmoid(x):
    return 0.5 * jnp.tanh(0.5 * x) + 0.5


FFN_ROW_CHUNK = 1024


class _RowTileIO:
    def __init__(self, h_hbm, out_hbm, hbuf, acc_ref, sems):
        self.h_hbm, self.out_hbm, self.hbuf, self.acc_ref, self.sems = h_hbm, out_hbm, hbuf, acc_ref, sems
        self.tm = acc_ref.shape[0]

    def _fetch(self, tile):
        return pltpu.make_async_copy(self.h_hbm.at[pl.ds(tile * self.tm, self.tm), :], self.hbuf, self.sems.at[0])

    def _write_back(self, tile):
        return pltpu.make_async_copy(self.acc_ref, self.out_hbm.at[pl.ds(tile * self.tm, self.tm), :],
                                     self.sems.at[1])

    def begin(self, xn_ref, nw_ref):
        i = pl.program_id(0)

        @pl.when(i == 0)
        def _():
            self._fetch(0).start()

        self._fetch(i).wait()
        xn_ref[...] = _rms_normalize(self.hbuf[...], nw_ref[...]).astype(BF16)

        @pl.when(i > 0)
        def _():
            self._write_back(i - 1).wait()

        self.acc_ref[...] = self.hbuf[...]

        @pl.when(i + 1 < pl.num_programs(0))
        def _():
            self._fetch(i + 1).start()

    def end(self):
        i = pl.program_id(0)
        self._write_back(i).start()

        @pl.when(i == pl.num_programs(0) - 1)
        def _():
            self._write_back(i).wait()


def _ffn_body(h_hbm, nw_ref, wg_ref, wu_ref, wo_ref, fw_ref, out_hbm, xn_ref, hbuf, acc_ref, sems, *,
              apply_final_norm):
    j = pl.program_id(1)
    io = _RowTileIO(h_hbm, out_hbm, hbuf, acc_ref, sems)

    @pl.when(j == 0)
    def _():
        io.begin(xn_ref, nw_ref)

    wg = wg_ref[...].astype(BF16)
    wu = wu_ref[...].astype(BF16)
    wo = wo_ref[...].astype(BF16)
    for r in range(acc_ref.shape[0] // FFN_ROW_CHUNK):
        rows = pl.ds(r * FFN_ROW_CHUNK, FFN_ROW_CHUNK)
        xn = xn_ref[rows, :]
        g = jnp.dot(xn, wg, preferred_element_type=F32)
        u = jnp.dot(xn, wu, preferred_element_type=F32)
        a = ((0.5 * g) * _sigmoid(g)) * u
        acc_ref[rows, :] += jnp.dot(a.astype(BF16), wo, preferred_element_type=F32)

    @pl.when(j == pl.num_programs(1) - 1)
    def _():
        if apply_final_norm:
            acc_ref[...] = _rms_normalize(acc_ref[...], fw_ref[...])
        io.end()


def _ffn(h, norm_w, w_in, w_out, final_w, layer, *, apply_final_norm, tm=1024, tf=512):
    n = h.shape[0]
    nf = D_FF // tf
    return pl.pallas_call(
        functools.partial(_ffn_body, apply_final_norm=apply_final_norm),
        grid=(n // tm, nf),
        in_specs=[
            pl.BlockSpec(memory_space=pl.ANY),
            pl.BlockSpec((1, D_MODEL), lambda i, j: (0, 0)),
            pl.BlockSpec((None, D_MODEL, tf), lambda i, j: (layer, 0, j)),
            pl.BlockSpec((None, D_MODEL, tf), lambda i, j: (layer, 0, j + nf)),
            pl.BlockSpec((None, tf, D_MODEL), lambda i, j: (layer, j, 0)),
            pl.BlockSpec((1, D_MODEL), lambda i, j: (0, 0)),
        ],
        out_specs=pl.BlockSpec(memory_space=pl.ANY),
        out_shape=jax.ShapeDtypeStruct((n, D_MODEL), F32),
        scratch_shapes=[pltpu.VMEM((tm, D_MODEL), BF16), pltpu.VMEM((tm, D_MODEL), F32),
                        pltpu.VMEM((tm, D_MODEL), F32), pltpu.SemaphoreType.DMA((2,))],
        compiler_params=_params(("arbitrary", "arbitrary")),
        name="ffn",
    )(h, norm_w, w_in, w_in, w_out, final_w)


def _mem_kv_body(m_ref, nw_ref, w_ref, o_ref, xn_ref):
    @pl.when(pl.program_id(1) == 0)
    def _():
        xn_ref[...] = _rms_normalize(m_ref[...], nw_ref[...]).astype(BF16)

    o_ref[...] = jnp.dot(xn_ref[...], w_ref[...].astype(BF16), preferred_element_type=F32).astype(BF16)


def _mem_kv(mem2, mem_norm, w_mem_kv, *, tn=512):
    rows = mem2.shape[0]
    width = 2 * MEM_WIDTH
    return pl.pallas_call(
        _mem_kv_body,
        grid=(DEPTH, width // tn),
        in_specs=[
            pl.BlockSpec((rows, D_MODEL), lambda l, j: (0, 0)),
            pl.BlockSpec((None, 1, D_MODEL), lambda l, j: (l, 0, 0)),
            pl.BlockSpec((None, D_MODEL, tn), lambda l, j: (l, 0, j)),
        ],
        out_specs=pl.BlockSpec((None, rows, tn), lambda l, j: (l, 0, j)),
        out_shape=jax.ShapeDtypeStruct((DEPTH, rows, width), BF16),
        scratch_shapes=[pltpu.VMEM((rows, D_MODEL), BF16)],
        compiler_params=_params(("arbitrary", "arbitrary")),
        name="mem_kv",
    )(mem2, mem_norm.reshape(DEPTH, 1, D_MODEL), w_mem_kv)


def _mix_proj_body(h_hbm, nw_ref, w_ref, qkv_ref, s_ref, mq_ref, xn_ref, hbuf, sem, *, n_qkv, n_ssm):
    i, j = pl.program_id(0), pl.program_id(1)
    tm = hbuf.shape[0]

    def fetch(tile):
        return pltpu.make_async_copy(h_hbm.at[pl.ds(tile * tm, tm), :], hbuf, sem.at[0])

    @pl.when(j == 0)
    def _():
        @pl.when(i == 0)
        def _():
            fetch(0).start()

        fetch(i).wait()
        xn_ref[...] = _rms_normalize(hbuf[...], nw_ref[...]).astype(BF16)

        @pl.when(i + 1 < pl.num_programs(0))
        def _():
            fetch(i + 1).start()

    def project(out_ref):
        out_ref[...] = jnp.dot(xn_ref[...], w_ref[...].astype(BF16), preferred_element_type=F32).astype(out_ref.dtype)

    @pl.when(j < n_qkv)
    def _():
        project(qkv_ref)

    @pl.when((j >= n_qkv) & (j < n_qkv + n_ssm))
    def _():
        project(s_ref)

    @pl.when(j >= n_qkv + n_ssm)
    def _():
        project(mq_ref)


def _mix_proj(h, norm_w, w_in, layer, *, tm=2048, tn=512):
    n = h.shape[0]
    n_qkv, n_ssm, n_mq = QKV_WIDTH // tn, SSM_WIDTH // tn, MEM_WIDTH // tn
    return pl.pallas_call(
        functools.partial(_mix_proj_body, n_qkv=n_qkv, n_ssm=n_ssm),
        grid=(n // tm, n_qkv + n_ssm + n_mq),
        in_specs=[
            pl.BlockSpec(memory_space=pl.ANY),
            pl.BlockSpec((1, D_MODEL), lambda i, j: (0, 0)),
            pl.BlockSpec((None, D_MODEL, tn), lambda i, j: (layer, 0, j)),
        ],
        out_specs=[
            pl.BlockSpec((tm, tn), lambda i, j: (i, jnp.minimum(j, n_qkv - 1))),
            pl.BlockSpec((tm, tn), lambda i, j: (i, jnp.clip(j - n_qkv, 0, n_ssm - 1))),
            pl.BlockSpec((tm, tn), lambda i, j: (i, jnp.clip(j - n_qkv - n_ssm, 0, n_mq - 1))),
            pl.BlockSpec((tm, D_MODEL), lambda i, j: (i, 0), pipeline_mode=pl.Buffered(1)),
        ],
        out_shape=[
            jax.ShapeDtypeStruct((n, QKV_WIDTH), BF16),
            jax.ShapeDtypeStruct((n, SSM_WIDTH), F32),
            jax.ShapeDtypeStruct((n, MEM_WIDTH), BF16),
            jax.ShapeDtypeStruct((n, D_MODEL), BF16),
        ],
        scratch_shapes=[pltpu.VMEM((tm, D_MODEL), F32), pltpu.SemaphoreType.DMA((1,))],
        compiler_params=_params(("arbitrary", "arbitrary")),
        name="mix_proj",
    )(h, norm_w, w_in)


SWA_STEP_BLOCKS = 2


def _swa_body(sinks_ref, q_ref, kvc_ref, kvp_ref, o_ref, s_ref, p_ref, den_ref):
    step = pl.program_id(1)
    for t in range(SWA_STEP_BLOCKS):
        rows = slice(t * WINDOW, (t + 1) * WINDOW)
        if t == 0:
            kvp, first_key = kvp_ref[...], jnp.where(step > 0, 0, WINDOW)
        else:
            kvp, first_key = kvc_ref[(t - 1) * WINDOW:t * WINDOW, :], 0
        _swa_block(sinks_ref, q_ref[rows, :], kvc_ref[rows, :], kvp, first_key, o_ref, rows, s_ref, p_ref, den_ref)


def _swa_block(sinks_ref, q, kvc, kvp, first_key, o_ref, rows, s_ref, p_ref, den_ref):
    qi = lax.broadcasted_iota(jnp.int32, (WINDOW, 2 * WINDOW), 0)
    kj = lax.broadcasted_iota(jnp.int32, (WINDOW, 2 * WINDOW), 1)
    valid = (kj > qi) & (kj <= qi + WINDOW) & (kj >= first_key)
    scale = HEAD_DIM ** -0.5
    vals = []
    for g in range(N_KV_HEADS):
        ks = slice(g * HEAD_DIM, (g + 1) * HEAD_DIM)
        vs = slice(KV_WIDTH + g * HEAD_DIM, KV_WIDTH + (g + 1) * HEAD_DIM)
        k = jnp.concatenate([kvp[:, ks], kvc[:, ks]], axis=0)
        vals.append(jnp.concatenate([kvp[:, vs], kvc[:, vs]], axis=0))
        for r in range(GQA_REP):
            h = g * GQA_REP + r
            qh = q[:, h * HEAD_DIM:(h + 1) * HEAD_DIM]
            s_ref[h] = lax.dot_general(qh, k, (((1,), (1,)), ((), ())), preferred_element_type=F32)
    for h in range(N_Q_HEADS):
        s = jnp.where(valid, s_ref[h] * scale, NEG_INF)
        sink = sinks_ref[h]
        m = jnp.maximum(jnp.max(s, axis=-1, keepdims=True), sink)
        den_ref[h] = jnp.broadcast_to(jnp.exp(sink - m), (WINDOW, LANES))
        p_ref[h] = jnp.exp(s - m).astype(BF16)
    ones = jnp.ones((2 * WINDOW, LANES), BF16)
    for h in range(N_Q_HEADS):
        p = p_ref[h]
        den = jnp.dot(p, ones, preferred_element_type=F32) + den_ref[h]
        o = jnp.dot(p, vals[h // GQA_REP], preferred_element_type=F32) / den[:, :HEAD_DIM]
        o_ref[rows, h * HEAD_DIM:(h + 1) * HEAD_DIM] = o.astype(BF16)


def _swa(qkv, sinks, batch, seq):
    nb = seq // WINDOW
    assert seq % (SWA_STEP_BLOCKS * WINDOW) == 0, seq
    ns = nb // SWA_STEP_BLOCKS
    span = SWA_STEP_BLOCKS * WINDOW
    kv_col = Q_WIDTH // (2 * KV_WIDTH)
    return pl.pallas_call(
        _swa_body,
        grid=(batch, ns),
        in_specs=[
            pl.BlockSpec(memory_space=pltpu.SMEM),
            pl.BlockSpec((span, Q_WIDTH), lambda b, n: (b * ns + n, 0)),
            pl.BlockSpec((span, 2 * KV_WIDTH), lambda b, n: (b * ns + n, kv_col)),
            pl.BlockSpec((WINDOW, 2 * KV_WIDTH),
                         lambda b, n: (b * nb + jnp.maximum(n * SWA_STEP_BLOCKS - 1, 0), kv_col)),
        ],
        out_specs=pl.BlockSpec((span, Q_WIDTH), lambda b, n: (b * ns + n, 0)),
        out_shape=jax.ShapeDtypeStruct((batch * seq, Q_WIDTH), BF16),
        scratch_shapes=[pltpu.VMEM((N_Q_HEADS, WINDOW, 2 * WINDOW), F32),
                        pltpu.VMEM((N_Q_HEADS, WINDOW, 2 * WINDOW), BF16),
                        pltpu.VMEM((N_Q_HEADS, WINDOW, LANES), F32)],
        compiler_params=_params(("parallel", "arbitrary")),
        name="swa",
    )(sinks, qkv, qkv, qkv)


def _mem_attn_body(q_ref, kv_ref, o_ref):
    scale = MEM_HEAD_DIM ** -0.5
    for h in range(MEM_HEADS):
        cs = slice(h * MEM_HEAD_DIM, (h + 1) * MEM_HEAD_DIM)
        vs = slice(MEM_WIDTH + h * MEM_HEAD_DIM, MEM_WIDTH + (h + 1) * MEM_HEAD_DIM)
        s = lax.dot_general(q_ref[:, cs], kv_ref[:, cs], (((1,), (1,)), ((), ())),
                            preferred_element_type=F32) * scale
        m = jnp.max(s, axis=-1, keepdims=True)
        p = jnp.exp(s - m)
        denom = jnp.sum(p, axis=-1, keepdims=True)
        o = jnp.dot(p.astype(BF16), kv_ref[:, vs], preferred_element_type=F32) / denom
        o_ref[:, cs] = o.astype(BF16)


def _mem_attn(mq, mem_kv, layer, batch, seq, *, tq=512):
    nq = seq // tq
    return pl.pallas_call(
        _mem_attn_body,
        grid=(batch, nq),
        in_specs=[
            pl.BlockSpec((tq, MEM_WIDTH), lambda b, i: (b * nq + i, 0)),
            pl.BlockSpec((None, N_MEM, 2 * MEM_WIDTH), lambda b, i: (layer, b, 0)),
        ],
        out_specs=pl.BlockSpec((tq, MEM_WIDTH), lambda b, i: (b * nq + i, 0)),
        out_shape=jax.ShapeDtypeStruct((batch * seq, MEM_WIDTH), BF16),
        compiler_params=_params(("parallel", "arbitrary")),
        name="mem_attn",
    )(mq, mem_kv)


def _ssm_operators(lam_re, lam_im, log_dt, b_re, b_im, c_re, c_im, d_skip):
    hp = lax.Precision.HIGHEST
    t_len, g_n, p_n, ch = SSM_CHUNK, SSM_GROUPS, SSM_STATE, SSM_GROUP
    lr = jnp.minimum(lam_re, -1e-4)
    li = lam_im
    dt = jnp.exp(log_dt)[:, None]
    mag = jnp.exp(lr * dt)
    ar = mag * jnp.cos(li * dt)
    ai = mag * jnp.sin(li * dt)
    nr, ni = ar - 1.0, ai
    den = lr * lr + li * li
    kr = (nr * lr + ni * li) / den
    ki = (ni * lr - nr * li) / den
    bbr = kr[..., None] * b_re - ki[..., None] * b_im
    bbi = kr[..., None] * b_im + ki[..., None] * b_re
    steps = jnp.arange(t_len + 1, dtype=F32)[None, :, None]
    pmag = jnp.exp(steps * (lr * dt)[:, None, :])
    ang = steps * (li * dt)[:, None, :]
    pr = pmag * jnp.cos(ang)
    pi = pmag * jnp.sin(ang)
    wr = pr[:, :t_len, :, None] * bbr[:, None] - pi[:, :t_len, :, None] * bbi[:, None]
    wi = pr[:, :t_len, :, None] * bbi[:, None] + pi[:, :t_len, :, None] * bbr[:, None]
    lagk = (jnp.einsum('gcp,gkpd->gkcd', c_re, wr, precision=hp)
            - jnp.einsum('gcp,gkpd->gkcd', c_im, wi, precision=hp))
    nt, gt = SSM_LANE_TILES, SSM_TILE_GROUPS
    lag_c = lagk.reshape(nt, gt, t_len, ch, ch).transpose(0, 2, 1, 4, 3).reshape(nt, t_len * LANES, ch)

    def pair(x, y):
        return jnp.concatenate([x, y], axis=-1)

    def per_step(w):
        return w.reshape(nt, gt, t_len, 2 * p_n).transpose(0, 2, 1, 3)[:, :, :, None, :]

    def per_channel(w):
        return w.reshape(nt, gt, ch, 2 * p_n)[:, None]

    back = jnp.arange(t_len - 1, -1, -1, dtype=F32)[None, :, None]
    bmag = jnp.exp(back * (lr * dt)[:, None, :])
    bang = back * (li * dt)[:, None, :]
    qr, qi = bmag * jnp.cos(bang), bmag * jnp.sin(bang)
    bbr_t, bbi_t = bbr.transpose(0, 2, 1), bbi.transpose(0, 2, 1)
    inp = (per_step(pair(qr, qr)) * per_channel(pair(bbr_t, bbi_t))
           + per_step(pair(-qi, qi)) * per_channel(pair(bbi_t, bbr_t))
           ).reshape(nt, t_len * LANES, 2 * p_n)
    pr1, pi1 = pr[:, 1:], pi[:, 1:]
    outp = (per_step(pair(pr1, pi1)) * per_channel(pair(c_re, -c_re))
            - per_step(pair(pi1, pr1)) * per_channel(pair(c_im, c_im))
            ).reshape(nt, t_len * LANES, 2 * p_n)
    a1 = jnp.concatenate([pr[:, t_len], pr[:, t_len]], axis=-1).reshape(nt, gt, 2 * p_n)
    a2 = jnp.concatenate([-pi[:, t_len], pi[:, t_len]], axis=-1).reshape(nt, gt, 2 * p_n)
    return lag_c, inp, outp, a1, a2, d_skip.reshape(1, SSM_WIDTH)


def _ssm_body(s_ref, lag_ref, inp_ref, outp_ref, a1_ref, a2_ref, d_ref, y_ref,
              u_ref, panel_ref, inpx_ref, outpx_ref, z_ref, zs_ref, sp_ref, *, batch):
    t_len, gt = SSM_CHUNK, SSM_TILE_GROUPS
    n_chunks = u_ref.shape[0]
    per_seq = n_chunks // batch
    flat = t_len * LANES

    for t in range(t_len):
        u_ref[:, t * LANES:(t + 1) * LANES] = s_ref[pl.ds(t, n_chunks, stride=t_len), :].astype(BF16)

    row_group = (lax.broadcasted_iota(jnp.int32, (flat, 1), 0) >> 4) & (gt - 1)
    col_group = lax.broadcasted_iota(jnp.int32, (1, LANES), 1) >> 4
    spread = (lax.broadcasted_iota(jnp.int32, (SSM_GROUP, LANES), 1) & (SSM_GROUP - 1)
              == lax.broadcasted_iota(jnp.int32, (SSM_GROUP, LANES), 0)).astype(BF16)
    lag = jnp.dot(lag_ref[...].astype(BF16), spread, preferred_element_type=F32)
    lag = jnp.where(row_group == col_group, lag, 0.0).astype(BF16)

    for r in range(t_len):
        k_left, k_right = t_len - 2 - r, t_len - 1 - r
        left = lag[k_left * LANES:(k_left + 1) * LANES] if k_left >= 0 else jnp.zeros((LANES, LANES), BF16)
        panel_ref[r * LANES:(r + 1) * LANES, :LANES] = left
        panel_ref[r * LANES:(r + 1) * LANES, LANES:] = lag[k_right * LANES:(k_right + 1) * LANES]

    inp = inp_ref[...]
    outp = outp_ref[...]
    for g in range(gt):
        inpx_ref[:, g * LANES:(g + 1) * LANES] = jnp.where(row_group == g, inp, 0.0).astype(BF16)
        outpx_ref[:, g * LANES:(g + 1) * LANES] = jnp.where(row_group == g, outp, 0.0).astype(BF16)

    z = jnp.dot(u_ref[...], inpx_ref[...], preferred_element_type=F32)
    for g in range(gt):
        zg = z[:, g * LANES:(g + 1) * LANES]
        z_ref[pl.ds(g, n_chunks, stride=gt), :] = zg
        zs_ref[pl.ds(g, n_chunks, stride=gt), :] = pltpu.roll(zg, SSM_STATE, axis=1)

    a1 = a1_ref[...]
    a2 = a2_ref[...]

    def step(c, carry):
        new = []
        for b in range(batch):
            v0, v1 = carry[b]
            row = pl.multiple_of((b * per_seq + c) * gt, gt)
            sp_ref[pl.ds(row, gt), :] = v0
            z0 = z_ref[pl.ds(row, gt), :]
            z1 = zs_ref[pl.ds(row, gt), :]
            new.append((a1 * v0 + a2 * v1 + z0, a1 * v1 - a2 * v0 + z1))
        return tuple(new)

    zero = jnp.zeros((gt, LANES), F32)
    lax.fori_loop(0, per_seq, step, tuple((zero, zero) for _ in range(batch)), unroll=4)

    sp = jnp.concatenate([sp_ref[pl.ds(g, n_chunks, stride=gt), :] for g in range(gt)], axis=1).astype(BF16)
    d2 = jnp.concatenate([d_ref[...], d_ref[...]], axis=1)
    for q in range(t_len // 2):
        cols = slice(2 * q * LANES, (2 * q + 2) * LANES)
        k_len = (2 * q + 2) * LANES
        y = jnp.dot(u_ref[:, :k_len], panel_ref[flat - k_len:, :], preferred_element_type=F32)
        y = y + lax.dot_general(sp, outpx_ref[cols, :], (((1,), (1,)), ((), ())), preferred_element_type=F32)
        y = y + d2 * u_ref[:, cols].astype(F32)
        y_ref[pl.ds(2 * q, n_chunks, stride=t_len), :] = y[:, :LANES]
        y_ref[pl.ds(2 * q + 1, n_chunks, stride=t_len), :] = y[:, LANES:]


def _ssm(s_in, ops, layer, batch):
    lag_c, inp, outp, a1, a2, d_row = ops
    n = s_in.shape[0]
    n_chunks = n // SSM_CHUNK
    flat = SSM_CHUNK * LANES
    return pl.pallas_call(
        functools.partial(_ssm_body, batch=batch),
        grid=(SSM_LANE_TILES,),
        in_specs=[
            pl.BlockSpec((n, LANES), lambda j: (0, j)),
            pl.BlockSpec((None, None, flat, SSM_GROUP), lambda j: (layer, j, 0, 0)),
            pl.BlockSpec((None, None, flat, 2 * SSM_STATE), lambda j: (layer, j, 0, 0)),
            pl.BlockSpec((None, None, flat, 2 * SSM_STATE), lambda j: (layer, j, 0, 0)),
            pl.BlockSpec((None, None, SSM_TILE_GROUPS, 2 * SSM_STATE), lambda j: (layer, j, 0, 0)),
            pl.BlockSpec((None, None, SSM_TILE_GROUPS, 2 * SSM_STATE), lambda j: (layer, j, 0, 0)),
            pl.BlockSpec((None, 1, LANES), lambda j: (layer, 0, j)),
        ],
        out_specs=pl.BlockSpec((n, LANES), lambda j: (0, j)),
        out_shape=jax.ShapeDtypeStruct((n, SSM_WIDTH), F32),
        scratch_shapes=[
            pltpu.VMEM((n_chunks, flat), BF16),
            pltpu.VMEM((flat, 2 * LANES), BF16),
            pltpu.VMEM((flat, SSM_TILE_GROUPS * 2 * SSM_STATE), BF16),
            pltpu.VMEM((flat, SSM_TILE_GROUPS * 2 * SSM_STATE), BF16),
            pltpu.VMEM((n_chunks * SSM_TILE_GROUPS, 2 * SSM_STATE), F32),
            pltpu.VMEM((n_chunks * SSM_TILE_GROUPS, 2 * SSM_STATE), F32),
            pltpu.VMEM((n_chunks * SSM_TILE_GROUPS, 2 * SSM_STATE), F32),
        ],
        compiler_params=_params(("parallel",)),
        name="ssm",
    )(s_in, lag_c, inp, outp, a1, a2, d_row)


MERGE_ROW_CHUNK = 256


def _gate_merge_body(xn_ref, swa_ref, ssm_ref, mem_ref, wg0_ref, wg1_ref, wg2_ref, wswa_ref,
                     wga_ref, wgb_ref, wmem_ref, o_ref):
    tn = o_ref.shape[1]
    half_k = D_MODEL // 2
    wg01 = jnp.concatenate([wg0_ref[...].astype(BF16), wg1_ref[...].astype(BF16)], axis=1)
    wg2 = wg2_ref[...].astype(BF16)
    wglu = jnp.concatenate([wga_ref[...].astype(BF16), wgb_ref[...].astype(BF16)], axis=1)
    wswa = wswa_ref[...].astype(BF16)
    wmem = wmem_ref[...].astype(BF16)
    for r in range(o_ref.shape[0] // MERGE_ROW_CHUNK):
        rows = pl.ds(r * MERGE_ROW_CHUNK, MERGE_ROW_CHUNK)
        xn = xn_ref[rows, :]
        ys = ssm_ref[rows, :].astype(BF16)
        y_swa = jnp.dot(swa_ref[rows, :], wswa, preferred_element_type=F32)
        y_mem = jnp.dot(mem_ref[rows, :], wmem, preferred_element_type=F32)
        glu = jnp.dot(ys, wglu, preferred_element_type=F32)
        y_ssm = glu[:, :tn] * _sigmoid(glu[:, tn:])
        g01 = jnp.dot(xn, wg01, preferred_element_type=F32)
        g2 = (jnp.dot(xn[:, :half_k], wg2[:half_k], preferred_element_type=F32)
              + jnp.dot(xn[:, half_k:], wg2[half_k:], preferred_element_type=F32))
        merged = _sigmoid(g01[:, :tn]) * y_swa + _sigmoid(g01[:, tn:]) * y_ssm + _sigmoid(g2) * y_mem
        o_ref[rows, :] = merged.astype(BF16)


def _gate_merge(xn, o_swa, y_s, o_mem, w_in, layer, w_swa_up, w_ssm_glu, w_mem_up, *, tm=1024, tn=256):
    n = xn.shape[0]
    nj = D_MODEL // tn
    g0 = GATE_OFFSET // tn
    row = lambda i, j: (i, 0)
    return pl.pallas_call(
        _gate_merge_body,
        grid=(n // tm, nj),
        in_specs=[
            pl.BlockSpec((tm, D_MODEL), row),
            pl.BlockSpec((tm, Q_WIDTH), row),
            pl.BlockSpec((tm, SSM_WIDTH), row),
            pl.BlockSpec((tm, MEM_WIDTH), row),
            pl.BlockSpec((None, D_MODEL, tn), lambda i, j: (layer, 0, g0 + j)),
            pl.BlockSpec((None, D_MODEL, tn), lambda i, j: (layer, 0, g0 + j + nj)),
            pl.BlockSpec((None, D_MODEL, tn), lambda i, j: (layer, 0, g0 + j + 2 * nj)),
            pl.BlockSpec((None, Q_WIDTH, tn), lambda i, j: (layer, 0, j)),
            pl.BlockSpec((None, SSM_WIDTH, tn), lambda i, j: (layer, 0, j)),
            pl.BlockSpec((None, SSM_WIDTH, tn), lambda i, j: (layer, 0, j + nj)),
            pl.BlockSpec((None, MEM_WIDTH, tn), lambda i, j: (layer, 0, j)),
        ],
        out_specs=pl.BlockSpec((tm, tn), lambda i, j: (i, j)),
        out_shape=jax.ShapeDtypeStruct((n, D_MODEL), BF16),
        compiler_params=_params(("parallel", "arbitrary")),
        name="gate_merge",
    )(xn, o_swa, y_s, o_mem, w_in, w_in, w_in, w_swa_up, w_ssm_glu, w_ssm_glu, w_mem_up)


def _out_proj_body(h_ref, m_ref, w_ref, o_ref, wb_ref):
    @pl.when(pl.program_id(0) == 0)
    def _():
        wb_ref[...] = w_ref[...].astype(BF16)

    half = D_MODEL // 2
    for c in range(2):
        cols = pl.ds(c * half, half)
        o_ref[:, cols] = h_ref[:, cols] + jnp.dot(m_ref[...], wb_ref[:, cols], preferred_element_type=F32)


def _out_proj(h, merged, w_out, layer, *, tm=512):
    n = h.shape[0]
    return pl.pallas_call(
        _out_proj_body,
        grid=(n // tm,),
        in_specs=[
            pl.BlockSpec((tm, D_MODEL), lambda i: (i, 0)),
            pl.BlockSpec((tm, D_MODEL), lambda i: (i, 0)),
            pl.BlockSpec((None, D_MODEL, D_MODEL), lambda i: (layer, 0, 0), pipeline_mode=pl.Buffered(1)),
        ],
        out_specs=pl.BlockSpec((tm, D_MODEL), lambda i: (i, 0)),
        out_shape=jax.ShapeDtypeStruct((n, D_MODEL), F32),
        scratch_shapes=[pltpu.VMEM((D_MODEL, D_MODEL), BF16)],
        compiler_params=_params(("arbitrary",)),
        name="out_proj",
    )(h, merged, w_out)


def kernel(x, mem, ffn1_norm, ffn1_w_in, ffn1_w_out, mix_norm, mem_norm, w_in, sinks, w_mem_kv, lam_re, lam_im, log_dt, b_re, b_im, c_re, c_im, d_skip, w_ssm_glu, w_swa_up, w_mem_up, w_out, ffn2_norm, ffn2_w_in, ffn2_w_out, final_norm):
    batch, seq = x.shape[0], x.shape[1]
    n = batch * seq
    h = x.reshape(n, D_MODEL)
    mem2 = mem.reshape(batch * N_MEM, D_MODEL)
    final_w = final_norm.reshape(1, D_MODEL)
    ssm_ops = jax.vmap(_ssm_operators)(lam_re, lam_im, log_dt, b_re, b_im, c_re, c_im, d_skip)
    mem_kv = _mem_kv(mem2, mem_norm, w_mem_kv)
    for l in range(DEPTH):
        mix_w = mix_norm[l].reshape(1, D_MODEL)

        h = _ffn(h, ffn1_norm[l].reshape(1, D_MODEL), ffn1_w_in, ffn1_w_out, final_w, l, apply_final_norm=False)
        qkv, s_in, mq, xn = _mix_proj(h, mix_w, w_in, l)
        o_swa = _swa(qkv, sinks[l], batch, seq)
        o_mem = _mem_attn(mq, mem_kv, l, batch, seq)
        y_s = _ssm(s_in, ssm_ops, l, batch)
        merged = _gate_merge(xn, o_swa, y_s, o_mem, w_in, l, w_swa_up, w_ssm_glu, w_mem_up)
        h = _out_proj(h, merged, w_out, l)
        h = _ffn(h, ffn2_norm[l].reshape(1, D_MODEL), ffn2_w_in, ffn2_w_out, final_w, l,
                 apply_final_norm=(l == DEPTH - 1))
    return h.reshape(batch, seq, D_MODEL)
```

```python
import functools

import jax
import jax.numpy as jnp
from jax import lax
from jax.experimental import pallas as pl
from jax.experimental.pallas import tpu as pltpu

D_MODEL = 2048
DEPTH = 4
N_MEM = 256
D_FF = 5632
RMS_EPS = 1e-5

WINDOW = 128
HEAD_DIM = 64
N_Q_HEADS = 16
N_KV_HEADS = 4
GQA_REP = N_Q_HEADS // N_KV_HEADS
Q_WIDTH = N_Q_HEADS * HEAD_DIM
KV_WIDTH = N_KV_HEADS * HEAD_DIM

SSM_WIDTH = 1024
SSM_GROUP = 16
SSM_GROUPS = SSM_WIDTH // SSM_GROUP
SSM_STATE = 64
SSM_CHUNK = 16
LANES = 128
SSM_LANE_TILES = SSM_WIDTH // LANES
SSM_TILE_GROUPS = LANES // SSM_GROUP

MEM_HEADS = 4
MEM_HEAD_DIM = 256
MEM_WIDTH = MEM_HEADS * MEM_HEAD_DIM

NEG_INF = -1e30

QKV_WIDTH = Q_WIDTH + 2 * KV_WIDTH
SSM_OFFSET = QKV_WIDTH
MEMQ_OFFSET = SSM_OFFSET + SSM_WIDTH
GATE_OFFSET = MEMQ_OFFSET + MEM_WIDTH

VMEM_LIMIT_BYTES = 56 * 1024 * 1024

BF16 = jnp.bfloat16
F32 = jnp.float32


def _params(semantics):
    return pltpu.CompilerParams(dimension_semantics=semantics, vmem_limit_bytes=VMEM_LIMIT_BYTES)


def _rms_normalize(x, w):
    ms = jnp.mean(x * x, axis=-1, keepdims=True)
    return (x * lax.rsqrt(ms + RMS_EPS)) * w


def _sigmoid(x):
    return 0.5 * jnp.tanh(0.5 * x) + 0.5


FFN_ROW_CHUNK = 1024


class _RowTileIO:
    def __init__(self, h_hbm, out_hbm, hbuf, acc_ref, sems):
        self.h_hbm, self.out_hbm, self.hbuf, self.acc_ref, self.sems = h_hbm, out_hbm, hbuf, acc_ref, sems
        self.tm = acc_ref.shape[0]

    def _fetch(self, tile):
        return pltpu.make_async_copy(self.h_hbm.at[pl.ds(tile * self.tm, self.tm), :], self.hbuf, self.sems.at[0])

    def _write_back(self, tile):
        return pltpu.make_async_copy(self.acc_ref, self.out_hbm.at[pl.ds(tile * self.tm, self.tm), :],
                                     self.sems.at[1])

    def begin(self, xn_ref, nw_ref):
        i = pl.program_id(0)

        @pl.when(i == 0)
        def _():
            self._fetch(0).start()

        self._fetch(i).wait()
        xn_ref[...] = _rms_normalize(self.hbuf[...], nw_ref[...]).astype(BF16)

        @pl.when(i > 0)
        def _():
            self._write_back(i - 1).wait()

        self.acc_ref[...] = self.hbuf[...]

        @pl.when(i + 1 < pl.num_programs(0))
        def _():
            self._fetch(i + 1).start()

    def end(self):
        i = pl.program_id(0)
        self._write_back(i).start()

        @pl.when(i == pl.num_programs(0) - 1)
        def _():
            self._write_back(i).wait()


def _ffn_body(h_hbm, nw_ref, wg_ref, wu_ref, wo_ref, fw_ref, out_hbm, xn_ref, hbuf, acc_ref, sems, *,
              apply_final_norm):
    j = pl.program_id(1)
    io = _RowTileIO(h_hbm, out_hbm, hbuf, acc_ref, sems)

    @pl.when(j == 0)
    def _():
        io.begin(xn_ref, nw_ref)

    wg = wg_ref[...].astype(BF16)
    wu = wu_ref[...].astype(BF16)
    wo = wo_ref[...].astype(BF16)
    for r in range(acc_ref.shape[0] // FFN_ROW_CHUNK):
        rows = pl.ds(r * FFN_ROW_CHUNK, FFN_ROW_CHUNK)
        xn = xn_ref[rows, :]
        g = jnp.dot(xn, wg, preferred_element_type=F32)
        u = jnp.dot(xn, wu, preferred_element_type=F32)
        a = ((0.5 * g) * _sigmoid(g)) * u
        acc_ref[rows, :] += jnp.dot(a.astype(BF16), wo, preferred_element_type=F32)

    @pl.when(j == pl.num_programs(1) - 1)
    def _():
        if apply_final_norm:
            acc_ref[...] = _rms_normalize(acc_ref[...], fw_ref[...])
        io.end()


def _ffn(h, norm_w, w_in, w_out, final_w, layer, *, apply_final_norm, tm=1024, tf=512):
    n = h.shape[0]
    nf = D_FF // tf
    return pl.pallas_call(
        functools.partial(_ffn_body, apply_final_norm=apply_final_norm),
        grid=(n // tm, nf),
        in_specs=[
            pl.BlockSpec(memory_space=pl.ANY),
            pl.BlockSpec((1, D_MODEL), lambda i, j: (0, 0)),
            pl.BlockSpec((None, D_MODEL, tf), lambda i, j: (layer, 0, j)),
            pl.BlockSpec((None, D_MODEL, tf), lambda i, j: (layer, 0, j + nf)),
            pl.BlockSpec((None, tf, D_MODEL), lambda i, j: (layer, j, 0)),
            pl.BlockSpec((1, D_MODEL), lambda i, j: (0, 0)),
        ],
        out_specs=pl.BlockSpec(memory_space=pl.ANY),
        out_shape=jax.ShapeDtypeStruct((n, D_MODEL), F32),
        scratch_shapes=[pltpu.VMEM((tm, D_MODEL), BF16), pltpu.VMEM((tm, D_MODEL), F32),
                        pltpu.VMEM((tm, D_MODEL), F32), pltpu.SemaphoreType.DMA((2,))],
        compiler_params=_params(("arbitrary", "arbitrary")),
        name="ffn",
    )(h, norm_w, w_in, w_in, w_out, final_w)


def _mem_kv_body(m_ref, nw_ref, w_ref, o_ref, xn_ref):
    @pl.when(pl.program_id(1) == 0)
    def _():
        xn_ref[...] = _rms_normalize(m_ref[...], nw_ref[...]).astype(BF16)

    o_ref[...] = jnp.dot(xn_ref[...], w_ref[...].astype(BF16), preferred_element_type=F32).astype(BF16)


def _mem_kv(mem2, mem_norm, w_mem_kv, *, tn=512):
    rows = mem2.shape[0]
    width = 2 * MEM_WIDTH
    return pl.pallas_call(
        _mem_kv_body,
        grid=(DEPTH, width // tn),
        in_specs=[
            pl.BlockSpec((rows, D_MODEL), lambda l, j: (0, 0)),
            pl.BlockSpec((None, 1, D_MODEL), lambda l, j: (l, 0, 0)),
            pl.BlockSpec((None, D_MODEL, tn), lambda l, j: (l, 0, j)),
        ],
        out_specs=pl.BlockSpec((None, rows, tn), lambda l, j: (l, 0, j)),
        out_shape=jax.ShapeDtypeStruct((DEPTH, rows, width), BF16),
        scratch_shapes=[pltpu.VMEM((rows, D_MODEL), BF16)],
        compiler_params=_params(("arbitrary", "arbitrary")),
        name="mem_kv",
    )(mem2, mem_norm.reshape(DEPTH, 1, D_MODEL), w_mem_kv)


def _mix_proj_body(h_hbm, nw_ref, w_ref, qkv_ref, s_ref, mq_ref, xn_ref, hbuf, sem, *, n_qkv, n_ssm):
    i, j = pl.program_id(0), pl.program_id(1)
    tm = hbuf.shape[0]

    def fetch(tile):
        return pltpu.make_async_copy(h_hbm.at[pl.ds(tile * tm, tm), :], hbuf, sem.at[0])

    @pl.when(j == 0)
    def _():
        @pl.when(i == 0)
        def _():
            fetch(0).start()

        fetch(i).wait()
        xn_ref[...] = _rms_normalize(hbuf[...], nw_ref[...]).astype(BF16)

        @pl.when(i + 1 < pl.num_programs(0))
        def _():
            fetch(i + 1).start()

    def project(out_ref):
        out_ref[...] = jnp.dot(xn_ref[...], w_ref[...].astype(BF16), preferred_element_type=F32).astype(out_ref.dtype)

    @pl.when(j < n_qkv)
    def _():
        project(qkv_ref)

    @pl.when((j >= n_qkv) & (j < n_qkv + n_ssm))
    def _():
        project(s_ref)

    @pl.when(j >= n_qkv + n_ssm)
    def _():
        project(mq_ref)


def _mix_proj(h, norm_w, w_in, layer, *, tm=2048, tn=512):
    n = h.shape[0]
    n_qkv, n_ssm, n_mq = QKV_WIDTH // tn, SSM_WIDTH // tn, MEM_WIDTH // tn
    return pl.pallas_call(
        functools.partial(_mix_proj_body, n_qkv=n_qkv, n_ssm=n_ssm),
        grid=(n // tm, n_qkv + n_ssm + n_mq),
        in_specs=[
            pl.BlockSpec(memory_space=pl.ANY),
            pl.BlockSpec((1, D_MODEL), lambda i, j: (0, 0)),
            pl.BlockSpec((None, D_MODEL, tn), lambda i, j: (layer, 0, j)),
        ],
        out_specs=[
            pl.BlockSpec((tm, tn), lambda i, j: (i, jnp.minimum(j, n_qkv - 1))),
            pl.BlockSpec((tm, tn), lambda i, j: (i, jnp.clip(j - n_qkv, 0, n_ssm - 1))),
            pl.BlockSpec((tm, tn), lambda i, j: (i, jnp.clip(j - n_qkv - n_ssm, 0, n_mq - 1))),
            pl.BlockSpec((tm, D_MODEL), lambda i, j: (i, 0), pipeline_mode=pl.Buffered(1)),
        ],
        out_shape=[
            jax.ShapeDtypeStruct((n, QKV_WIDTH), BF16),
            jax.ShapeDtypeStruct((n, SSM_WIDTH), F32),
            jax.ShapeDtypeStruct((n, MEM_WIDTH), BF16),
            jax.ShapeDtypeStruct((n, D_MODEL), BF16),
        ],
        scratch_shapes=[pltpu.VMEM((tm, D_MODEL), F32), pltpu.SemaphoreType.DMA((1,))],
        compiler_params=_params(("arbitrary", "arbitrary")),
        name="mix_proj",
    )(h, norm_w, w_in)


SWA_STEP_BLOCKS = 4


def _swa_body(sinks_ref, q_ref, kvc_ref, kvp_ref, o_ref, s_ref, p_ref, den_ref):
    step = pl.program_id(1)
    for t in range(SWA_STEP_BLOCKS):
        rows = slice(t * WINDOW, (t + 1) * WINDOW)
        if t == 0:
            kvp, first_key = kvp_ref[...], jnp.where(step > 0, 0, WINDOW)
        else:
            kvp, first_key = kvc_ref[(t - 1) * WINDOW:t * WINDOW, :], 0
        _swa_block(sinks_ref, q_ref[rows, :], kvc_ref[rows, :], kvp, first_key, o_ref, rows, s_ref, p_ref, den_ref)


def _swa_block(sinks_ref, q, kvc, kvp, first_key, o_ref, rows, s_ref, p_ref, den_ref):
    qi = lax.broadcasted_iota(jnp.int32, (WINDOW, 2 * WINDOW), 0)
    kj = lax.broadcasted_iota(jnp.int32, (WINDOW, 2 * WINDOW), 1)
    valid = (kj > qi) & (kj <= qi + WINDOW) & (kj >= first_key)
    scale = HEAD_DIM ** -0.5
    vals = []
    for g in range(N_KV_HEADS):
        ks = slice(g * HEAD_DIM, (g + 1) * HEAD_DIM)
        vs = slice(KV_WIDTH + g * HEAD_DIM, KV_WIDTH + (g + 1) * HEAD_DIM)
        k = jnp.concatenate([kvp[:, ks], kvc[:, ks]], axis=0)
        vals.append(jnp.concatenate([kvp[:, vs], kvc[:, vs]], axis=0))
        for r in range(GQA_REP):
            h = g * GQA_REP + r
            qh = q[:, h * HEAD_DIM:(h + 1) * HEAD_DIM]
            s_ref[h] = lax.dot_general(qh, k, (((1,), (1,)), ((), ())), preferred_element_type=F32)
    for h in range(N_Q_HEADS):
        s = jnp.where(valid, s_ref[h] * scale, NEG_INF)
        sink = sinks_ref[h]
        m = jnp.maximum(jnp.max(s, axis=-1, keepdims=True), sink)
        den_ref[h] = jnp.broadcast_to(jnp.exp(sink - m), (WINDOW, LANES))
        p_ref[h] = jnp.exp(s - m).astype(BF16)
    ones = jnp.ones((2 * WINDOW, LANES), BF16)
    for h in range(N_Q_HEADS):
        p = p_ref[h]
        den = jnp.dot(p, ones, preferred_element_type=F32) + den_ref[h]
        o = jnp.dot(p, vals[h // GQA_REP], preferred_element_type=F32) / den[:, :HEAD_DIM]
        o_ref[rows, h * HEAD_DIM:(h + 1) * HEAD_DIM] = o.astype(BF16)


def _swa(qkv, sinks, batch, seq):
    nb = seq // WINDOW
    assert seq % (SWA_STEP_BLOCKS * WINDOW) == 0, seq
    ns = nb // SWA_STEP_BLOCKS
    span = SWA_STEP_BLOCKS * WINDOW
    kv_col = Q_WIDTH // (2 * KV_WIDTH)
    return pl.pallas_call(
        _swa_body,
        grid=(batch, ns),
        in_specs=[
            pl.BlockSpec(memory_space=pltpu.SMEM),
            pl.BlockSpec((span, Q_WIDTH), lambda b, n: (b * ns + n, 0)),
            pl.BlockSpec((span, 2 * KV_WIDTH), lambda b, n: (b * ns + n, kv_col)),
            pl.BlockSpec((WINDOW, 2 * KV_WIDTH),
                         lambda b, n: (b * nb + jnp.maximum(n * SWA_STEP_BLOCKS - 1, 0), kv_col)),
        ],
        out_specs=pl.BlockSpec((span, Q_WIDTH), lambda b, n: (b * ns + n, 0)),
        out_shape=jax.ShapeDtypeStruct((batch * seq, Q_WIDTH), BF16),
        scratch_shapes=[pltpu.VMEM((N_Q_HEADS, WINDOW, 2 * WINDOW), F32),
                        pltpu.VMEM((N_Q_HEADS, WINDOW, 2 * WINDOW), BF16),
                        pltpu.VMEM((N_Q_HEADS, WINDOW, LANES), F32)],
        compiler_params=_params(("parallel", "arbitrary")),
        name="swa",
    )(sinks, qkv, qkv, qkv)


def _mem_attn_body(q_ref, kv_ref, o_ref):
    scale = MEM_HEAD_DIM ** -0.5
    for h in range(MEM_HEADS):
        cs = slice(h * MEM_HEAD_DIM, (h + 1) * MEM_HEAD_DIM)
        vs = slice(MEM_WIDTH + h * MEM_HEAD_DIM, MEM_WIDTH + (h + 1) * MEM_HEAD_DIM)
        s = lax.dot_general(q_ref[:, cs], kv_ref[:, cs], (((1,), (1,)), ((), ())),
                            preferred_element_type=F32) * scale
        m = jnp.max(s, axis=-1, keepdims=True)
        p = jnp.exp(s - m)
        denom = jnp.sum(p, axis=-1, keepdims=True)
        o = jnp.dot(p.astype(BF16), kv_ref[:, vs], preferred_element_type=F32) / denom
        o_ref[:, cs] = o.astype(BF16)


def _mem_attn(mq, mem_kv, layer, batch, seq, *, tq=512):
    nq = seq // tq
    return pl.pallas_call(
        _mem_attn_body,
        grid=(batch, nq),
        in_specs=[
            pl.BlockSpec((tq, MEM_WIDTH), lambda b, i: (b * nq + i, 0)),
            pl.BlockSpec((None, N_MEM, 2 * MEM_WIDTH), lambda b, i: (layer, b, 0)),
        ],
        out_specs=pl.BlockSpec((tq, MEM_WIDTH), lambda b, i: (b * nq + i, 0)),
        out_shape=jax.ShapeDtypeStruct((batch * seq, MEM_WIDTH), BF16),
        compiler_params=_params(("parallel", "arbitrary")),
        name="mem_attn",
    )(mq, mem_kv)


def _ssm_operators(lam_re, lam_im, log_dt, b_re, b_im, c_re, c_im, d_skip):
    hp = lax.Precision.HIGHEST
    t_len, g_n, p_n, ch = SSM_CHUNK, SSM_GROUPS, SSM_STATE, SSM_GROUP
    lr = jnp.minimum(lam_re, -1e-4)
    li = lam_im
    dt = jnp.exp(log_dt)[:, None]
    mag = jnp.exp(lr * dt)
    ar = mag * jnp.cos(li * dt)
    ai = mag * jnp.sin(li * dt)
    nr, ni = ar - 1.0, ai
    den = lr * lr + li * li
    kr = (nr * lr + ni * li) / den
    ki = (ni * lr - nr * li) / den
    bbr = kr[..., None] * b_re - ki[..., None] * b_im
    bbi = kr[..., None] * b_im + ki[..., None] * b_re
    steps = jnp.arange(t_len + 1, dtype=F32)[None, :, None]
    pmag = jnp.exp(steps * (lr * dt)[:, None, :])
    ang = steps * (li * dt)[:, None, :]
    pr = pmag * jnp.cos(ang)
    pi = pmag * jnp.sin(ang)
    wr = pr[:, :t_len, :, None] * bbr[:, None] - pi[:, :t_len, :, None] * bbi[:, None]
    wi = pr[:, :t_len, :, None] * bbi[:, None] + pi[:, :t_len, :, None] * bbr[:, None]
    lagk = (jnp.einsum('gcp,gkpd->gkcd', c_re, wr, precision=hp)
            - jnp.einsum('gcp,gkpd->gkcd', c_im, wi, precision=hp))
    nt, gt = SSM_LANE_TILES, SSM_TILE_GROUPS
    lag_c = lagk.reshape(nt, gt, t_len, ch, ch).transpose(0, 2, 1, 4, 3).reshape(nt, t_len * LANES, ch)

    def pair(x, y):
        return jnp.concatenate([x, y], axis=-1)

    def per_step(w):
        return w.reshape(nt, gt, t_len, 2 * p_n).transpose(0, 2, 1, 3)[:, :, :, None, :]

    def per_channel(w):
        return w.reshape(nt, gt, ch, 2 * p_n)[:, None]

    back = jnp.arange(t_len - 1, -1, -1, dtype=F32)[None, :, None]
    bmag = jnp.exp(back * (lr * dt)[:, None, :])
    bang = back * (li * dt)[:, None, :]
    qr, qi = bmag * jnp.cos(bang), bmag * jnp.sin(bang)
    bbr_t, bbi_t = bbr.transpose(0, 2, 1), bbi.transpose(0, 2, 1)
    inp = (per_step(pair(qr, qr)) * per_channel(pair(bbr_t, bbi_t))
           + per_step(pair(-qi, qi)) * per_channel(pair(bbi_t, bbr_t))
           ).reshape(nt, t_len * LANES, 2 * p_n)
    pr1, pi1 = pr[:, 1:], pi[:, 1:]
    outp = (per_step(pair(pr1, pi1)) * per_channel(pair(c_re, -c_re))
            - per_step(pair(pi1, pr1)) * per_channel(pair(c_im, c_im))
            ).reshape(nt, t_len * LANES, 2 * p_n)
    a1 = jnp.concatenate([pr[:, t_len], pr[:, t_len]], axis=-1).reshape(nt, gt, 2 * p_n)
    a2 = jnp.concatenate([-pi[:, t_len], pi[:, t_len]], axis=-1).reshape(nt, gt, 2 * p_n)
    return lag_c, inp, outp, a1, a2, d_skip.reshape(1, SSM_WIDTH)


def _ssm_body(s_ref, lag_ref, inp_ref, outp_ref, a1_ref, a2_ref, d_ref, y_ref,
              u_ref, panel_ref, inpx_ref, outpx_ref, z_ref, zs_ref, sp_ref, *, batch):
    t_len, gt = SSM_CHUNK, SSM_TILE_GROUPS
    n_chunks = u_ref.shape[0]
    per_seq = n_chunks // batch
    flat = t_len * LANES

    for t in range(t_len):
        u_ref[:, t * LANES:(t + 1) * LANES] = s_ref[pl.ds(t, n_chunks, stride=t_len), :].astype(BF16)

    row_group = (lax.broadcasted_iota(jnp.int32, (flat, 1), 0) >> 4) & (gt - 1)
    col_group = lax.broadcasted_iota(jnp.int32, (1, LANES), 1) >> 4
    spread = (lax.broadcasted_iota(jnp.int32, (SSM_GROUP, LANES), 1) & (SSM_GROUP - 1)
              == lax.broadcasted_iota(jnp.int32, (SSM_GROUP, LANES), 0)).astype(BF16)
    lag = jnp.dot(lag_ref[...].astype(BF16), spread, preferred_element_type=F32)
    lag = jnp.where(row_group == col_group, lag, 0.0).astype(BF16)

    for r in range(t_len):
        k_left, k_right = t_len - 2 - r, t_len - 1 - r
        left = lag[k_left * LANES:(k_left + 1) * LANES] if k_left >= 0 else jnp.zeros((LANES, LANES), BF16)
        panel_ref[r * LANES:(r + 1) * LANES, :LANES] = left
        panel_ref[r * LANES:(r + 1) * LANES, LANES:] = lag[k_right * LANES:(k_right + 1) * LANES]

    inp = inp_ref[...]
    outp = outp_ref[...]
    for g in range(gt):
        inpx_ref[:, g * LANES:(g + 1) * LANES] = jnp.where(row_group == g, inp, 0.0).astype(BF16)
        outpx_ref[:, g * LANES:(g + 1) * LANES] = jnp.where(row_group == g, outp, 0.0).astype(BF16)

    z = jnp.dot(u_ref[...], inpx_ref[...], preferred_element_type=F32)
    for g in range(gt):
        zg = z[:, g * LANES:(g + 1) * LANES]
        z_ref[pl.ds(g, n_chunks, stride=gt), :] = zg
        zs_ref[pl.ds(g, n_chunks, stride=gt), :] = pltpu.roll(zg, SSM_STATE, axis=1)

    a1 = a1_ref[...]
    a2 = a2_ref[...]

    def step(c, carry):
        new = []
        for b in range(batch):
            v0, v1 = carry[b]
            row = pl.multiple_of((b * per_seq + c) * gt, gt)
            sp_ref[pl.ds(row, gt), :] = v0
            z0 = z_ref[pl.ds(row, gt), :]
            z1 = zs_ref[pl.ds(row, gt), :]
            new.append((a1 * v0 + a2 * v1 + z0, a1 * v1 - a2 * v0 + z1))
        return tuple(new)

    zero = jnp.zeros((gt, LANES), F32)
    lax.fori_loop(0, per_seq, step, tuple((zero, zero) for _ in range(batch)), unroll=4)

    sp = jnp.concatenate([sp_ref[pl.ds(g, n_chunks, stride=gt), :] for g in range(gt)], axis=1).astype(BF16)
    d2 = jnp.concatenate([d_ref[...], d_ref[...]], axis=1)
    for q in range(t_len // 2):
        cols = slice(2 * q * LANES, (2 * q + 2) * LANES)
        k_len = (2 * q + 2) * LANES
        y = jnp.dot(u_ref[:, :k_len], panel_ref[flat - k_len:, :], preferred_element_type=F32)
        y = y + lax.dot_general(sp, outpx_ref[cols, :], (((1,), (1,)), ((), ())), preferred_element_type=F32)
        y = y + d2 * u_ref[:, cols].astype(F32)
        y_ref[pl.ds(2 * q, n_chunks, stride=t_len), :] = y[:, :LANES]
        y_ref[pl.ds(2 * q + 1, n_chunks, stride=t_len), :] = y[:, LANES:]


def _ssm(s_in, ops, layer, batch):
    lag_c, inp, outp, a1, a2, d_row = ops
    n = s_in.shape[0]
    n_chunks = n // SSM_CHUNK
    flat = SSM_CHUNK * LANES
    return pl.pallas_call(
        functools.partial(_ssm_body, batch=batch),
        grid=(SSM_LANE_TILES,),
        in_specs=[
            pl.BlockSpec((n, LANES), lambda j: (0, j)),
            pl.BlockSpec((None, None, flat, SSM_GROUP), lambda j: (layer, j, 0, 0)),
            pl.BlockSpec((None, None, flat, 2 * SSM_STATE), lambda j: (layer, j, 0, 0)),
            pl.BlockSpec((None, None, flat, 2 * SSM_STATE), lambda j: (layer, j, 0, 0)),
            pl.BlockSpec((None, None, SSM_TILE_GROUPS, 2 * SSM_STATE), lambda j: (layer, j, 0, 0)),
            pl.BlockSpec((None, None, SSM_TILE_GROUPS, 2 * SSM_STATE), lambda j: (layer, j, 0, 0)),
            pl.BlockSpec((None, 1, LANES), lambda j: (layer, 0, j)),
        ],
        out_specs=pl.BlockSpec((n, LANES), lambda j: (0, j)),
        out_shape=jax.ShapeDtypeStruct((n, SSM_WIDTH), F32),
        scratch_shapes=[
            pltpu.VMEM((n_chunks, flat), BF16),
            pltpu.VMEM((flat, 2 * LANES), BF16),
            pltpu.VMEM((flat, SSM_TILE_GROUPS * 2 * SSM_STATE), BF16),
            pltpu.VMEM((flat, SSM_TILE_GROUPS * 2 * SSM_STATE), BF16),
            pltpu.VMEM((n_chunks * SSM_TILE_GROUPS, 2 * SSM_STATE), F32),
            pltpu.VMEM((n_chunks * SSM_TILE_GROUPS, 2 * SSM_STATE), F32),
            pltpu.VMEM((n_chunks * SSM_TILE_GROUPS, 2 * SSM_STATE), F32),
        ],
        compiler_params=_params(("parallel",)),
        name="ssm",
    )(s_in, lag_c, inp, outp, a1, a2, d_row)


MERGE_ROW_CHUNK = 256


def _gate_merge_body(xn_ref, swa_ref, ssm_ref, mem_ref, wg0_ref, wg1_ref, wg2_ref, wswa_ref,
                     wga_ref, wgb_ref, wmem_ref, o_ref):
    tn = o_ref.shape[1]
    half_k = D_MODEL // 2
    wg01 = jnp.concatenate([wg0_ref[...].astype(BF16), wg1_ref[...].astype(BF16)], axis=1)
    wg2 = wg2_ref[...].astype(BF16)
    wglu = jnp.concatenate([wga_ref[...].astype(BF16), wgb_ref[...].astype(BF16)], axis=1)
    wswa = wswa_ref[...].astype(BF16)
    wmem = wmem_ref[...].astype(BF16)
    for r in range(o_ref.shape[0] // MERGE_ROW_CHUNK):
        rows = pl.ds(r * MERGE_ROW_CHUNK, MERGE_ROW_CHUNK)
        xn = xn_ref[rows, :]
        ys = ssm_ref[rows, :].astype(BF16)
        y_swa = jnp.dot(swa_ref[rows, :], wswa, preferred_element_type=F32)
        y_mem = jnp.dot(mem_ref[rows, :], wmem, preferred_element_type=F32)
        glu = jnp.dot(ys, wglu, preferred_element_type=F32)
        y_ssm = glu[:, :tn] * _sigmoid(glu[:, tn:])
        g01 = jnp.dot(xn, wg01, preferred_element_type=F32)
        g2 = (jnp.dot(xn[:, :half_k], wg2[:half_k], preferred_element_type=F32)
              + jnp.dot(xn[:, half_k:], wg2[half_k:], preferred_element_type=F32))
        merged = _sigmoid(g01[:, :tn]) * y_swa + _sigmoid(g01[:, tn:]) * y_ssm + _sigmoid(g2) * y_mem
        o_ref[rows, :] = merged.astype(BF16)


def _gate_merge(xn, o_swa, y_s, o_mem, w_in, layer, w_swa_up, w_ssm_glu, w_mem_up, *, tm=1024, tn=256):
    n = xn.shape[0]
    nj = D_MODEL // tn
    g0 = GATE_OFFSET // tn
    row = lambda i, j: (i, 0)
    return pl.pallas_call(
        _gate_merge_body,
        grid=(n // tm, nj),
        in_specs=[
            pl.BlockSpec((tm, D_MODEL), row),
            pl.BlockSpec((tm, Q_WIDTH), row),
            pl.BlockSpec((tm, SSM_WIDTH), row),
            pl.BlockSpec((tm, MEM_WIDTH), row),
            pl.BlockSpec((None, D_MODEL, tn), lambda i, j: (layer, 0, g0 + j)),
            pl.BlockSpec((None, D_MODEL, tn), lambda i, j: (layer, 0, g0 + j + nj)),
            pl.BlockSpec((None, D_MODEL, tn), lambda i, j: (layer, 0, g0 + j + 2 * nj)),
            pl.BlockSpec((None, Q_WIDTH, tn), lambda i, j: (layer, 0, j)),
            pl.BlockSpec((None, SSM_WIDTH, tn), lambda i, j: (layer, 0, j)),
            pl.BlockSpec((None, SSM_WIDTH, tn), lambda i, j: (layer, 0, j + nj)),
            pl.BlockSpec((None, MEM_WIDTH, tn), lambda i, j: (layer, 0, j)),
        ],
        out_specs=pl.BlockSpec((tm, tn), lambda i, j: (i, j)),
        out_shape=jax.ShapeDtypeStruct((n, D_MODEL), BF16),
        compiler_params=_params(("parallel", "arbitrary")),
        name="gate_merge",
    )(xn, o_swa, y_s, o_mem, w_in, w_in, w_in, w_swa_up, w_ssm_glu, w_ssm_glu, w_mem_up)


def _out_proj_body(h_ref, m_ref, w_ref, o_ref, wb_ref):
    @pl.when(pl.program_id(0) == 0)
    def _():
        wb_ref[...] = w_ref[...].astype(BF16)

    half = D_MODEL // 2
    for c in range(2):
        cols = pl.ds(c * half, half)
        o_ref[:, cols] = h_ref[:, cols] + jnp.dot(m_ref[...], wb_ref[:, cols], preferred_element_type=F32)


def _out_proj(h, merged, w_out, layer, *, tm=512):
    n = h.shape[0]
    return pl.pallas_call(
        _out_proj_body,
        grid=(n // tm,),
        in_specs=[
            pl.BlockSpec((tm, D_MODEL), lambda i: (i, 0)),
            pl.BlockSpec((tm, D_MODEL), lambda i: (i, 0)),
            pl.BlockSpec((None, D_MODEL, D_MODEL), lambda i: (layer, 0, 0), pipeline_mode=pl.Buffered(1)),
        ],
        out_specs=pl.BlockSpec((tm, D_MODEL), lambda i: (i, 0)),
        out_shape=jax.ShapeDtypeStruct((n, D_MODEL), F32),
        scratch_shapes=[pltpu.VMEM((D_MODEL, D_MODEL), BF16)],
        compiler_params=_params(("arbitrary",)),
        name="out_proj",
    )(h, merged, w_out)


def kernel(x, mem, ffn1_norm, ffn1_w_in, ffn1_w_out, mix_norm, mem_norm, w_in, sinks, w_mem_kv, lam_re, lam_im, log_dt, b_re, b_im, c_re, c_im, d_skip, w_ssm_glu, w_swa_up, w_mem_up, w_out, ffn2_norm, ffn2_w_in, ffn2_w_out, final_norm):
    batch, seq = x.shape[0], x.shape[1]
    n = batch * seq
    h = x.reshape(n, D_MODEL)
    mem2 = mem.reshape(batch * N_MEM, D_MODEL)
    final_w = final_norm.reshape(1, D_MODEL)
    ssm_ops = jax.vmap(_ssm_operators)(lam_re, lam_im, log_dt, b_re, b_im, c_re, c_im, d_skip)
    mem_kv = _mem_kv(mem2, mem_norm, w_mem_kv)
    for l in range(DEPTH):
        mix_w = mix_norm[l].reshape(1, D_MODEL)

        h = _ffn(h, ffn1_norm[l].reshape(1, D_MODEL), ffn1_w_in, ffn1_w_out, final_w, l, apply_final_norm=False)
        qkv, s_in, mq, xn = _mix_proj(h, mix_w, w_in, l)
        o_swa = _swa(qkv, sinks[l], batch, seq)
        o_mem = _mem_attn(mq, mem_kv, l, batch, seq)
        y_s = _ssm(s_in, ssm_ops, l, batch)
        merged = _gate_merge(xn, o_swa, y_s, o_mem, w_in, l, w_swa_up, w_ssm_glu, w_mem_up)
        h = _out_proj(h, merged, w_out, l)
        h = _ffn(h, ffn2_norm[l].reshape(1, D_MODEL), ffn2_w_in, ffn2_w_out, final_w, l,
                 apply_final_norm=(l == DEPTH - 1))
    return h.reshape(batch, seq, D_MODEL)
```

```python
import functools

import jax
import jax.numpy as jnp
from jax import lax
from jax.experimental import pallas as pl
from jax.experimental.pallas import tpu as pltpu

D_MODEL = 2048
DEPTH = 4
N_MEM = 256
D_FF = 5632
RMS_EPS = 1e-5

WINDOW = 128
HEAD_DIM = 64
N_Q_HEADS = 16
N_KV_HEADS = 4
GQA_REP = N_Q_HEADS // N_KV_HEADS
Q_WIDTH = N_Q_HEADS * HEAD_DIM
KV_WIDTH = N_KV_HEADS * HEAD_DIM

SSM_WIDTH = 1024
SSM_GROUP = 16
SSM_GROUPS = SSM_WIDTH // SSM_GROUP
SSM_STATE = 64
SSM_CHUNK = 16
LANES = 128
SSM_LANE_TILES = SSM_WIDTH // LANES
SSM_TILE_GROUPS = LANES // SSM_GROUP

MEM_HEADS = 4
MEM_HEAD_DIM = 256
MEM_WIDTH = MEM_HEADS * MEM_HEAD_DIM

NEG_INF = -1e30

QKV_WIDTH = Q_WIDTH + 2 * KV_WIDTH
SSM_OFFSET = QKV_WIDTH
MEMQ_OFFSET = SSM_OFFSET + SSM_WIDTH
GATE_OFFSET = MEMQ_OFFSET + MEM_WIDTH

VMEM_LIMIT_BYTES = 56 * 1024 * 1024

BF16 = jnp.bfloat16
F32 = jnp.float32


def _params(semantics):
    return pltpu.CompilerParams(dimension_semantics=semantics, vmem_limit_bytes=VMEM_LIMIT_BYTES)


def _rms_normalize(x, w):
    ms = jnp.mean(x * x, axis=-1, keepdims=True)
    return (x * lax.rsqrt(ms + RMS_EPS)) * w


def _sigmoid(x):
    return 0.5 * jnp.tanh(0.5 * x) + 0.5


FFN_ROW_CHUNK = 1024


class _RowTileIO:
    def __init__(self, h_hbm, out_hbm, hbuf, acc_ref, sems):
        self.h_hbm, self.out_hbm, self.hbuf, self.acc_ref, self.sems = h_hbm, out_hbm, hbuf, acc_ref, sems
        self.tm = acc_ref.shape[0]

    def _fetch(self, tile):
        return pltpu.make_async_copy(self.h_hbm.at[pl.ds(tile * self.tm, self.tm), :], self.hbuf, self.sems.at[0])

    def _write_back(self, tile):
        return pltpu.make_async_copy(self.acc_ref, self.out_hbm.at[pl.ds(tile * self.tm, self.tm), :],
                                     self.sems.at[1])

    def begin(self, xn_ref, nw_ref):
        i = pl.program_id(0)

        @pl.when(i == 0)
        def _():
            self._fetch(0).start()

        self._fetch(i).wait()
        xn_ref[...] = _rms_normalize(self.hbuf[...], nw_ref[...]).astype(BF16)

        @pl.when(i > 0)
        def _():
            self._write_back(i - 1).wait()

        self.acc_ref[...] = self.hbuf[...]

        @pl.when(i + 1 < pl.num_programs(0))
        def _():
            self._fetch(i + 1).start()

    def end(self):
        i = pl.program_id(0)
        self._write_back(i).start()

        @pl.when(i == pl.num_programs(0) - 1)
        def _():
            self._write_back(i).wait()


def _ffn_body(h_hbm, nw_ref, wg_ref, wu_ref, wo_ref, fw_ref, out_hbm, xn_ref, hbuf, acc_ref, sems, *,
              apply_final_norm):
    j = pl.program_id(1)
    io = _RowTileIO(h_hbm, out_hbm, hbuf, acc_ref, sems)

    @pl.when(j == 0)
    def _():
        io.begin(xn_ref, nw_ref)

    wg = wg_ref[...].astype(BF16)
    wu = wu_ref[...].astype(BF16)
    wo = wo_ref[...].astype(BF16)
    for r in range(acc_ref.shape[0] // FFN_ROW_CHUNK):
        rows = pl.ds(r * FFN_ROW_CHUNK, FFN_ROW_CHUNK)
        xn = xn_ref[rows, :]
        g = jnp.dot(xn, wg, preferred_element_type=F32)
        u = jnp.dot(xn, wu, preferred_element_type=F32)
        a = ((0.5 * g) * _sigmoid(g)) * u
        acc_ref[rows, :] += jnp.dot(a.astype(BF16), wo, preferred_element_type=F32)

    @pl.when(j == pl.num_programs(1) - 1)
    def _():
        if apply_final_norm:
            acc_ref[...] = _rms_normalize(acc_ref[...], fw_ref[...])
        io.end()


def _ffn(h, norm_w, w_in, w_out, final_w, layer, *, apply_final_norm, tm=1024, tf=512):
    n = h.shape[0]
    nf = D_FF // tf
    return pl.pallas_call(
        functools.partial(_ffn_body, apply_final_norm=apply_final_norm),
        grid=(n // tm, nf),
        in_specs=[
            pl.BlockSpec(memory_space=pl.ANY),
            pl.BlockSpec((1, D_MODEL), lambda i, j: (0, 0)),
            pl.BlockSpec((None, D_MODEL, tf), lambda i, j: (layer, 0, j)),
            pl.BlockSpec((None, D_MODEL, tf), lambda i, j: (layer, 0, j + nf)),
            pl.BlockSpec((None, tf, D_MODEL), lambda i, j: (layer, j, 0)),
            pl.BlockSpec((1, D_MODEL), lambda i, j: (0, 0)),
        ],
        out_specs=pl.BlockSpec(memory_space=pl.ANY),
        out_shape=jax.ShapeDtypeStruct((n, D_MODEL), F32),
        scratch_shapes=[pltpu.VMEM((tm, D_MODEL), BF16), pltpu.VMEM((tm, D_MODEL), F32),
                        pltpu.VMEM((tm, D_MODEL), F32), pltpu.SemaphoreType.DMA((2,))],
        compiler_params=_params(("arbitrary", "arbitrary")),
        name="ffn",
    )(h, norm_w, w_in, w_in, w_out, final_w)


def _mem_kv_body(m_ref, nw_ref, w_ref, o_ref, xn_ref):
    @pl.when(pl.program_id(1) == 0)
    def _():
        xn_ref[...] = _rms_normalize(m_ref[...], nw_ref[...]).astype(BF16)

    o_ref[...] = jnp.dot(xn_ref[...], w_ref[...].astype(BF16), preferred_element_type=F32).astype(BF16)


def _mem_kv(mem2, mem_norm, w_mem_kv, *, tn=512):
    rows = mem2.shape[0]
    width = 2 * MEM_WIDTH
    return pl.pallas_call(
        _mem_kv_body,
        grid=(DEPTH, width // tn),
        in_specs=[
            pl.BlockSpec((rows, D_MODEL), lambda l, j: (0, 0)),
            pl.BlockSpec((None, 1, D_MODEL), lambda l, j: (l, 0, 0)),
            pl.BlockSpec((None, D_MODEL, tn), lambda l, j: (l, 0, j)),
        ],
        out_specs=pl.BlockSpec((None, rows, tn), lambda l, j: (l, 0, j)),
        out_shape=jax.ShapeDtypeStruct((DEPTH, rows, width), BF16),
        scratch_shapes=[pltpu.VMEM((rows, D_MODEL), BF16)],
        compiler_params=_params(("arbitrary", "arbitrary")),
        name="mem_kv",
    )(mem2, mem_norm.reshape(DEPTH, 1, D_MODEL), w_mem_kv)


def _mix_proj_body(h_hbm, nw_ref, w_ref, qkv_ref, s_ref, mq_ref, xn_ref, hbuf, sem, *, n_qkv, n_ssm):
    i, j = pl.program_id(0), pl.program_id(1)
    tm = hbuf.shape[0]

    def fetch(tile):
        return pltpu.make_async_copy(h_hbm.at[pl.ds(tile * tm, tm), :], hbuf, sem.at[0])

    @pl.when(j == 0)
    def _():
        @pl.when(i == 0)
        def _():
            fetch(0).start()

        fetch(i).wait()
        xn_ref[...] = _rms_normalize(hbuf[...], nw_ref[...]).astype(BF16)

        @pl.when(i + 1 < pl.num_programs(0))
        def _():
            fetch(i + 1).start()

    def project(out_ref):
        out_ref[...] = jnp.dot(xn_ref[...], w_ref[...].astype(BF16), preferred_element_type=F32).astype(out_ref.dtype)

    @pl.when(j < n_qkv)
    def _():
        project(qkv_ref)

    @pl.when((j >= n_qkv) & (j < n_qkv + n_ssm))
    def _():
        project(s_ref)

    @pl.when(j >= n_qkv + n_ssm)
    def _():
        project(mq_ref)


def _mix_proj(h, norm_w, w_in, layer, *, tm=2048, tn=512):
    n = h.shape[0]
    n_qkv, n_ssm, n_mq = QKV_WIDTH // tn, SSM_WIDTH // tn, MEM_WIDTH // tn
    return pl.pallas_call(
        functools.partial(_mix_proj_body, n_qkv=n_qkv, n_ssm=n_ssm),
        grid=(n // tm, n_qkv + n_ssm + n_mq),
        in_specs=[
            pl.BlockSpec(memory_space=pl.ANY),
            pl.BlockSpec((1, D_MODEL), lambda i, j: (0, 0)),
            pl.BlockSpec((None, D_MODEL, tn), lambda i, j: (layer, 0, j)),
        ],
        out_specs=[
            pl.BlockSpec((tm, tn), lambda i, j: (i, jnp.minimum(j, n_qkv - 1))),
            pl.BlockSpec((tm, tn), lambda i, j: (i, jnp.clip(j - n_qkv, 0, n_ssm - 1))),
            pl.BlockSpec((tm, tn), lambda i, j: (i, jnp.clip(j - n_qkv - n_ssm, 0, n_mq - 1))),
            pl.BlockSpec((tm, D_MODEL), lambda i, j: (i, 0), pipeline_mode=pl.Buffered(1)),
        ],
        out_shape=[
            jax.ShapeDtypeStruct((n, QKV_WIDTH), BF16),
            jax.ShapeDtypeStruct((n, SSM_WIDTH), F32),
            jax.ShapeDtypeStruct((n, MEM_WIDTH), BF16),
            jax.ShapeDtypeStruct((n, D_MODEL), BF16),
        ],
        scratch_shapes=[pltpu.VMEM((tm, D_MODEL), F32), pltpu.SemaphoreType.DMA((1,))],
        compiler_params=_params(("arbitrary", "arbitrary")),
        name="mix_proj",
    )(h, norm_w, w_in)


SWA_STEP_BLOCKS = 4


def _swa_body(sinks_ref, q_ref, kvc_ref, kvp_ref, o_ref, s_ref, p_ref, den_ref):
    step = pl.program_id(1)
    for t in range(SWA_STEP_BLOCKS):
        rows = slice(t * WINDOW, (t + 1) * WINDOW)
        if t == 0:
            kvp, first_key = kvp_ref[...], jnp.where(step > 0, 0, WINDOW)
        else:
            kvp, first_key = kvc_ref[(t - 1) * WINDOW:t * WINDOW, :], 0
        _swa_block(sinks_ref, q_ref[rows, :], kvc_ref[rows, :], kvp, first_key, o_ref, rows, s_ref, p_ref, den_ref)


def _swa_block(sinks_ref, q, kvc, kvp, first_key, o_ref, rows, s_ref, p_ref, den_ref):
    qi = lax.broadcasted_iota(jnp.int32, (WINDOW, 2 * WINDOW), 0)
    kj = lax.broadcasted_iota(jnp.int32, (WINDOW, 2 * WINDOW), 1)
    valid = (kj > qi) & (kj <= qi + WINDOW) & (kj >= first_key)
    scale = HEAD_DIM ** -0.5
    vals = []
    for g in range(N_KV_HEADS):
        ks = slice(g * HEAD_DIM, (g + 1) * HEAD_DIM)
        vs = slice(KV_WIDTH + g * HEAD_DIM, KV_WIDTH + (g + 1) * HEAD_DIM)
        k = jnp.concatenate([kvp[:, ks], kvc[:, ks]], axis=0)
        vals.append(jnp.concatenate([kvp[:, vs], kvc[:, vs]], axis=0))
        for r in range(GQA_REP):
            h = g * GQA_REP + r
            qh = q[:, h * HEAD_DIM:(h + 1) * HEAD_DIM]
            s_ref[h] = lax.dot_general(qh, k, (((1,), (1,)), ((), ())), preferred_element_type=F32)
    for h in range(N_Q_HEADS):
        s = jnp.where(valid, s_ref[h] * scale, NEG_INF)
        sink = sinks_ref[h]
        m = jnp.maximum(jnp.max(s, axis=-1, keepdims=True), sink)
        den_ref[h] = jnp.broadcast_to(jnp.exp(sink - m), (WINDOW, LANES))
        p_ref[h] = jnp.exp(s - m).astype(BF16)
    ones = jnp.ones((2 * WINDOW, LANES), BF16)
    for h in range(N_Q_HEADS):
        p = p_ref[h]
        den = jnp.dot(p, ones, preferred_element_type=F32) + den_ref[h]
        o = jnp.dot(p, vals[h // GQA_REP], preferred_element_type=F32) / den[:, :HEAD_DIM]
        o_ref[rows, h * HEAD_DIM:(h + 1) * HEAD_DIM] = o.astype(BF16)


def _swa(qkv, sinks, batch, seq):
    nb = seq // WINDOW
    assert seq % (SWA_STEP_BLOCKS * WINDOW) == 0, seq
    ns = nb // SWA_STEP_BLOCKS
    span = SWA_STEP_BLOCKS * WINDOW
    kv_col = Q_WIDTH // (2 * KV_WIDTH)
    return pl.pallas_call(
        _swa_body,
        grid=(batch, ns),
        in_specs=[
            pl.BlockSpec(memory_space=pltpu.SMEM),
            pl.BlockSpec((span, Q_WIDTH), lambda b, n: (b * ns + n, 0)),
            pl.BlockSpec((span, 2 * KV_WIDTH), lambda b, n: (b * ns + n, kv_col)),
            pl.BlockSpec((WINDOW, 2 * KV_WIDTH),
                         lambda b, n: (b * nb + jnp.maximum(n * SWA_STEP_BLOCKS - 1, 0), kv_col)),
        ],
        out_specs=pl.BlockSpec((span, Q_WIDTH), lambda b, n: (b * ns + n, 0)),
        out_shape=jax.ShapeDtypeStruct((batch * seq, Q_WIDTH), BF16),
        scratch_shapes=[pltpu.VMEM((N_Q_HEADS, WINDOW, 2 * WINDOW), F32),
                        pltpu.VMEM((N_Q_HEADS, WINDOW, 2 * WINDOW), BF16),
                        pltpu.VMEM((N_Q_HEADS, WINDOW, LANES), F32)],
        compiler_params=_params(("parallel", "arbitrary")),
        name="swa",
    )(sinks, qkv, qkv, qkv)


def _mem_attn_body(q_ref, kv_ref, o_ref):
    scale = MEM_HEAD_DIM ** -0.5
    for h in range(MEM_HEADS):
        cs = slice(h * MEM_HEAD_DIM, (h + 1) * MEM_HEAD_DIM)
        vs = slice(MEM_WIDTH + h * MEM_HEAD_DIM, MEM_WIDTH + (h + 1) * MEM_HEAD_DIM)
        s = lax.dot_general(q_ref[:, cs], kv_ref[:, cs], (((1,), (1,)), ((), ())),
                            preferred_element_type=F32) * scale
        m = jnp.max(s, axis=-1, keepdims=True)
        p = jnp.exp(s - m)
        denom = jnp.sum(p, axis=-1, keepdims=True)
        o = jnp.dot(p.astype(BF16), kv_ref[:, vs], preferred_element_type=F32) / denom
        o_ref[:, cs] = o.astype(BF16)


def _mem_attn(mq, mem_kv, layer, batch, seq, *, tq=512):
    nq = seq // tq
    return pl.pallas_call(
        _mem_attn_body,
        grid=(batch, nq),
        in_specs=[
            pl.BlockSpec((tq, MEM_WIDTH), lambda b, i: (b * nq + i, 0)),
            pl.BlockSpec((None, N_MEM, 2 * MEM_WIDTH), lambda b, i: (layer, b, 0)),
        ],
        out_specs=pl.BlockSpec((tq, MEM_WIDTH), lambda b, i: (b * nq + i, 0)),
        out_shape=jax.ShapeDtypeStruct((batch * seq, MEM_WIDTH), BF16),
        compiler_params=_params(("parallel", "arbitrary")),
        name="mem_attn",
    )(mq, mem_kv)


def _ssm_operators(lam_re, lam_im, log_dt, b_re, b_im, c_re, c_im, d_skip):
    hp = lax.Precision.HIGH
    t_len, g_n, p_n, ch = SSM_CHUNK, SSM_GROUPS, SSM_STATE, SSM_GROUP
    lr = jnp.minimum(lam_re, -1e-4)
    li = lam_im
    dt = jnp.exp(log_dt)[:, None]
    mag = jnp.exp(lr * dt)
    ar = mag * jnp.cos(li * dt)
    ai = mag * jnp.sin(li * dt)
    nr, ni = ar - 1.0, ai
    den = lr * lr + li * li
    kr = (nr * lr + ni * li) / den
    ki = (ni * lr - nr * li) / den
    bbr = kr[..., None] * b_re - ki[..., None] * b_im
    bbi = kr[..., None] * b_im + ki[..., None] * b_re
    steps = jnp.arange(t_len + 1, dtype=F32)[None, :, None]
    pmag = jnp.exp(steps * (lr * dt)[:, None, :])
    ang = steps * (li * dt)[:, None, :]
    pr = pmag * jnp.cos(ang)
    pi = pmag * jnp.sin(ang)
    wr = pr[:, :t_len, :, None] * bbr[:, None] - pi[:, :t_len, :, None] * bbi[:, None]
    wi = pr[:, :t_len, :, None] * bbi[:, None] + pi[:, :t_len, :, None] * bbr[:, None]
    lagk = (jnp.einsum('gcp,gkpd->gkcd', c_re, wr, precision=hp)
            - jnp.einsum('gcp,gkpd->gkcd', c_im, wi, precision=hp))
    nt, gt = SSM_LANE_TILES, SSM_TILE_GROUPS
    lag_c = lagk.reshape(nt, gt, t_len, ch, ch).transpose(0, 2, 1, 4, 3).reshape(nt, t_len * LANES, ch)

    def pair(x, y):
        return jnp.concatenate([x, y], axis=-1)

    def per_step(w):
        return w.reshape(nt, gt, t_len, 2 * p_n).transpose(0, 2, 1, 3)[:, :, :, None, :]

    def per_channel(w):
        return w.reshape(nt, gt, ch, 2 * p_n)[:, None]

    back = jnp.arange(t_len - 1, -1, -1, dtype=F32)[None, :, None]
    bmag = jnp.exp(back * (lr * dt)[:, None, :])
    bang = back * (li * dt)[:, None, :]
    qr, qi = bmag * jnp.cos(bang), bmag * jnp.sin(bang)
    bbr_t, bbi_t = bbr.transpose(0, 2, 1), bbi.transpose(0, 2, 1)
    inp = (per_step(pair(qr, qr)) * per_channel(pair(bbr_t, bbi_t))
           + per_step(pair(-qi, qi)) * per_channel(pair(bbi_t, bbr_t))
           ).reshape(nt, t_len * LANES, 2 * p_n)
    pr1, pi1 = pr[:, 1:], pi[:, 1:]
    outp = (per_step(pair(pr1, pi1)) * per_channel(pair(c_re, -c_re))
            - per_step(pair(pi1, pr1)) * per_channel(pair(c_im, c_im))
            ).reshape(nt, t_len * LANES, 2 * p_n)
    a1 = jnp.concatenate([pr[:, t_len], pr[:, t_len]], axis=-1).reshape(nt, gt, 2 * p_n)
    a2 = jnp.concatenate([-pi[:, t_len], pi[:, t_len]], axis=-1).reshape(nt, gt, 2 * p_n)
    return lag_c, inp, outp, a1, a2, d_skip.reshape(1, SSM_WIDTH)


def _ssm_body(s_ref, lag_ref, inp_ref, outp_ref, a1_ref, a2_ref, d_ref, y_ref,
              u_ref, panel_ref, inpx_ref, outpx_ref, z_ref, zs_ref, sp_ref, *, batch):
    t_len, gt = SSM_CHUNK, SSM_TILE_GROUPS
    n_chunks = u_ref.shape[0]
    per_seq = n_chunks // batch
    flat = t_len * LANES

    for t in range(t_len):
        u_ref[:, t * LANES:(t + 1) * LANES] = s_ref[pl.ds(t, n_chunks, stride=t_len), :].astype(BF16)

    row_group = (lax.broadcasted_iota(jnp.int32, (flat, 1), 0) >> 4) & (gt - 1)
    col_group = lax.broadcasted_iota(jnp.int32, (1, LANES), 1) >> 4
    spread = (lax.broadcasted_iota(jnp.int32, (SSM_GROUP, LANES), 1) & (SSM_GROUP - 1)
              == lax.broadcasted_iota(jnp.int32, (SSM_GROUP, LANES), 0)).astype(BF16)
    lag = jnp.dot(lag_ref[...].astype(BF16), spread, preferred_element_type=F32)
    lag = jnp.where(row_group == col_group, lag, 0.0).astype(BF16)

    for r in range(t_len):
        k_left, k_right = t_len - 2 - r, t_len - 1 - r
        left = lag[k_left * LANES:(k_left + 1) * LANES] if k_left >= 0 else jnp.zeros((LANES, LANES), BF16)
        panel_ref[r * LANES:(r + 1) * LANES, :LANES] = left
        panel_ref[r * LANES:(r + 1) * LANES, LANES:] = lag[k_right * LANES:(k_right + 1) * LANES]

    inp = inp_ref[...]
    outp = outp_ref[...]
    for g in range(gt):
        inpx_ref[:, g * LANES:(g + 1) * LANES] = jnp.where(row_group == g, inp, 0.0).astype(BF16)
        outpx_ref[:, g * LANES:(g + 1) * LANES] = jnp.where(row_group == g, outp, 0.0).astype(BF16)

    z = jnp.dot(u_ref[...], inpx_ref[...], preferred_element_type=F32)
    for g in range(gt):
        zg = z[:, g * LANES:(g + 1) * LANES]
        z_ref[pl.ds(g, n_chunks, stride=gt), :] = zg
        zs_ref[pl.ds(g, n_chunks, stride=gt), :] = pltpu.roll(zg, SSM_STATE, axis=1)

    a1 = a1_ref[...]
    a2 = a2_ref[...]

    def step(c, carry):
        new = []
        for b in range(batch):
            v0, v1 = carry[b]
            row = pl.multiple_of((b * per_seq + c) * gt, gt)
            sp_ref[pl.ds(row, gt), :] = v0
            z0 = z_ref[pl.ds(row, gt), :]
            z1 = zs_ref[pl.ds(row, gt), :]
            new.append((a1 * v0 + a2 * v1 + z0, a1 * v1 - a2 * v0 + z1))
        return tuple(new)

    zero = jnp.zeros((gt, LANES), F32)
    lax.fori_loop(0, per_seq, step, tuple((zero, zero) for _ in range(batch)), unroll=4)

    sp = jnp.concatenate([sp_ref[pl.ds(g, n_chunks, stride=gt), :] for g in range(gt)], axis=1).astype(BF16)
    d2 = jnp.concatenate([d_ref[...], d_ref[...]], axis=1)
    for q in range(t_len // 2):
        cols = slice(2 * q * LANES, (2 * q + 2) * LANES)
        k_len = (2 * q + 2) * LANES
        y = jnp.dot(u_ref[:, :k_len], panel_ref[flat - k_len:, :], preferred_element_type=F32)
        y = y + lax.dot_general(sp, outpx_ref[cols, :], (((1,), (1,)), ((), ())), preferred_element_type=F32)
        y = y + d2 * u_ref[:, cols].astype(F32)
        y_ref[pl.ds(2 * q, n_chunks, stride=t_len), :] = y[:, :LANES]
        y_ref[pl.ds(2 * q + 1, n_chunks, stride=t_len), :] = y[:, LANES:]


def _ssm(s_in, ops, layer, batch):
    lag_c, inp, outp, a1, a2, d_row = ops
    n = s_in.shape[0]
    n_chunks = n // SSM_CHUNK
    flat = SSM_CHUNK * LANES
    return pl.pallas_call(
        functools.partial(_ssm_body, batch=batch),
        grid=(SSM_LANE_TILES,),
        in_specs=[
            pl.BlockSpec((n, LANES), lambda j: (0, j)),
            pl.BlockSpec((None, None, flat, SSM_GROUP), lambda j: (layer, j, 0, 0)),
            pl.BlockSpec((None, None, flat, 2 * SSM_STATE), lambda j: (layer, j, 0, 0)),
            pl.BlockSpec((None, None, flat, 2 * SSM_STATE), lambda j: (layer, j, 0, 0)),
            pl.BlockSpec((None, None, SSM_TILE_GROUPS, 2 * SSM_STATE), lambda j: (layer, j, 0, 0)),
            pl.BlockSpec((None, None, SSM_TILE_GROUPS, 2 * SSM_STATE), lambda j: (layer, j, 0, 0)),
            pl.BlockSpec((None, 1, LANES), lambda j: (layer, 0, j)),
        ],
        out_specs=pl.BlockSpec((n, LANES), lambda j: (0, j)),
        out_shape=jax.ShapeDtypeStruct((n, SSM_WIDTH), F32),
        scratch_shapes=[
            pltpu.VMEM((n_chunks, flat), BF16),
            pltpu.VMEM((flat, 2 * LANES), BF16),
            pltpu.VMEM((flat, SSM_TILE_GROUPS * 2 * SSM_STATE), BF16),
            pltpu.VMEM((flat, SSM_TILE_GROUPS * 2 * SSM_STATE), BF16),
            pltpu.VMEM((n_chunks * SSM_TILE_GROUPS, 2 * SSM_STATE), F32),
            pltpu.VMEM((n_chunks * SSM_TILE_GROUPS, 2 * SSM_STATE), F32),
            pltpu.VMEM((n_chunks * SSM_TILE_GROUPS, 2 * SSM_STATE), F32),
        ],
        compiler_params=_params(("parallel",)),
        name="ssm",
    )(s_in, lag_c, inp, outp, a1, a2, d_row)


MERGE_ROW_CHUNK = 256


def _gate_merge_body(xn_ref, swa_ref, ssm_ref, mem_ref, wg0_ref, wg1_ref, wg2_ref, wswa_ref,
                     wga_ref, wgb_ref, wmem_ref, o_ref):
    tn = o_ref.shape[1]
    half_k = D_MODEL // 2
    wg01 = jnp.concatenate([wg0_ref[...].astype(BF16), wg1_ref[...].astype(BF16)], axis=1)
    wg2 = wg2_ref[...].astype(BF16)
    wglu = jnp.concatenate([wga_ref[...].astype(BF16), wgb_ref[...].astype(BF16)], axis=1)
    wswa = wswa_ref[...].astype(BF16)
    wmem = wmem_ref[...].astype(BF16)
    for r in range(o_ref.shape[0] // MERGE_ROW_CHUNK):
        rows = pl.ds(r * MERGE_ROW_CHUNK, MERGE_ROW_CHUNK)
        xn = xn_ref[rows, :]
        ys = ssm_ref[rows, :].astype(BF16)
        y_swa = jnp.dot(swa_ref[rows, :], wswa, preferred_element_type=F32)
        y_mem = jnp.dot(mem_ref[rows, :], wmem, preferred_element_type=F32)
        glu = jnp.dot(ys, wglu, preferred_element_type=F32)
        y_ssm = glu[:, :tn] * _sigmoid(glu[:, tn:])
        g01 = jnp.dot(xn, wg01, preferred_element_type=F32)
        g2 = (jnp.dot(xn[:, :half_k], wg2[:half_k], preferred_element_type=F32)
              + jnp.dot(xn[:, half_k:], wg2[half_k:], preferred_element_type=F32))
        merged = _sigmoid(g01[:, :tn]) * y_swa + _sigmoid(g01[:, tn:]) * y_ssm + _sigmoid(g2) * y_mem
        o_ref[rows, :] = merged.astype(BF16)


def _gate_merge(xn, o_swa, y_s, o_mem, w_in, layer, w_swa_up, w_ssm_glu, w_mem_up, *, tm=1024, tn=256):
    n = xn.shape[0]
    nj = D_MODEL // tn
    g0 = GATE_OFFSET // tn
    row = lambda i, j: (i, 0)
    return pl.pallas_call(
        _gate_merge_body,
        grid=(n // tm, nj),
        in_specs=[
            pl.BlockSpec((tm, D_MODEL), row),
            pl.BlockSpec((tm, Q_WIDTH), row),
            pl.BlockSpec((tm, SSM_WIDTH), row),
            pl.BlockSpec((tm, MEM_WIDTH), row),
            pl.BlockSpec((None, D_MODEL, tn), lambda i, j: (layer, 0, g0 + j)),
            pl.BlockSpec((None, D_MODEL, tn), lambda i, j: (layer, 0, g0 + j + nj)),
            pl.BlockSpec((None, D_MODEL, tn), lambda i, j: (layer, 0, g0 + j + 2 * nj)),
            pl.BlockSpec((None, Q_WIDTH, tn), lambda i, j: (layer, 0, j)),
            pl.BlockSpec((None, SSM_WIDTH, tn), lambda i, j: (layer, 0, j)),
            pl.BlockSpec((None, SSM_WIDTH, tn), lambda i, j: (layer, 0, j + nj)),
            pl.BlockSpec((None, MEM_WIDTH, tn), lambda i, j: (layer, 0, j)),
        ],
        out_specs=pl.BlockSpec((tm, tn), lambda i, j: (i, j)),
        out_shape=jax.ShapeDtypeStruct((n, D_MODEL), BF16),
        compiler_params=_params(("parallel", "arbitrary")),
        name="gate_merge",
    )(xn, o_swa, y_s, o_mem, w_in, w_in, w_in, w_swa_up, w_ssm_glu, w_ssm_glu, w_mem_up)


def _out_proj_body(h_ref, m_ref, w_ref, o_ref, wb_ref):
    @pl.when(pl.program_id(0) == 0)
    def _():
        wb_ref[...] = w_ref[...].astype(BF16)

    half = D_MODEL // 2
    for c in range(2):
        cols = pl.ds(c * half, half)
        o_ref[:, cols] = h_ref[:, cols] + jnp.dot(m_ref[...], wb_ref[:, cols], preferred_element_type=F32)


def _out_proj(h, merged, w_out, layer, *, tm=512):
    n = h.shape[0]
    return pl.pallas_call(
        _out_proj_body,
        grid=(n // tm,),
        in_specs=[
            pl.BlockSpec((tm, D_MODEL), lambda i: (i, 0)),
            pl.BlockSpec((tm, D_MODEL), lambda i: (i, 0)),
            pl.BlockSpec((None, D_MODEL, D_MODEL), lambda i: (layer, 0, 0), pipeline_mode=pl.Buffered(1)),
        ],
        out_specs=pl.BlockSpec((tm, D_MODEL), lambda i: (i, 0)),
        out_shape=jax.ShapeDtypeStruct((n, D_MODEL), F32),
        scratch_shapes=[pltpu.VMEM((D_MODEL, D_MODEL), BF16)],
        compiler_params=_params(("arbitrary",)),
        name="out_proj",
    )(h, merged, w_out)


def kernel(x, mem, ffn1_norm, ffn1_w_in, ffn1_w_out, mix_norm, mem_norm, w_in, sinks, w_mem_kv, lam_re, lam_im, log_dt, b_re, b_im, c_re, c_im, d_skip, w_ssm_glu, w_swa_up, w_mem_up, w_out, ffn2_norm, ffn2_w_in, ffn2_w_out, final_norm):
    batch, seq = x.shape[0], x.shape[1]
    n = batch * seq
    h = x.reshape(n, D_MODEL)
    mem2 = mem.reshape(batch * N_MEM, D_MODEL)
    final_w = final_norm.reshape(1, D_MODEL)
    ssm_ops = jax.vmap(_ssm_operators)(lam_re, lam_im, log_dt, b_re, b_im, c_re, c_im, d_skip)
    mem_kv = _mem_kv(mem2, mem_norm, w_mem_kv)
    for l in range(DEPTH):
        mix_w = mix_norm[l].reshape(1, D_MODEL)

        h = _ffn(h, ffn1_norm[l].reshape(1, D_MODEL), ffn1_w_in, ffn1_w_out, final_w, l, apply_final_norm=False)
        qkv, s_in, mq, xn = _mix_proj(h, mix_w, w_in, l)
        o_swa = _swa(qkv, sinks[l], batch, seq)
        o_mem = _mem_attn(mq, mem_kv, l, batch, seq)
        y_s = _ssm(s_in, ssm_ops, l, batch)
        merged = _gate_merge(xn, o_swa, y_s, o_mem, w_in, l, w_swa_up, w_ssm_glu, w_mem_up)
        h = _out_proj(h, merged, w_out, l)
        h = _ffn(h, ffn2_norm[l].reshape(1, D_MODEL), ffn2_w_in, ffn2_w_out, final_w, l,
                 apply_final_norm=(l == DEPTH - 1))
    return h.reshape(batch, seq, D_MODEL)
```

```python
import functools

import jax
import jax.numpy as jnp
from jax import lax
from jax.experimental import pallas as pl
from jax.experimental.pallas import tpu as pltpu

D_MODEL = 2048
DEPTH = 4
N_MEM = 256
D_FF = 5632
RMS_EPS = 1e-5

WINDOW = 128
HEAD_DIM = 64
N_Q_HEADS = 16
N_KV_HEADS = 4
GQA_REP = N_Q_HEADS // N_KV_HEADS
Q_WIDTH = N_Q_HEADS * HEAD_DIM
KV_WIDTH = N_KV_HEADS * HEAD_DIM

SSM_WIDTH = 1024
SSM_GROUP = 16
SSM_GROUPS = SSM_WIDTH // SSM_GROUP
SSM_STATE = 64
SSM_CHUNK = 16
LANES = 128
SSM_LANE_TILES = SSM_WIDTH // LANES
SSM_TILE_GROUPS = LANES // SSM_GROUP

MEM_HEADS = 4
MEM_HEAD_DIM = 256
MEM_WIDTH = MEM_HEADS * MEM_HEAD_DIM

NEG_INF = -1e30

QKV_WIDTH = Q_WIDTH + 2 * KV_WIDTH
SSM_OFFSET = QKV_WIDTH
MEMQ_OFFSET = SSM_OFFSET + SSM_WIDTH
GATE_OFFSET = MEMQ_OFFSET + MEM_WIDTH

VMEM_LIMIT_BYTES = 56 * 1024 * 1024

BF16 = jnp.bfloat16
F32 = jnp.float32


def _params(semantics):
    return pltpu.CompilerParams(dimension_semantics=semantics, vmem_limit_bytes=VMEM_LIMIT_BYTES)


def _rms_normalize(x, w):
    ms = jnp.mean(x * x, axis=-1, keepdims=True)
    return (x * lax.rsqrt(ms + RMS_EPS)) * w


def _sigmoid(x):
    return 0.5 * jnp.tanh(0.5 * x) + 0.5


FFN_ROW_CHUNK = 1024


class _RowTileIO:
    def __init__(self, h_hbm, out_hbm, hbuf, acc_ref, sems):
        self.h_hbm, self.out_hbm, self.hbuf, self.acc_ref, self.sems = h_hbm, out_hbm, hbuf, acc_ref, sems
        self.tm = acc_ref.shape[0]

    def _fetch(self, tile):
        return pltpu.make_async_copy(self.h_hbm.at[pl.ds(tile * self.tm, self.tm), :], self.hbuf, self.sems.at[0])

    def _write_back(self, tile):
        return pltpu.make_async_copy(self.acc_ref, self.out_hbm.at[pl.ds(tile * self.tm, self.tm), :],
                                     self.sems.at[1])

    def begin(self, xn_ref, nw_ref):
        i = pl.program_id(0)

        @pl.when(i == 0)
        def _():
            self._fetch(0).start()

        self._fetch(i).wait()
        xn_ref[...] = _rms_normalize(self.hbuf[...], nw_ref[...]).astype(BF16)

        @pl.when(i > 0)
        def _():
            self._write_back(i - 1).wait()

        self.acc_ref[...] = self.hbuf[...]

        @pl.when(i + 1 < pl.num_programs(0))
        def _():
            self._fetch(i + 1).start()

    def end(self):
        i = pl.program_id(0)
        self._write_back(i).start()

        @pl.when(i == pl.num_programs(0) - 1)
        def _():
            self._write_back(i).wait()


def _ffn_body(h_hbm, nw_ref, wg_ref, wu_ref, wo_ref, fw_ref, out_hbm, xn_ref, hbuf, acc_ref, sems, *,
              apply_final_norm):
    j = pl.program_id(1)
    io = _RowTileIO(h_hbm, out_hbm, hbuf, acc_ref, sems)

    @pl.when(j == 0)
    def _():
        io.begin(xn_ref, nw_ref)

    wg = wg_ref[...].astype(BF16)
    wu = wu_ref[...].astype(BF16)
    wo = wo_ref[...].astype(BF16)
    for r in range(acc_ref.shape[0] // FFN_ROW_CHUNK):
        rows = pl.ds(r * FFN_ROW_CHUNK, FFN_ROW_CHUNK)
        xn = xn_ref[rows, :]
        g = jnp.dot(xn, wg, preferred_element_type=F32)
        u = jnp.dot(xn, wu, preferred_element_type=F32)
        a = ((0.5 * g) * _sigmoid(g)) * u
        acc_ref[rows, :] += jnp.dot(a.astype(BF16), wo, preferred_element_type=F32)

    @pl.when(j == pl.num_programs(1) - 1)
    def _():
        if apply_final_norm:
            acc_ref[...] = _rms_normalize(acc_ref[...], fw_ref[...])
        io.end()


def _ffn(h, norm_w, w_in, w_out, final_w, layer, *, apply_final_norm, tm=1024, tf=512):
    n = h.shape[0]
    nf = D_FF // tf
    return pl.pallas_call(
        functools.partial(_ffn_body, apply_final_norm=apply_final_norm),
        grid=(n // tm, nf),
        in_specs=[
            pl.BlockSpec(memory_space=pl.ANY),
            pl.BlockSpec((1, D_MODEL), lambda i, j: (0, 0)),
            pl.BlockSpec((None, D_MODEL, tf), lambda i, j: (layer, 0, j)),
            pl.BlockSpec((None, D_MODEL, tf), lambda i, j: (layer, 0, j + nf)),
            pl.BlockSpec((None, tf, D_MODEL), lambda i, j: (layer, j, 0)),
            pl.BlockSpec((1, D_MODEL), lambda i, j: (0, 0)),
        ],
        out_specs=pl.BlockSpec(memory_space=pl.ANY),
        out_shape=jax.ShapeDtypeStruct((n, D_MODEL), F32),
        scratch_shapes=[pltpu.VMEM((tm, D_MODEL), BF16), pltpu.VMEM((tm, D_MODEL), F32),
                        pltpu.VMEM((tm, D_MODEL), F32), pltpu.SemaphoreType.DMA((2,))],
        compiler_params=_params(("arbitrary", "arbitrary")),
        name="ffn",
    )(h, norm_w, w_in, w_in, w_out, final_w)


def _mem_kv_body(m_ref, nw_ref, w_ref, o_ref, xn_ref):
    @pl.when(pl.program_id(1) == 0)
    def _():
        xn_ref[...] = _rms_normalize(m_ref[...], nw_ref[...]).astype(BF16)

    o_ref[...] = jnp.dot(xn_ref[...], w_ref[...].astype(BF16), preferred_element_type=F32).astype(BF16)


def _mem_kv(mem2, mem_norm, w_mem_kv, *, tn=512):
    rows = mem2.shape[0]
    width = 2 * MEM_WIDTH
    return pl.pallas_call(
        _mem_kv_body,
        grid=(DEPTH, width // tn),
        in_specs=[
            pl.BlockSpec((rows, D_MODEL), lambda l, j: (0, 0)),
            pl.BlockSpec((None, 1, D_MODEL), lambda l, j: (l, 0, 0)),
            pl.BlockSpec((None, D_MODEL, tn), lambda l, j: (l, 0, j)),
        ],
        out_specs=pl.BlockSpec((None, rows, tn), lambda l, j: (l, 0, j)),
        out_shape=jax.ShapeDtypeStruct((DEPTH, rows, width), BF16),
        scratch_shapes=[pltpu.VMEM((rows, D_MODEL), BF16)],
        compiler_params=_params(("arbitrary", "arbitrary")),
        name="mem_kv",
    )(mem2, mem_norm.reshape(DEPTH, 1, D_MODEL), w_mem_kv)


def _mix_proj_body(h_hbm, nw_ref, w_ref, qkv_ref, s_ref, mq_ref, xn_ref, hbuf, sem, *, n_qkv, n_ssm):
    i, j = pl.program_id(0), pl.program_id(1)
    tm = hbuf.shape[0]

    def fetch(tile):
        return pltpu.make_async_copy(h_hbm.at[pl.ds(tile * tm, tm), :], hbuf, sem.at[0])

    @pl.when(j == 0)
    def _():
        @pl.when(i == 0)
        def _():
            fetch(0).start()

        fetch(i).wait()
        xn_ref[...] = _rms_normalize(hbuf[...], nw_ref[...]).astype(BF16)

        @pl.when(i + 1 < pl.num_programs(0))
        def _():
            fetch(i + 1).start()

    def project(out_ref):
        out_ref[...] = jnp.dot(xn_ref[...], w_ref[...].astype(BF16), preferred_element_type=F32).astype(out_ref.dtype)

    @pl.when(j < n_qkv)
    def _():
        project(qkv_ref)

    @pl.when((j >= n_qkv) & (j < n_qkv + n_ssm))
    def _():
        project(s_ref)

    @pl.when(j >= n_qkv + n_ssm)
    def _():
        project(mq_ref)


def _mix_proj(h, norm_w, w_in, layer, *, tm=2048, tn=512):
    n = h.shape[0]
    n_qkv, n_ssm, n_mq = QKV_WIDTH // tn, SSM_WIDTH // tn, MEM_WIDTH // tn
    return pl.pallas_call(
        functools.partial(_mix_proj_body, n_qkv=n_qkv, n_ssm=n_ssm),
        grid=(n // tm, n_qkv + n_ssm + n_mq),
        in_specs=[
            pl.BlockSpec(memory_space=pl.ANY),
            pl.BlockSpec((1, D_MODEL), lambda i, j: (0, 0)),
            pl.BlockSpec((None, D_MODEL, tn), lambda i, j: (layer, 0, j)),
        ],
        out_specs=[
            pl.BlockSpec((tm, tn), lambda i, j: (i, jnp.minimum(j, n_qkv - 1))),
            pl.BlockSpec((tm, tn), lambda i, j: (i, jnp.clip(j - n_qkv, 0, n_ssm - 1))),
            pl.BlockSpec((tm, tn), lambda i, j: (i, jnp.clip(j - n_qkv - n_ssm, 0, n_mq - 1))),
            pl.BlockSpec((tm, D_MODEL), lambda i, j: (i, 0), pipeline_mode=pl.Buffered(1)),
        ],
        out_shape=[
            jax.ShapeDtypeStruct((n, QKV_WIDTH), BF16),
            jax.ShapeDtypeStruct((n, SSM_WIDTH), F32),
            jax.ShapeDtypeStruct((n, MEM_WIDTH), BF16),
            jax.ShapeDtypeStruct((n, D_MODEL), BF16),
        ],
        scratch_shapes=[pltpu.VMEM((tm, D_MODEL), F32), pltpu.SemaphoreType.DMA((1,))],
        compiler_params=_params(("arbitrary", "arbitrary")),
        name="mix_proj",
    )(h, norm_w, w_in)


SWA_STEP_BLOCKS = 4


def _swa_body(sinks_ref, q_ref, kvc_ref, kvp_ref, o_ref, s_ref, p_ref, den_ref):
    step = pl.program_id(1)
    for t in range(SWA_STEP_BLOCKS):
        rows = slice(t * WINDOW, (t + 1) * WINDOW)
        if t == 0:
            kvp, first_key = kvp_ref[...], jnp.where(step > 0, 0, WINDOW)
        else:
            kvp, first_key = kvc_ref[(t - 1) * WINDOW:t * WINDOW, :], 0
        _swa_block(sinks_ref, q_ref[rows, :], kvc_ref[rows, :], kvp, first_key, o_ref, rows, s_ref, p_ref, den_ref)


def _swa_block(sinks_ref, q, kvc, kvp, first_key, o_ref, rows, s_ref, p_ref, den_ref):
    qi = lax.broadcasted_iota(jnp.int32, (WINDOW, 2 * WINDOW), 0)
    kj = lax.broadcasted_iota(jnp.int32, (WINDOW, 2 * WINDOW), 1)
    valid = (kj > qi) & (kj <= qi + WINDOW) & (kj >= first_key)
    scale = HEAD_DIM ** -0.5
    vals = []
    for g in range(N_KV_HEADS):
        ks = slice(g * HEAD_DIM, (g + 1) * HEAD_DIM)
        vs = slice(KV_WIDTH + g * HEAD_DIM, KV_WIDTH + (g + 1) * HEAD_DIM)
        k = jnp.concatenate([kvp[:, ks], kvc[:, ks]], axis=0)
        vals.append(jnp.concatenate([kvp[:, vs], kvc[:, vs]], axis=0))
        for r in range(GQA_REP):
            h = g * GQA_REP + r
            qh = q[:, h * HEAD_DIM:(h + 1) * HEAD_DIM]
            s_ref[h] = lax.dot_general(qh, k, (((1,), (1,)), ((), ())), preferred_element_type=F32)
    for h in range(N_Q_HEADS):
        s = jnp.where(valid, s_ref[h] * scale, NEG_INF)
        sink = sinks_ref[h]
        m = jnp.maximum(jnp.max(s, axis=-1, keepdims=True), sink)
        den_ref[h] = jnp.broadcast_to(jnp.exp(sink - m), (WINDOW, LANES))
        p_ref[h] = jnp.exp(s - m).astype(BF16)
    ones = jnp.ones((2 * WINDOW, LANES), BF16)
    for h in range(N_Q_HEADS):
        p = p_ref[h]
        den = jnp.dot(p, ones, preferred_element_type=F32) + den_ref[h]
        o = jnp.dot(p, vals[h // GQA_REP], preferred_element_type=F32) / den[:, :HEAD_DIM]
        o_ref[rows, h * HEAD_DIM:(h + 1) * HEAD_DIM] = o.astype(BF16)


def _swa(qkv, sinks, batch, seq):
    nb = seq // WINDOW
    assert seq % (SWA_STEP_BLOCKS * WINDOW) == 0, seq
    ns = nb // SWA_STEP_BLOCKS
    span = SWA_STEP_BLOCKS * WINDOW
    kv_col = Q_WIDTH // (2 * KV_WIDTH)
    return pl.pallas_call(
        _swa_body,
        grid=(batch, ns),
        in_specs=[
            pl.BlockSpec(memory_space=pltpu.SMEM),
            pl.BlockSpec((span, Q_WIDTH), lambda b, n: (b * ns + n, 0)),
            pl.BlockSpec((span, 2 * KV_WIDTH), lambda b, n: (b * ns + n, kv_col)),
            pl.BlockSpec((WINDOW, 2 * KV_WIDTH),
                         lambda b, n: (b * nb + jnp.maximum(n * SWA_STEP_BLOCKS - 1, 0), kv_col)),
        ],
        out_specs=pl.BlockSpec((span, Q_WIDTH), lambda b, n: (b * ns + n, 0)),
        out_shape=jax.ShapeDtypeStruct((batch * seq, Q_WIDTH), BF16),
        scratch_shapes=[pltpu.VMEM((N_Q_HEADS, WINDOW, 2 * WINDOW), F32),
                        pltpu.VMEM((N_Q_HEADS, WINDOW, 2 * WINDOW), BF16),
                        pltpu.VMEM((N_Q_HEADS, WINDOW, LANES), F32)],
        compiler_params=_params(("parallel", "arbitrary")),
        name="swa",
    )(sinks, qkv, qkv, qkv)


def _mem_attn_body(q_ref, kv_ref, o_ref):
    scale = MEM_HEAD_DIM ** -0.5
    for h in range(MEM_HEADS):
        cs = slice(h * MEM_HEAD_DIM, (h + 1) * MEM_HEAD_DIM)
        vs = slice(MEM_WIDTH + h * MEM_HEAD_DIM, MEM_WIDTH + (h + 1) * MEM_HEAD_DIM)
        s = lax.dot_general(q_ref[:, cs], kv_ref[:, cs], (((1,), (1,)), ((), ())),
                            preferred_element_type=F32) * scale
        m = jnp.max(s, axis=-1, keepdims=True)
        p = jnp.exp(s - m)
        denom = jnp.sum(p, axis=-1, keepdims=True)
        o = jnp.dot(p.astype(BF16), kv_ref[:, vs], preferred_element_type=F32) / denom
        o_ref[:, cs] = o.astype(BF16)


def _mem_attn(mq, mem_kv, layer, batch, seq, *, tq=512):
    nq = seq // tq
    return pl.pallas_call(
        _mem_attn_body,
        grid=(batch, nq),
        in_specs=[
            pl.BlockSpec((tq, MEM_WIDTH), lambda b, i: (b * nq + i, 0)),
            pl.BlockSpec((None, N_MEM, 2 * MEM_WIDTH), lambda b, i: (layer, b, 0)),
        ],
        out_specs=pl.BlockSpec((tq, MEM_WIDTH), lambda b, i: (b * nq + i, 0)),
        out_shape=jax.ShapeDtypeStruct((batch * seq, MEM_WIDTH), BF16),
        compiler_params=_params(("parallel", "arbitrary")),
        name="mem_attn",
    )(mq, mem_kv)


def _ssm_operators(lam_re, lam_im, log_dt, b_re, b_im, c_re, c_im, d_skip):
    hp = lax.Precision.HIGH
    t_len, g_n, p_n, ch = SSM_CHUNK, SSM_GROUPS, SSM_STATE, SSM_GROUP
    lr = jnp.minimum(lam_re, -1e-4)
    li = lam_im
    dt = jnp.exp(log_dt)[:, None]
    mag = jnp.exp(lr * dt)
    ar = mag * jnp.cos(li * dt)
    ai = mag * jnp.sin(li * dt)
    nr, ni = ar - 1.0, ai
    den = lr * lr + li * li
    kr = (nr * lr + ni * li) / den
    ki = (ni * lr - nr * li) / den
    bbr = kr[..., None] * b_re - ki[..., None] * b_im
    bbi = kr[..., None] * b_im + ki[..., None] * b_re
    steps = jnp.arange(t_len + 1, dtype=F32)[None, :, None]
    pmag = jnp.exp(steps * (lr * dt)[:, None, :])
    ang = steps * (li * dt)[:, None, :]
    pr = pmag * jnp.cos(ang)
    pi = pmag * jnp.sin(ang)
    wr = pr[:, :t_len, :, None] * bbr[:, None] - pi[:, :t_len, :, None] * bbi[:, None]
    wi = pr[:, :t_len, :, None] * bbi[:, None] + pi[:, :t_len, :, None] * bbr[:, None]
    lagk = (jnp.einsum('gcp,gkpd->gkcd', c_re, wr, precision=hp)
            - jnp.einsum('gcp,gkpd->gkcd', c_im, wi, precision=hp))
    nt, gt = SSM_LANE_TILES, SSM_TILE_GROUPS
    lag_c = lagk.reshape(nt, gt, t_len, ch, ch).transpose(0, 2, 1, 4, 3).reshape(nt, t_len * LANES, ch)

    def pair(x, y):
        return jnp.concatenate([x, y], axis=-1)

    def per_step(w):
        return w.reshape(nt, gt, t_len, 2 * p_n).transpose(0, 2, 1, 3)[:, :, :, None, :]

    def per_channel(w):
        return w.reshape(nt, gt, ch, 2 * p_n)[:, None]

    back = jnp.arange(t_len - 1, -1, -1, dtype=F32)[None, :, None]
    bmag = jnp.exp(back * (lr * dt)[:, None, :])
    bang = back * (li * dt)[:, None, :]
    qr, qi = bmag * jnp.cos(bang), bmag * jnp.sin(bang)
    bbr_t, bbi_t = bbr.transpose(0, 2, 1), bbi.transpose(0, 2, 1)
    inp = (per_step(pair(qr, qr)) * per_channel(pair(bbr_t, bbi_t))
           + per_step(pair(-qi, qi)) * per_channel(pair(bbi_t, bbr_t))
           ).reshape(nt, t_len * LANES, 2 * p_n)
    pr1, pi1 = pr[:, 1:], pi[:, 1:]
    outp = (per_step(pair(pr1, pi1)) * per_channel(pair(c_re, -c_re))
            - per_step(pair(pi1, pr1)) * per_channel(pair(c_im, c_im))
            ).reshape(nt, t_len * LANES, 2 * p_n)
    a1 = jnp.concatenate([pr[:, t_len], pr[:, t_len]], axis=-1).reshape(nt, gt, 2 * p_n)
    a2 = jnp.concatenate([-pi[:, t_len], pi[:, t_len]], axis=-1).reshape(nt, gt, 2 * p_n)
    return lag_c, inp, outp, a1, a2, d_skip.reshape(1, SSM_WIDTH)


def _ssm_body(s_ref, lag_ref, inp_ref, outp_ref, a1_ref, a2_ref, d_ref, y_ref,
              u_ref, panel_ref, inpx_ref, outpx_ref, z_ref, zs_ref, sp_ref, *, batch):
    t_len, gt = SSM_CHUNK, SSM_TILE_GROUPS
    n_chunks = u_ref.shape[0]
    per_seq = n_chunks // batch
    flat = t_len * LANES

    for t in range(t_len):
        u_ref[:, t * LANES:(t + 1) * LANES] = s_ref[pl.ds(t, n_chunks, stride=t_len), :].astype(BF16)

    row_group = (lax.broadcasted_iota(jnp.int32, (flat, 1), 0) >> 4) & (gt - 1)
    col_group = lax.broadcasted_iota(jnp.int32, (1, LANES), 1) >> 4
    spread = (lax.broadcasted_iota(jnp.int32, (SSM_GROUP, LANES), 1) & (SSM_GROUP - 1)
              == lax.broadcasted_iota(jnp.int32, (SSM_GROUP, LANES), 0)).astype(BF16)
    lag = jnp.dot(lag_ref[...].astype(BF16), spread, preferred_element_type=F32)
    lag = jnp.where(row_group == col_group, lag, 0.0).astype(BF16)

    for r in range(t_len):
        k_left, k_right = t_len - 2 - r, t_len - 1 - r
        left = lag[k_left * LANES:(k_left + 1) * LANES] if k_left >= 0 else jnp.zeros((LANES, LANES), BF16)
        panel_ref[r * LANES:(r + 1) * LANES, :LANES] = left
        panel_ref[r * LANES:(r + 1) * LANES, LANES:] = lag[k_right * LANES:(k_right + 1) * LANES]

    inp = inp_ref[...]
    outp = outp_ref[...]
    for g in range(gt):
        inpx_ref[:, g * LANES:(g + 1) * LANES] = jnp.where(row_group == g, inp, 0.0).astype(BF16)
        outpx_ref[:, g * LANES:(g + 1) * LANES] = jnp.where(row_group == g, outp, 0.0).astype(BF16)

    z = jnp.dot(u_ref[...], inpx_ref[...], preferred_element_type=F32)
    for g in range(gt):
        zg = z[:, g * LANES:(g + 1) * LANES]
        z_ref[pl.ds(g, n_chunks, stride=gt), :] = zg
        zs_ref[pl.ds(g, n_chunks, stride=gt), :] = pltpu.roll(zg, SSM_STATE, axis=1)

    a1 = a1_ref[...]
    a2 = a2_ref[...]

    def step(c, carry):
        new = []
        for b in range(batch):
            v0, v1 = carry[b]
            row = pl.multiple_of((b * per_seq + c) * gt, gt)
            sp_ref[pl.ds(row, gt), :] = v0
            z0 = z_ref[pl.ds(row, gt), :]
            z1 = zs_ref[pl.ds(row, gt), :]
            new.append((a1 * v0 + a2 * v1 + z0, a1 * v1 - a2 * v0 + z1))
        return tuple(new)

    zero = jnp.zeros((gt, LANES), F32)
    lax.fori_loop(0, per_seq, step, tuple((zero, zero) for _ in range(batch)), unroll=True)

    sp = jnp.concatenate([sp_ref[pl.ds(g, n_chunks, stride=gt), :] for g in range(gt)], axis=1).astype(BF16)
    d2 = jnp.concatenate([d_ref[...], d_ref[...]], axis=1)
    for q in range(t_len // 2):
        cols = slice(2 * q * LANES, (2 * q + 2) * LANES)
        k_len = (2 * q + 2) * LANES
        y = jnp.dot(u_ref[:, :k_len], panel_ref[flat - k_len:, :], preferred_element_type=F32)
        y = y + lax.dot_general(sp, outpx_ref[cols, :], (((1,), (1,)), ((), ())), preferred_element_type=F32)
        y = y + d2 * u_ref[:, cols].astype(F32)
        y_ref[pl.ds(2 * q, n_chunks, stride=t_len), :] = y[:, :LANES]
        y_ref[pl.ds(2 * q + 1, n_chunks, stride=t_len), :] = y[:, LANES:]


def _ssm(s_in, ops, layer, batch):
    lag_c, inp, outp, a1, a2, d_row = ops
    n = s_in.shape[0]
    n_chunks = n // SSM_CHUNK
    flat = SSM_CHUNK * LANES
    return pl.pallas_call(
        functools.partial(_ssm_body, batch=batch),
        grid=(SSM_LANE_TILES,),
        in_specs=[
            pl.BlockSpec((n, LANES), lambda j: (0, j)),
            pl.BlockSpec((None, None, flat, SSM_GROUP), lambda j: (layer, j, 0, 0)),
            pl.BlockSpec((None, None, flat, 2 * SSM_STATE), lambda j: (layer, j, 0, 0)),
            pl.BlockSpec((None, None, flat, 2 * SSM_STATE), lambda j: (layer, j, 0, 0)),
            pl.BlockSpec((None, None, SSM_TILE_GROUPS, 2 * SSM_STATE), lambda j: (layer, j, 0, 0)),
            pl.BlockSpec((None, None, SSM_TILE_GROUPS, 2 * SSM_STATE), lambda j: (layer, j, 0, 0)),
            pl.BlockSpec((None, 1, LANES), lambda j: (layer, 0, j)),
        ],
        out_specs=pl.BlockSpec((n, LANES), lambda j: (0, j)),
        out_shape=jax.ShapeDtypeStruct((n, SSM_WIDTH), F32),
        scratch_shapes=[
            pltpu.VMEM((n_chunks, flat), BF16),
            pltpu.VMEM((flat, 2 * LANES), BF16),
            pltpu.VMEM((flat, SSM_TILE_GROUPS * 2 * SSM_STATE), BF16),
            pltpu.VMEM((flat, SSM_TILE_GROUPS * 2 * SSM_STATE), BF16),
            pltpu.VMEM((n_chunks * SSM_TILE_GROUPS, 2 * SSM_STATE), F32),
            pltpu.VMEM((n_chunks * SSM_TILE_GROUPS, 2 * SSM_STATE), F32),
            pltpu.VMEM((n_chunks * SSM_TILE_GROUPS, 2 * SSM_STATE), F32),
        ],
        compiler_params=_params(("parallel",)),
        name="ssm",
    )(s_in, lag_c, inp, outp, a1, a2, d_row)


MERGE_ROW_CHUNK = 256


def _gate_merge_body(xn_ref, swa_ref, ssm_ref, mem_ref, wg0_ref, wg1_ref, wg2_ref, wswa_ref,
                     wga_ref, wgb_ref, wmem_ref, o_ref):
    tn = o_ref.shape[1]
    half_k = D_MODEL // 2
    wg01 = jnp.concatenate([wg0_ref[...].astype(BF16), wg1_ref[...].astype(BF16)], axis=1)
    wg2 = wg2_ref[...].astype(BF16)
    wglu = jnp.concatenate([wga_ref[...].astype(BF16), wgb_ref[...].astype(BF16)], axis=1)
    wswa = wswa_ref[...].astype(BF16)
    wmem = wmem_ref[...].astype(BF16)
    for r in range(o_ref.shape[0] // MERGE_ROW_CHUNK):
        rows = pl.ds(r * MERGE_ROW_CHUNK, MERGE_ROW_CHUNK)
        xn = xn_ref[rows, :]
        ys = ssm_ref[rows, :].astype(BF16)
        y_swa = jnp.dot(swa_ref[rows, :], wswa, preferred_element_type=F32)
        y_mem = jnp.dot(mem_ref[rows, :], wmem, preferred_element_type=F32)
        glu = jnp.dot(ys, wglu, preferred_element_type=F32)
        y_ssm = glu[:, :tn] * _sigmoid(glu[:, tn:])
        g01 = jnp.dot(xn, wg01, preferred_element_type=F32)
        g2 = (jnp.dot(xn[:, :half_k], wg2[:half_k], preferred_element_type=F32)
              + jnp.dot(xn[:, half_k:], wg2[half_k:], preferred_element_type=F32))
        merged = _sigmoid(g01[:, :tn]) * y_swa + _sigmoid(g01[:, tn:]) * y_ssm + _sigmoid(g2) * y_mem
        o_ref[rows, :] = merged.astype(BF16)


def _gate_merge(xn, o_swa, y_s, o_mem, w_in, layer, w_swa_up, w_ssm_glu, w_mem_up, *, tm=1024, tn=256):
    n = xn.shape[0]
    nj = D_MODEL // tn
    g0 = GATE_OFFSET // tn
    row = lambda i, j: (i, 0)
    return pl.pallas_call(
        _gate_merge_body,
        grid=(n // tm, nj),
        in_specs=[
            pl.BlockSpec((tm, D_MODEL), row),
            pl.BlockSpec((tm, Q_WIDTH), row),
            pl.BlockSpec((tm, SSM_WIDTH), row),
            pl.BlockSpec((tm, MEM_WIDTH), row),
            pl.BlockSpec((None, D_MODEL, tn), lambda i, j: (layer, 0, g0 + j)),
            pl.BlockSpec((None, D_MODEL, tn), lambda i, j: (layer, 0, g0 + j + nj)),
            pl.BlockSpec((None, D_MODEL, tn), lambda i, j: (layer, 0, g0 + j + 2 * nj)),
            pl.BlockSpec((None, Q_WIDTH, tn), lambda i, j: (layer, 0, j)),
            pl.BlockSpec((None, SSM_WIDTH, tn), lambda i, j: (layer, 0, j)),
            pl.BlockSpec((None, SSM_WIDTH, tn), lambda i, j: (layer, 0, j + nj)),
            pl.BlockSpec((None, MEM_WIDTH, tn), lambda i, j: (layer, 0, j)),
        ],
        out_specs=pl.BlockSpec((tm, tn), lambda i, j: (i, j)),
        out_shape=jax.ShapeDtypeStruct((n, D_MODEL), BF16),
        compiler_params=_params(("parallel", "arbitrary")),
        name="gate_merge",
    )(xn, o_swa, y_s, o_mem, w_in, w_in, w_in, w_swa_up, w_ssm_glu, w_ssm_glu, w_mem_up)


def _out_proj_body(h_ref, m_ref, w_ref, o_ref, wb_ref):
    @pl.when(pl.program_id(0) == 0)
    def _():
        wb_ref[...] = w_ref[...].astype(BF16)

    half = D_MODEL // 2
    for c in range(2):
        cols = pl.ds(c * half, half)
        o_ref[:, cols] = h_ref[:, cols] + jnp.dot(m_ref[...], wb_ref[:, cols], preferred_element_type=F32)


def _out_proj(h, merged, w_out, layer, *, tm=512):
    n = h.shape[0]
    return pl.pallas_call(
        _out_proj_body,
        grid=(n // tm,),
        in_specs=[
            pl.BlockSpec((tm, D_MODEL), lambda i: (i, 0)),
            pl.BlockSpec((tm, D_MODEL), lambda i: (i, 0)),
            pl.BlockSpec((None, D_MODEL, D_MODEL), lambda i: (layer, 0, 0), pipeline_mode=pl.Buffered(1)),
        ],
        out_specs=pl.BlockSpec((tm, D_MODEL), lambda i: (i, 0)),
        out_shape=jax.ShapeDtypeStruct((n, D_MODEL), F32),
        scratch_shapes=[pltpu.VMEM((D_MODEL, D_MODEL), BF16)],
        compiler_params=_params(("arbitrary",)),
        name="out_proj",
    )(h, merged, w_out)


def kernel(x, mem, ffn1_norm, ffn1_w_in, ffn1_w_out, mix_norm, mem_norm, w_in, sinks, w_mem_kv, lam_re, lam_im, log_dt, b_re, b_im, c_re, c_im, d_skip, w_ssm_glu, w_swa_up, w_mem_up, w_out, ffn2_norm, ffn2_w_in, ffn2_w_out, final_norm):
    batch, seq = x.shape[0], x.shape[1]
    n = batch * seq
    h = x.reshape(n, D_MODEL)
    mem2 = mem.reshape(batch * N_MEM, D_MODEL)
    final_w = final_norm.reshape(1, D_MODEL)
    ssm_ops = jax.vmap(_ssm_operators)(lam_re, lam_im, log_dt, b_re, b_im, c_re, c_im, d_skip)
    mem_kv = _mem_kv(mem2, mem_norm, w_mem_kv)
    for l in range(DEPTH):
        mix_w = mix_norm[l].reshape(1, D_MODEL)

        h = _ffn(h, ffn1_norm[l].reshape(1, D_MODEL), ffn1_w_in, ffn1_w_out, final_w, l, apply_final_norm=False)
        qkv, s_in, mq, xn = _mix_proj(h, mix_w, w_in, l)
        o_swa = _swa(qkv, sinks[l], batch, seq)
        o_mem = _mem_attn(mq, mem_kv, l, batch, seq)
        y_s = _ssm(s_in, ssm_ops, l, batch)
        merged = _gate_merge(xn, o_swa, y_s, o_mem, w_in, l, w_swa_up, w_ssm_glu, w_mem_up)
        h = _out_proj(h, merged, w_out, l)
        h = _ffn(h, ffn2_norm[l].reshape(1, D_MODEL), ffn2_w_in, ffn2_w_out, final_w, l,
                 apply_final_norm=(l == DEPTH - 1))
    return h.reshape(batch, seq, D_MODEL)
```
